```python
import math
import jax, jax.numpy as jnp
from jax import lax
import numpy as np

D_MODEL = 1024
BATCH = 1
SEQ = 16384
DEPTH = 4

N_A = DEPTH // 2
N_B = DEPTH - N_A
Q_BLOCK = 128
RMS_EPS = 1e-6

FOX_HEADS = 16
FOX_HEAD_DIM = D_MODEL // FOX_HEADS
FOX_WIDTH = FOX_HEADS * FOX_HEAD_DIM

MLA_HEADS = 16
MLA_NOPE = 64
MLA_ROPE = 32
MLA_V = 64
KV_RANK = 4 * MLA_V
Q_RANK = 12 * MLA_V
ROPE_THETA = 10000.0

D_FF = ((8 * D_MODEL // 3 + 255) // 256) * 256

kernel_name = "yoco_fox_mla_hybrid"


def rmsnorm(x, g):
    xf = x.astype(jnp.float32)
    inv = lax.rsqrt(jnp.mean(xf * xf, axis=-1, keepdims=True) + RMS_EPS)
    return (xf * inv).astype(x.dtype) * g


def rope(x, positions):
    d = x.shape[-1]
    inv_freq = ROPE_THETA ** (-jnp.arange(0, d // 2, dtype=jnp.float32) * 2.0 / d)
    ang = positions.astype(jnp.float32)[..., None] * inv_freq
    cos = jnp.cos(ang)[:, :, None, :]
    sin = jnp.sin(ang)[:, :, None, :]
    xf = x.astype(jnp.float32)
    x1, x2 = xf[..., : d // 2], xf[..., d // 2:]
    out = jnp.concatenate([x1 * cos - x2 * sin, x1 * sin + x2 * cos], axis=-1)
    return out.astype(x.dtype)


def causal_block_attention(q, k, v, scale, log_f_cum=None):
    B, S, H, dq = q.shape
    nb = S // Q_BLOCK
    qb = q.reshape(B, nb, Q_BLOCK, H, dq).transpose(1, 0, 2, 3, 4)
    k_pos = jnp.arange(S)
    if log_f_cum is not None:
        cq_blocks = log_f_cum.reshape(B, nb, Q_BLOCK, H).transpose(1, 0, 2, 3)
        ck = log_f_cum.transpose(0, 2, 1)
    else:
        cq_blocks, ck = None, None

    def one_block(args):
        i, q_blk, cq = args
        s = jnp.einsum('bqhd,bkhd->bhqk', q_blk, k,
                       preferred_element_type=jnp.float32) * scale
        if cq is not None:
            s = s + (cq.transpose(0, 2, 1)[..., None] - ck[:, :, None, :])
        q_pos = i * Q_BLOCK + jnp.arange(Q_BLOCK)
        mask = k_pos[None, :] <= q_pos[:, None]
        s = jnp.where(mask[None, None], s, -jnp.inf)
        p = jax.nn.softmax(s, axis=-1)
        return jnp.einsum('bhqk,bkhd->bqhd', p.astype(v.dtype), v)

    out = lax.map(one_block, (jnp.arange(nb), qb, cq_blocks))
    return out.transpose(1, 0, 2, 3, 4).reshape(B, S, H, v.shape[-1])


def fox_attention(xn, w_in, b_f, w_o):
    B, S, _ = xn.shape
    proj = xn @ w_in
    q = proj[..., :FOX_WIDTH].reshape(B, S, FOX_HEADS, FOX_HEAD_DIM)
    k = proj[..., FOX_WIDTH:2 * FOX_WIDTH].reshape(B, S, FOX_HEADS, FOX_HEAD_DIM)
    v = proj[..., 2 * FOX_WIDTH:3 * FOX_WIDTH].reshape(B, S, FOX_HEADS, FOX_HEAD_DIM)
    f_logit = proj[..., 3 * FOX_WIDTH:] + b_f
    log_f = jax.nn.log_sigmoid(f_logit.astype(jnp.float32))
    cum = jnp.cumsum(log_f, axis=1)
    o = causal_block_attention(q, k, v, 1.0 / math.sqrt(FOX_HEAD_DIM), cum)
    return o.reshape(B, S, FOX_WIDTH) @ w_o


def mla_shared_kv(h, positions, kv_norm, w_kv_a, ckv_norm, w_uk, w_uv):
    B, S, _ = h.shape
    hn = rmsnorm(h, kv_norm)
    a = hn @ w_kv_a
    c_kv = rmsnorm(a[..., :KV_RANK], ckv_norm)
    k_rope = rope(a[..., KV_RANK:][:, :, None, :], positions)
    k_nope = jnp.einsum('bsr,rhd->bshd', c_kv, w_uk)
    v = jnp.einsum('bsr,rhd->bshd', c_kv, w_uv)
    k = jnp.concatenate(
        [k_nope, jnp.broadcast_to(k_rope, (B, S, MLA_HEADS, MLA_ROPE))], axis=-1)
    return k, v


def mla_attention(xn, positions, k, v, w_dq, cq_norm, w_uq, w_o):
    B, S, _ = xn.shape
    c_q = rmsnorm(xn @ w_dq, cq_norm)
    q = (c_q @ w_uq).reshape(B, S, MLA_HEADS, MLA_NOPE + MLA_ROPE)
    q = jnp.concatenate([q[..., :MLA_NOPE], rope(q[..., MLA_NOPE:], positions)], axis=-1)
    o = causal_block_attention(q, k, v, 1.0 / math.sqrt(MLA_NOPE + MLA_ROPE))
    return o.reshape(B, S, MLA_HEADS * MLA_V) @ w_o


def swiglu(xn, w_gate, w_up, w_down):
    return (jax.nn.silu(xn @ w_gate) * (xn @ w_up)) @ w_down


def setup_inputs(seed: int = 0) -> dict:
    key = jax.random.key(seed)
    ks = jax.random.split(key, 24)

    def nrm(k, shape, fan_in):
        return jax.random.normal(k, shape, jnp.float32) * (fan_in ** -0.5)

    def gain(k, shape):
        return 1.0 + 0.02 * jax.random.normal(k, shape, jnp.float32)

    x = jax.random.normal(ks[0], (BATCH, SEQ, D_MODEL), jnp.float32)
    positions = jnp.broadcast_to(jnp.arange(SEQ, dtype=jnp.int32), (BATCH, SEQ))
    return {
        "x": x,
        "positions": positions,
        "attn_norm": gain(ks[1], (DEPTH, D_MODEL)),
        "ffn_norm": gain(ks[2], (DEPTH, D_MODEL)),
        "w_gate": nrm(ks[3], (DEPTH, D_MODEL, D_FF), D_MODEL),
        "w_up": nrm(ks[4], (DEPTH, D_MODEL, D_FF), D_MODEL),
        "w_down": nrm(ks[5], (DEPTH, D_FF, D_MODEL), D_FF),
        "fox_w_in": nrm(ks[6], (N_A, D_MODEL, 3 * FOX_WIDTH + FOX_HEADS), D_MODEL),
        "fox_b_f": 3.0 + 0.5 * jax.random.normal(ks[7], (N_A, FOX_HEADS), jnp.float32),
        "fox_w_o": nrm(ks[8], (N_A, FOX_WIDTH, D_MODEL), FOX_WIDTH),
        "kv_norm": gain(ks[9], (D_MODEL,)),
        "w_kv_a": nrm(ks[10], (D_MODEL, KV_RANK + MLA_ROPE), D_MODEL),
        "ckv_norm": gain(ks[11], (KV_RANK,)),
        "w_uk": nrm(ks[12], (KV_RANK, MLA_HEADS, MLA_NOPE), KV_RANK),
        "w_uv": nrm(ks[13], (KV_RANK, MLA_HEADS, MLA_V), KV_RANK),
        "mla_w_dq": nrm(ks[14], (N_B, D_MODEL, Q_RANK), D_MODEL),
        "cq_norm": gain(ks[15], (N_B, Q_RANK)),
        "mla_w_uq": nrm(ks[16], (N_B, Q_RANK, MLA_HEADS * (MLA_NOPE + MLA_ROPE)), Q_RANK),
        "mla_w_o": nrm(ks[17], (N_B, MLA_HEADS * MLA_V, D_MODEL), MLA_HEADS * MLA_V),
        "final_norm": gain(ks[18], (D_MODEL,)),
    }


def reference(x, positions, attn_norm, ffn_norm, w_gate, w_up, w_down,
              fox_w_in, fox_b_f, fox_w_o, kv_norm, w_kv_a, ckv_norm, w_uk, w_uv,
              mla_w_dq, cq_norm, mla_w_uq, mla_w_o, final_norm):
    h = x
    k_shared, v_shared = None, None
    for l in range(DEPTH):
        if l == N_A:
            k_shared, v_shared = mla_shared_kv(h, positions, kv_norm, w_kv_a,
                                               ckv_norm, w_uk, w_uv)
        xn = rmsnorm(h, attn_norm[l])
        if l < N_A:
            h = h + fox_attention(xn, fox_w_in[l], fox_b_f[l], fox_w_o[l])
        else:
            j = l - N_A
            h = h + mla_attention(xn, positions, k_shared, v_shared,
                                  mla_w_dq[j], cq_norm[j], mla_w_uq[j], mla_w_o[j])
        h = h + swiglu(rmsnorm(h, ffn_norm[l]), w_gate[l], w_up[l], w_down[l])
    return rmsnorm(h, final_norm)
```

```python
import functools
import math

import numpy as np
import jax
import jax.numpy as jnp
from jax import lax
from jax.experimental import pallas as pl
from jax.experimental.pallas import tpu as pltpu

F32 = jnp.float32
BF16 = jnp.bfloat16

D_MODEL = 1024
SEQ = 16384
DEPTH = 4
N_A = DEPTH // 2
RMS_EPS = 1e-6
HEADS = 16
HEAD_DIM = 64
MLA_NOPE = 64
MLA_ROPE = 32
HALF_ROPE = MLA_ROPE // 2
KV_RANK = 256
Q_RANK = 768
ROPE_THETA = 10000.0
D_FF = 2816

LANES = 128
SLOT = 128
AUG = HEADS * SLOT
LOG2E = math.log2(math.e)
NEG = -1e30

TM = 512
TQ = 256
TKC = 128
FF_CHUNK = D_FF // 2
VMEM_LIMIT = 56 * 1024 * 1024


def _dot(a, b):
    return jnp.dot(a, b, preferred_element_type=F32)


def _rms(x, g):
    inv = lax.rsqrt(jnp.mean(x * x, axis=-1, keepdims=True) + RMS_EPS)
    return (x * inv) * g


def _split3(x):
    hi = x.astype(BF16)
    r = x - hi.astype(F32)
    mid = r.astype(BF16)
    lo = (r - mid.astype(F32)).astype(BF16)
    return hi, mid, lo


def _const_spec(shape):
    nd = len(shape)
    return pl.BlockSpec(shape, lambda *_: (0,) * nd, pipeline_mode=pl.Buffered(1))


def _params(sem):
    return pltpu.CompilerParams(dimension_semantics=sem, vmem_limit_bytes=VMEM_LIMIT)


def _fox_proj_kernel(h_ref, g_ref, wqT_ref, wk_ref, wvT_ref, wf_ref, bf_ref,
                     pk_ref, pqT_ref, tri_ref, qT_ref, k_ref, vT_ref, carry_ref):
    @pl.when(pl.program_id(0) == 0)
    def _():
        carry_ref[...] = jnp.zeros_like(carry_ref)

    xn = _rms(h_ref[...], g_ref[...]).astype(BF16)
    xnT = xn.T

    f = _dot(xn, wf_ref[...]) + bf_ref[...]
    logf = (jnp.minimum(f, 0.0) - jnp.log1p(jnp.exp(-jnp.abs(f)))) * LOG2E
    hi, mid, lo = _split3(logf)
    tri = tri_ref[...]
    cum = _dot(tri, hi) + _dot(tri, mid) + _dot(tri, lo) + carry_ref[...]
    carry_ref[...] = cum[TM - 1:TM, :]

    lane = lax.broadcasted_iota(jnp.int32, cum.shape, 1)
    c = jnp.where(lane < HEADS, cum, 1.0)
    c3 = jnp.concatenate(_split3(c), axis=1)
    c3T = jnp.concatenate(_split3(c.T), axis=0)

    k = _dot(xn, wk_ref[...]) + _dot(c3, pk_ref[...])
    k_ref[...] = k.astype(BF16)
    qT = _dot(wqT_ref[...], xnT) * (LOG2E / math.sqrt(HEAD_DIM)) + _dot(pqT_ref[...], c3T)
    qT_ref[...] = qT.astype(BF16)
    vT_ref[...] = _dot(wvT_ref[...], xnT).astype(BF16)


def _fox_proj(h, g, wqT, wk, wvT, wf, bf, pk, pqT, tri):
    n = SEQ // TM
    return pl.pallas_call(
        _fox_proj_kernel,
        grid=(n,),
        in_specs=[
            pl.BlockSpec((TM, D_MODEL), lambda i: (i, 0)),
            _const_spec((1, D_MODEL)),
            _const_spec((AUG, D_MODEL)),
            _const_spec((D_MODEL, AUG)),
            _const_spec((D_MODEL, D_MODEL)),
            _const_spec((D_MODEL, LANES)),
            _const_spec((1, LANES)),
            _const_spec((3 * LANES, AUG)),
            _const_spec((AUG, 3 * LANES)),
            _const_spec((TM, TM)),
        ],
        out_specs=[
            pl.BlockSpec((AUG, TM), lambda i: (0, i)),
            pl.BlockSpec((TM, AUG), lambda i: (i, 0)),
            pl.BlockSpec((D_MODEL, TM), lambda i: (0, i)),
        ],
        out_shape=[
            jax.ShapeDtypeStruct((AUG, SEQ), BF16),
            jax.ShapeDtypeStruct((SEQ, AUG), BF16),
            jax.ShapeDtypeStruct((D_MODEL, SEQ), BF16),
        ],
        scratch_shapes=[pltpu.VMEM((1, LANES), F32)],
        compiler_params=_params(("arbitrary",)),
        name="fox_proj",
    )(h, g, wqT, wk, wvT, wf, bf, pk, pqT, tri)


def _attn_kernel(qT_ref, k_ref, vT_ref, o_ref):
    i = pl.program_id(1)
    qT = qT_ref[...]

    def sub(off, carry, mask):
        m, l, acc = carry
        kc = k_ref[pl.ds(off, TKC), :]
        vc = vT_ref[:, pl.ds(off, TKC)]
        s = _dot(kc, qT)
        if mask is not None:
            s = jnp.where(mask, s, NEG)
        m_new = jnp.maximum(m, jnp.max(s, axis=0, keepdims=True))
        alpha = jnp.exp2(m - m_new)
        p = jnp.exp2(s - m_new)
        l = alpha * l + jnp.sum(p, axis=0, keepdims=True)
        acc = alpha * acc + _dot(vc, p.astype(BF16))
        return m_new, l, acc

    def body(j, carry):
        base = pl.multiple_of(j * TQ, TQ)
        for u in range(TQ // TKC):
            carry = sub(base + u * TKC, carry, None)
        return carry

    init = (jnp.full((1, TQ), NEG, F32), jnp.zeros((1, TQ), F32),
            jnp.zeros((HEAD_DIM, TQ), F32))
    carry = lax.fori_loop(0, i, body, init)

    row = lax.broadcasted_iota(jnp.int32, (TKC, TQ), 0)
    col = lax.broadcasted_iota(jnp.int32, (TKC, TQ), 1)
    base = pl.multiple_of(i * TQ, TQ)
    for u in range(TQ // TKC):
        carry = sub(base + u * TKC, carry, row + u * TKC <= col)
    _, l, acc = carry
    o_ref[...] = (acc / l).astype(BF16)


def _attention(qT, k, vT):
    return pl.pallas_call(
        _attn_kernel,
        grid=(HEADS, SEQ // TQ),
        in_specs=[
            pl.BlockSpec((None, SLOT, TQ), lambda h, i: (h, 0, i)),
            pl.BlockSpec((SEQ, SLOT), lambda h, i: (0, h)),
            pl.BlockSpec((HEAD_DIM, SEQ), lambda h, i: (h, 0)),
        ],
        out_specs=pl.BlockSpec((HEAD_DIM, TQ), lambda h, i: (h, i)),
        out_shape=jax.ShapeDtypeStruct((HEADS * HEAD_DIM, SEQ), BF16),
        compiler_params=_params(("arbitrary", "arbitrary")),
        name="attention",
    )(qT, k, vT)


def _out_ffn_kernel(oT_ref, h_ref, wo_ref, g_ref, wg_ref, wu_ref, wd_ref, gfin_ref,
                    out_ref, *, final):
    o = oT_ref[...].T
    h1 = h_ref[...] + _dot(o, wo_ref[...])
    xn = _rms(h1, g_ref[...]).astype(BF16)
    acc = h1
    for c in range(D_FF // FF_CHUNK):
        sl = slice(c * FF_CHUNK, (c + 1) * FF_CHUNK)
        gate = _dot(xn, wg_ref[:, sl])
        up = _dot(xn, wu_ref[:, sl])
        act = (gate * (1.0 / (1.0 + jnp.exp(-gate))) * up).astype(BF16)
        acc = acc + _dot(act, wd_ref[sl, :])
    if final:
        acc = _rms(acc, gfin_ref[...])
    out_ref[...] = acc


def _out_ffn(oT, h, wo, g, wg, wu, wd, gfin, final):
    n = SEQ // TM
    return pl.pallas_call(
        functools.partial(_out_ffn_kernel, final=final),
        grid=(n,),
        in_specs=[
            pl.BlockSpec((D_MODEL, TM), lambda i: (0, i)),
            pl.BlockSpec((TM, D_MODEL), lambda i: (i, 0)),
            _const_spec((D_MODEL, D_MODEL)),
            _const_spec((1, D_MODEL)),
            _const_spec((D_MODEL, D_FF)),
            _const_spec((D_MODEL, D_FF)),
            _const_spec((D_FF, D_MODEL)),
            _const_spec((1, D_MODEL)),
        ],
        out_specs=pl.BlockSpec((TM, D_MODEL), lambda i: (i, 0)),
        out_shape=jax.ShapeDtypeStruct((SEQ, D_MODEL), F32),
        compiler_params=_params(("arbitrary",)),
        name="out_ffn",
    )(oT, h, wo, g, wg, wu, wd, gfin)


def _rope_table_kernel(pos_ref, invf_ref, cos_ref, sin_ref, cosT_ref, sinT_ref):
    ang = pos_ref[...] * invf_ref[...]
    cos = jnp.cos(ang)
    sin = jnp.sin(ang)
    cos_ref[...] = cos
    sin_ref[...] = sin
    cosT_ref[...] = cos.T
    sinT_ref[...] = sin.T


def _rope_tables(pos_col, invf):
    n = SEQ // TM
    row = pl.BlockSpec((TM, LANES), lambda i: (i, 0))
    colT = pl.BlockSpec((LANES, TM), lambda i: (0, i))
    return pl.pallas_call(
        _rope_table_kernel,
        grid=(n,),
        in_specs=[pl.BlockSpec((TM, 1), lambda i: (i, 0)), _const_spec((1, LANES))],
        out_specs=[row, row, colT, colT],
        out_shape=[jax.ShapeDtypeStruct((SEQ, LANES), F32)] * 2
        + [jax.ShapeDtypeStruct((LANES, SEQ), F32)] * 2,
        compiler_params=_params(("arbitrary",)),
        name="rope_tables",
    )(pos_col, invf)


def _mla_kv_kernel(h_ref, g_ref, wa_ref, gckv_ref, cos_ref, sin_ref, wuk_ref,
                   p1_ref, p2_ref, wuvT_ref, k_ref, vT_ref):
    hn = _rms(h_ref[...], g_ref[...]).astype(BF16)
    a = _dot(hn, wa_ref[...])
    c = _rms(a[:, :KV_RANK], gckv_ref[...]).astype(BF16)
    x1 = a[:, KV_RANK:KV_RANK + LANES]
    x2 = a[:, KV_RANK + LANES:]
    cos = cos_ref[...]
    sin = sin_ref[...]
    r1 = (x1 * cos - x2 * sin).astype(BF16)
    r2 = (x1 * sin + x2 * cos).astype(BF16)
    k = _dot(c, wuk_ref[...]) + _dot(r1, p1_ref[...]) + _dot(r2, p2_ref[...])
    k_ref[...] = k.astype(BF16)
    vT_ref[...] = _dot(wuvT_ref[...], c.T).astype(BF16)


def _mla_kv(h, g, wa, gckv, cos, sin, wuk, p1, p2, wuvT):
    n = SEQ // TM
    return pl.pallas_call(
        _mla_kv_kernel,
        grid=(n,),
        in_specs=[
            pl.BlockSpec((TM, D_MODEL), lambda i: (i, 0)),
            _const_spec((1, D_MODEL)),
            _const_spec((D_MODEL, 4 * LANES)),
            _const_spec((1, KV_RANK)),
            pl.BlockSpec((TM, LANES), lambda i: (i, 0)),
            pl.BlockSpec((TM, LANES), lambda i: (i, 0)),
            _const_spec((KV_RANK, AUG)),
            _const_spec((LANES, AUG)),
            _const_spec((LANES, AUG)),
            _const_spec((D_MODEL, KV_RANK)),
        ],
        out_specs=[
            pl.BlockSpec((TM, AUG), lambda i: (i, 0)),
            pl.BlockSpec((D_MODEL, TM), lambda i: (0, i)),
        ],
        out_shape=[
            jax.ShapeDtypeStruct((SEQ, AUG), BF16),
            jax.ShapeDtypeStruct((D_MODEL, SEQ), BF16),
        ],
        compiler_params=_params(("arbitrary",)),
        name="mla_kv",
    )(h, g, wa, gckv, cos, sin, wuk, p1, p2, wuvT)


def _mla_q_kernel(h_ref, g_ref, wdq_ref, gcq_ref, wnT_ref, wx1T_ref, wx2T_ref,
                  cosT_ref, sinT_ref, qT_ref):
    scale = LOG2E / math.sqrt(MLA_NOPE + MLA_ROPE)
    xn = _rms(h_ref[...], g_ref[...]).astype(BF16)
    cq = _rms(_dot(xn, wdq_ref[...]), gcq_ref[...]).astype(BF16)
    cqT = cq.T
    nT = _dot(wnT_ref[...], cqT) * scale
    x1T = _dot(wx1T_ref[...], cqT)
    x2T = _dot(wx2T_ref[...], cqT)
    cosT = jnp.concatenate([cosT_ref[...]] * HEADS, axis=0)
    sinT = jnp.concatenate([sinT_ref[...]] * HEADS, axis=0)
    r1 = (x1T * cosT - x2T * sinT) * scale
    r2 = (x1T * sinT + x2T * cosT) * scale
    zeros = jnp.zeros((SLOT - MLA_NOPE - MLA_ROPE, TM), BF16)
    for h in range(HEADS):
        qT_ref[h, 0:MLA_NOPE, :] = nT[h * MLA_NOPE:(h + 1) * MLA_NOPE].astype(BF16)
        qT_ref[h, MLA_NOPE:MLA_NOPE + HALF_ROPE, :] = (
            r1[h * HALF_ROPE:(h + 1) * HALF_ROPE].astype(BF16))
        qT_ref[h, MLA_NOPE + HALF_ROPE:MLA_NOPE + MLA_ROPE, :] = (
            r2[h * HALF_ROPE:(h + 1) * HALF_ROPE].astype(BF16))
        qT_ref[h, MLA_NOPE + MLA_ROPE:, :] = zeros


def _mla_q(h, g, wdq, gcq, wnT, wx1T, wx2T, cosT, sinT):
    n = SEQ // TM
    return pl.pallas_call(
        _mla_q_kernel,
        grid=(n,),
        in_specs=[
            pl.BlockSpec((TM, D_MODEL), lambda i: (i, 0)),
            _const_spec((1, D_MODEL)),
            _const_spec((D_MODEL, Q_RANK)),
            _const_spec((1, Q_RANK)),
            _const_spec((HEADS * MLA_NOPE, Q_RANK)),
            _const_spec((HEADS * HALF_ROPE, Q_RANK)),
            _const_spec((HEADS * HALF_ROPE, Q_RANK)),
            pl.BlockSpec((HALF_ROPE, TM), lambda i: (0, i)),
            pl.BlockSpec((HALF_ROPE, TM), lambda i: (0, i)),
        ],
        out_specs=pl.BlockSpec((HEADS, SLOT, TM), lambda i: (0, 0, i)),
        out_shape=jax.ShapeDtypeStruct((HEADS, SLOT, SEQ), BF16),
        compiler_params=_params(("arbitrary",)),
        name="mla_q",
    )(h, g, wdq, gcq, wnT, wx1T, wx2T, cosT, sinT)


def _pad_heads_cols(w, width):
    k = w.shape[0]
    w = w.reshape(k, HEADS, width)
    w = jnp.pad(w, ((0, 0), (0, 0), (0, SLOT - width)))
    return w.reshape(k, AUG)


def _fox_bias_placement():
    pk = np.zeros((3 * LANES, AUG), np.float32)
    for h in range(HEADS):
        for j in range(3):
            pk[HEADS, h * SLOT + HEAD_DIM + j] = 1.0
            pk[j * LANES + h, h * SLOT + HEAD_DIM + 3 + j] = -1.0
    pq = np.zeros((AUG, 3 * LANES), np.float32)
    for h in range(HEADS):
        for j in range(3):
            pq[h * SLOT + HEAD_DIM + j, j * LANES + h] = 1.0
            pq[h * SLOT + HEAD_DIM + 3 + j, HEADS] = 1.0
    return jnp.asarray(pk, BF16), jnp.asarray(pq, BF16)


def _rope_placement():
    p1 = np.zeros((LANES, AUG), np.float32)
    p2 = np.zeros((LANES, AUG), np.float32)
    for h in range(HEADS):
        for j in range(HALF_ROPE):
            p1[j, h * SLOT + MLA_NOPE + j] = 1.0
            p2[j, h * SLOT + MLA_NOPE + HALF_ROPE + j] = 1.0
    return jnp.asarray(p1, BF16), jnp.asarray(p2, BF16)


def kernel(x, positions, attn_norm, ffn_norm, w_gate, w_up, w_down, fox_w_in, fox_b_f,
           fox_w_o, kv_norm, w_kv_a, ckv_norm, w_uk, w_uv, mla_w_dq, cq_norm, mla_w_uq,
           mla_w_o, final_norm):
    assert x.shape == (1, SEQ, D_MODEL)
    h = x.reshape(SEQ, D_MODEL)
    W = HEADS * HEAD_DIM
    pk, pqT = _fox_bias_placement()
    p1, p2 = _rope_placement()
    tri = jnp.asarray(np.tril(np.ones((TM, TM), np.float32)), BF16)
    row = lambda v: v.reshape(1, -1)

    def ffn_args(l):
        return (row(ffn_norm[l]), w_gate[l].astype(BF16), w_up[l].astype(BF16),
                w_down[l].astype(BF16), row(final_norm))

    for l in range(N_A):
        w_in = fox_w_in[l]
        wqT = _pad_heads_cols(w_in[:, :W], HEAD_DIM).T.astype(BF16)
        wk = _pad_heads_cols(w_in[:, W:2 * W], HEAD_DIM).astype(BF16)
        wvT = w_in[:, 2 * W:3 * W].T.astype(BF16)
        wf = jnp.pad(w_in[:, 3 * W:], ((0, 0), (0, LANES - HEADS))).astype(BF16)
        bf = jnp.pad(fox_b_f[l], (0, LANES - HEADS)).reshape(1, LANES)
        qT, k, vT = _fox_proj(h, row(attn_norm[l]), wqT, wk, wvT, wf, bf, pk, pqT, tri)
        oT = _attention(qT.reshape(HEADS, SLOT, SEQ), k, vT)
        h = _out_ffn(oT, h, fox_w_o[l].astype(BF16), *ffn_args(l), final=False)

    invf = ROPE_THETA ** (-jnp.arange(0, HALF_ROPE, dtype=F32) * 2.0 / MLA_ROPE)
    invf = jnp.pad(invf, (0, LANES - HALF_ROPE)).reshape(1, LANES)
    pos_col = positions.reshape(SEQ, 1).astype(F32)
    cos, sin, cosT, sinT = _rope_tables(pos_col, invf)
    wa = jnp.concatenate([
        w_kv_a[:, :KV_RANK],
        jnp.pad(w_kv_a[:, KV_RANK:KV_RANK + HALF_ROPE], ((0, 0), (0, LANES - HALF_ROPE))),
        jnp.pad(w_kv_a[:, KV_RANK + HALF_ROPE:], ((0, 0), (0, LANES - HALF_ROPE))),
    ], axis=1).astype(BF16)
    wuk = _pad_heads_cols(w_uk.reshape(KV_RANK, HEADS * MLA_NOPE), MLA_NOPE).astype(BF16)
    wuvT = w_uv.reshape(KV_RANK, HEADS * HEAD_DIM).T.astype(BF16)
    k_sh, vT_sh = _mla_kv(h, row(kv_norm), wa, row(ckv_norm), cos, sin, wuk, p1, p2, wuvT)

    for l in range(N_A, DEPTH):
        j = l - N_A
        wuq = mla_w_uq[j].reshape(Q_RANK, HEADS, MLA_NOPE + MLA_ROPE)
        wnT = wuq[:, :, :MLA_NOPE].reshape(Q_RANK, -1).T.astype(BF16)
        wx1T = wuq[:, :, MLA_NOPE:MLA_NOPE + HALF_ROPE].reshape(Q_RANK, -1).T.astype(BF16)
        wx2T = wuq[:, :, MLA_NOPE + HALF_ROPE:].reshape(Q_RANK, -1).T.astype(BF16)
        qT = _mla_q(h, row(attn_norm[l]), mla_w_dq[j].astype(BF16), row(cq_norm[j]),
                    wnT, wx1T, wx2T, cosT, sinT)
        oT = _attention(qT, k_sh, vT_sh)
        h = _out_ffn(oT, h, mla_w_o[j].astype(BF16), *ffn_args(l), final=(l == DEPTH - 1))

    return h.reshape(1, SEQ, D_MODEL)
```

```python
import functools
import math

import numpy as np
import jax
import jax.numpy as jnp
from jax import lax
from jax.experimental import pallas as pl
from jax.experimental.pallas import tpu as pltpu

F32 = jnp.float32
BF16 = jnp.bfloat16

D_MODEL = 1024
SEQ = 16384
DEPTH = 4
N_A = DEPTH // 2
RMS_EPS = 1e-6
HEADS = 16
HEAD_DIM = 64
MLA_NOPE = 64
MLA_ROPE = 32
HALF_ROPE = MLA_ROPE // 2
KV_RANK = 256
Q_RANK = 768
ROPE_THETA = 10000.0
D_FF = 2816

LANES = 128
SLOT = 128
AUG = HEADS * SLOT
LOG2E = math.log2(math.e)
NEG = -1e30

TM = 512
TQ = 512
CH = TQ // 2
FF_CHUNK = D_FF // 2
VMEM_LIMIT = 56 * 1024 * 1024


def _dot(a, b):
    return jnp.dot(a, b, preferred_element_type=F32)


def _rms(x, g):
    inv = lax.rsqrt(jnp.mean(x * x, axis=-1, keepdims=True) + RMS_EPS)
    return (x * inv) * g


def _split3(x):
    hi = x.astype(BF16)
    r = x - hi.astype(F32)
    mid = r.astype(BF16)
    lo = (r - mid.astype(F32)).astype(BF16)
    return hi, mid, lo


def _const_spec(shape):
    nd = len(shape)
    return pl.BlockSpec(shape, lambda *_: (0,) * nd, pipeline_mode=pl.Buffered(1))


def _params(sem):
    return pltpu.CompilerParams(dimension_semantics=sem, vmem_limit_bytes=VMEM_LIMIT)


def _fox_proj_kernel(h_ref, g_ref, wqT_ref, wk_ref, wvT_ref, wf_ref, bf_ref,
                     pk_ref, pqT_ref, tri_ref, qT_ref, k_ref, vT_ref, carry_ref):
    @pl.when(pl.program_id(0) == 0)
    def _():
        carry_ref[...] = jnp.zeros_like(carry_ref)

    xn = _rms(h_ref[...], g_ref[...]).astype(BF16)
    xnT = xn.T

    f = _dot(xn, wf_ref[...]) + bf_ref[...]
    logf = (jnp.minimum(f, 0.0) - jnp.log1p(jnp.exp(-jnp.abs(f)))) * LOG2E
    hi, mid, lo = _split3(logf)
    tri = tri_ref[...]
    cum = _dot(tri, hi) + _dot(tri, mid) + _dot(tri, lo) + carry_ref[...]
    carry_ref[...] = cum[TM - 1:TM, :]

    lane = lax.broadcasted_iota(jnp.int32, cum.shape, 1)
    c = jnp.where(lane < HEADS, cum, 1.0)
    c3 = jnp.concatenate(_split3(c), axis=1)
    c3T = jnp.concatenate(_split3(c.T), axis=0)

    k = _dot(xn, wk_ref[...]) + _dot(c3, pk_ref[...])
    k_ref[...] = k.astype(BF16)
    qT = _dot(wqT_ref[...], xnT) * (LOG2E / math.sqrt(HEAD_DIM)) + _dot(pqT_ref[...], c3T)
    qT_ref[...] = qT.astype(BF16)
    vT_ref[...] = _dot(wvT_ref[...], xnT).astype(BF16)


def _fox_proj(h, g, wqT, wk, wvT, wf, bf, pk, pqT, tri):
    n = SEQ // TM
    return pl.pallas_call(
        _fox_proj_kernel,
        grid=(n,),
        in_specs=[
            pl.BlockSpec((TM, D_MODEL), lambda i: (i, 0)),
            _const_spec((1, D_MODEL)),
            _const_spec((AUG, D_MODEL)),
            _const_spec((D_MODEL, AUG)),
            _const_spec((D_MODEL, D_MODEL)),
            _const_spec((D_MODEL, LANES)),
            _const_spec((1, LANES)),
            _const_spec((3 * LANES, AUG)),
            _const_spec((AUG, 3 * LANES)),
            _const_spec((TM, TM)),
        ],
        out_specs=[
            pl.BlockSpec((AUG, TM), lambda i: (0, i)),
            pl.BlockSpec((TM, AUG), lambda i: (i, 0)),
            pl.BlockSpec((D_MODEL, TM), lambda i: (0, i)),
        ],
        out_shape=[
            jax.ShapeDtypeStruct((AUG, SEQ), BF16),
            jax.ShapeDtypeStruct((SEQ, AUG), BF16),
            jax.ShapeDtypeStruct((D_MODEL, SEQ), BF16),
        ],
        scratch_shapes=[pltpu.VMEM((1, LANES), F32)],
        compiler_params=_params(("arbitrary",)),
        name="fox_proj",
    )(h, g, wqT, wk, wvT, wf, bf, pk, pqT, tri)


def _attn_kernel(qT_ref, k_ref, vT_ref, o_ref, sa_scr, sb_scr, smax_scr, m_scr, l_scr, acc_scr):
    i = pl.program_id(1)
    qT = qT_ref[...]
    groups = CH // 8

    def qk(j, s_ref, diag_row0):
        off = pl.multiple_of(j * CH, CH)
        s = _dot(k_ref[pl.ds(off, CH), :], qT)
        if diag_row0 is not None:
            row = lax.broadcasted_iota(jnp.int32, (CH, TQ), 0)
            col = lax.broadcasted_iota(jnp.int32, (CH, TQ), 1)
            s = jnp.where(row + diag_row0 <= col, s, NEG)
        s_ref[...] = s
        return jnp.max(s.reshape(groups, 8, TQ), axis=0)

    def soft_pv(j, s_ref, smax):
        off = pl.multiple_of(j * CH, CH)
        m_old = m_scr[...]
        m_new = jnp.maximum(m_old, jnp.max(smax, axis=0, keepdims=True))
        alpha = jnp.exp2(m_old - m_new)
        m_scr[...] = m_new
        p = jnp.exp2(s_ref[...] - m_new)
        lsum = jnp.sum(p.reshape(groups, 8, TQ), axis=0)
        pv = _dot(vT_ref[:, pl.ds(off, CH)], p.astype(BF16))
        acc_scr[...] = alpha * acc_scr[...] + pv
        l_scr[...] = alpha * l_scr[...] + lsum

    def pair(t, first_of_next_diag):
        smax_b = qk(2 * t + 1, sb_scr, None)
        soft_pv(2 * t, sa_scr, smax_scr[...])
        smax_scr[...] = qk(2 * t + 2, sa_scr, 0 if first_of_next_diag else None)
        soft_pv(2 * t + 1, sb_scr, smax_b)

    m_scr[...] = jnp.full(m_scr.shape, NEG, F32)
    l_scr[...] = jnp.zeros(l_scr.shape, F32)
    acc_scr[...] = jnp.zeros(acc_scr.shape, F32)

    @pl.when(i == 0)
    def _():
        smax_scr[...] = qk(0, sa_scr, 0)

    @pl.when(i > 0)
    def _():
        smax_scr[...] = qk(0, sa_scr, None)

    def body(t, carry):
        pair(t, False)
        return carry

    lax.fori_loop(0, i - 1, body, 0)

    @pl.when(i > 0)
    def _():
        pair(i - 1, True)

    smax_b = qk(2 * i + 1, sb_scr, CH)
    soft_pv(2 * i, sa_scr, smax_scr[...])
    soft_pv(2 * i + 1, sb_scr, smax_b)
    l = jnp.sum(l_scr[...], axis=0, keepdims=True)
    o_ref[...] = (acc_scr[...] / l).astype(BF16)


def _attention(qT, k, vT):
    return pl.pallas_call(
        _attn_kernel,
        grid=(HEADS, SEQ // TQ),
        in_specs=[
            pl.BlockSpec((None, SLOT, TQ), lambda h, i: (h, 0, i)),
            pl.BlockSpec((SEQ, SLOT), lambda h, i: (0, h)),
            pl.BlockSpec((HEAD_DIM, SEQ), lambda h, i: (h, 0)),
        ],
        out_specs=pl.BlockSpec((HEAD_DIM, TQ), lambda h, i: (h, i)),
        out_shape=jax.ShapeDtypeStruct((HEADS * HEAD_DIM, SEQ), BF16),
        scratch_shapes=[
            pltpu.VMEM((CH, TQ), F32),
            pltpu.VMEM((CH, TQ), F32),
            pltpu.VMEM((8, TQ), F32),
            pltpu.VMEM((1, TQ), F32),
            pltpu.VMEM((8, TQ), F32),
            pltpu.VMEM((HEAD_DIM, TQ), F32),
        ],
        compiler_params=_params(("arbitrary", "arbitrary")),
        name="attention",
    )(qT, k, vT)


def _out_ffn_kernel(oT_ref, h_ref, wo_ref, g_ref, wg_ref, wu_ref, wd_ref, gfin_ref,
                    out_ref, *, final):
    o = oT_ref[...].T
    h1 = h_ref[...] + _dot(o, wo_ref[...])
    xn = _rms(h1, g_ref[...]).astype(BF16)
    acc = h1
    for c in range(D_FF // FF_CHUNK):
        sl = slice(c * FF_CHUNK, (c + 1) * FF_CHUNK)
        gate = _dot(xn, wg_ref[:, sl])
        up = _dot(xn, wu_ref[:, sl])
        act = (gate * (1.0 / (1.0 + jnp.exp(-gate))) * up).astype(BF16)
        acc = acc + _dot(act, wd_ref[sl, :])
    if final:
        acc = _rms(acc, gfin_ref[...])
    out_ref[...] = acc


def _out_ffn(oT, h, wo, g, wg, wu, wd, gfin, final):
    n = SEQ // TM
    return pl.pallas_call(
        functools.partial(_out_ffn_kernel, final=final),
        grid=(n,),
        in_specs=[
            pl.BlockSpec((D_MODEL, TM), lambda i: (0, i)),
            pl.BlockSpec((TM, D_MODEL), lambda i: (i, 0)),
            _const_spec((D_MODEL, D_MODEL)),
            _const_spec((1, D_MODEL)),
            _const_spec((D_MODEL, D_FF)),
            _const_spec((D_MODEL, D_FF)),
            _const_spec((D_FF, D_MODEL)),
            _const_spec((1, D_MODEL)),
        ],
        out_specs=pl.BlockSpec((TM, D_MODEL), lambda i: (i, 0)),
        out_shape=jax.ShapeDtypeStruct((SEQ, D_MODEL), F32),
        compiler_params=_params(("arbitrary",)),
        name="out_ffn",
    )(oT, h, wo, g, wg, wu, wd, gfin)


def _rope_table_kernel(pos_ref, invf_ref, cos_ref, sin_ref, cosT_ref, sinT_ref):
    ang = pos_ref[...] * invf_ref[...]
    cos = jnp.cos(ang)
    sin = jnp.sin(ang)
    cos_ref[...] = cos
    sin_ref[...] = sin
    cosT_ref[...] = cos.T
    sinT_ref[...] = sin.T


def _rope_tables(pos_col, invf):
    n = SEQ // TM
    row = pl.BlockSpec((TM, LANES), lambda i: (i, 0))
    colT = pl.BlockSpec((LANES, TM), lambda i: (0, i))
    return pl.pallas_call(
        _rope_table_kernel,
        grid=(n,),
        in_specs=[pl.BlockSpec((TM, 1), lambda i: (i, 0)), _const_spec((1, LANES))],
        out_specs=[row, row, colT, colT],
        out_shape=[jax.ShapeDtypeStruct((SEQ, LANES), F32)] * 2
        + [jax.ShapeDtypeStruct((LANES, SEQ), F32)] * 2,
        compiler_params=_params(("arbitrary",)),
        name="rope_tables",
    )(pos_col, invf)


def _mla_kv_kernel(h_ref, g_ref, wa_ref, gckv_ref, cos_ref, sin_ref, wuk_ref,
                   p1_ref, p2_ref, wuvT_ref, k_ref, vT_ref):
    hn = _rms(h_ref[...], g_ref[...]).astype(BF16)
    a = _dot(hn, wa_ref[...])
    c = _rms(a[:, :KV_RANK], gckv_ref[...]).astype(BF16)
    x1 = a[:, KV_RANK:KV_RANK + LANES]
    x2 = a[:, KV_RANK + LANES:]
    cos = cos_ref[...]
    sin = sin_ref[...]
    r1 = (x1 * cos - x2 * sin).astype(BF16)
    r2 = (x1 * sin + x2 * cos).astype(BF16)
    k = _dot(c, wuk_ref[...]) + _dot(r1, p1_ref[...]) + _dot(r2, p2_ref[...])
    k_ref[...] = k.astype(BF16)
    vT_ref[...] = _dot(wuvT_ref[...], c.T).astype(BF16)


def _mla_kv(h, g, wa, gckv, cos, sin, wuk, p1, p2, wuvT):
    n = SEQ // TM
    return pl.pallas_call(
        _mla_kv_kernel,
        grid=(n,),
        in_specs=[
            pl.BlockSpec((TM, D_MODEL), lambda i: (i, 0)),
            _const_spec((1, D_MODEL)),
            _const_spec((D_MODEL, 4 * LANES)),
            _const_spec((1, KV_RANK)),
            pl.BlockSpec((TM, LANES), lambda i: (i, 0)),
            pl.BlockSpec((TM, LANES), lambda i: (i, 0)),
            _const_spec((KV_RANK, AUG)),
            _const_spec((LANES, AUG)),
            _const_spec((LANES, AUG)),
            _const_spec((D_MODEL, KV_RANK)),
        ],
        out_specs=[
            pl.BlockSpec((TM, AUG), lambda i: (i, 0)),
            pl.BlockSpec((D_MODEL, TM), lambda i: (0, i)),
        ],
        out_shape=[
            jax.ShapeDtypeStruct((SEQ, AUG), BF16),
            jax.ShapeDtypeStruct((D_MODEL, SEQ), BF16),
        ],
        compiler_params=_params(("arbitrary",)),
        name="mla_kv",
    )(h, g, wa, gckv, cos, sin, wuk, p1, p2, wuvT)


def _mla_q_kernel(h_ref, g_ref, wdq_ref, gcq_ref, wnT_ref, wx1T_ref, wx2T_ref,
                  cosT_ref, sinT_ref, qT_ref):
    scale = LOG2E / math.sqrt(MLA_NOPE + MLA_ROPE)
    xn = _rms(h_ref[...], g_ref[...]).astype(BF16)
    cq = _rms(_dot(xn, wdq_ref[...]), gcq_ref[...]).astype(BF16)
    cqT = cq.T
    nT = _dot(wnT_ref[...], cqT) * scale
    x1T = _dot(wx1T_ref[...], cqT)
    x2T = _dot(wx2T_ref[...], cqT)
    cosT = jnp.concatenate([cosT_ref[...]] * HEADS, axis=0)
    sinT = jnp.concatenate([sinT_ref[...]] * HEADS, axis=0)
    r1 = (x1T * cosT - x2T * sinT) * scale
    r2 = (x1T * sinT + x2T * cosT) * scale
    zeros = jnp.zeros((SLOT - MLA_NOPE - MLA_ROPE, TM), BF16)
    for h in range(HEADS):
        qT_ref[h, 0:MLA_NOPE, :] = nT[h * MLA_NOPE:(h + 1) * MLA_NOPE].astype(BF16)
        qT_ref[h, MLA_NOPE:MLA_NOPE + HALF_ROPE, :] = (
            r1[h * HALF_ROPE:(h + 1) * HALF_ROPE].astype(BF16))
        qT_ref[h, MLA_NOPE + HALF_ROPE:MLA_NOPE + MLA_ROPE, :] = (
            r2[h * HALF_ROPE:(h + 1) * HALF_ROPE].astype(BF16))
        qT_ref[h, MLA_NOPE + MLA_ROPE:, :] = zeros


def _mla_q(h, g, wdq, gcq, wnT, wx1T, wx2T, cosT, sinT):
    n = SEQ // TM
    return pl.pallas_call(
        _mla_q_kernel,
        grid=(n,),
        in_specs=[
            pl.BlockSpec((TM, D_MODEL), lambda i: (i, 0)),
            _const_spec((1, D_MODEL)),
            _const_spec((D_MODEL, Q_RANK)),
            _const_spec((1, Q_RANK)),
            _const_spec((HEADS * MLA_NOPE, Q_RANK)),
            _const_spec((HEADS * HALF_ROPE, Q_RANK)),
            _const_spec((HEADS * HALF_ROPE, Q_RANK)),
            pl.BlockSpec((HALF_ROPE, TM), lambda i: (0, i)),
            pl.BlockSpec((HALF_ROPE, TM), lambda i: (0, i)),
        ],
        out_specs=pl.BlockSpec((HEADS, SLOT, TM), lambda i: (0, 0, i)),
        out_shape=jax.ShapeDtypeStruct((HEADS, SLOT, SEQ), BF16),
        compiler_params=_params(("arbitrary",)),
        name="mla_q",
    )(h, g, wdq, gcq, wnT, wx1T, wx2T, cosT, sinT)


def _pad_heads_cols(w, width):
    k = w.shape[0]
    w = w.reshape(k, HEADS, width)
    w = jnp.pad(w, ((0, 0), (0, 0), (0, SLOT - width)))
    return w.reshape(k, AUG)


def _fox_bias_placement():
    pk = np.zeros((3 * LANES, AUG), np.float32)
    for h in range(HEADS):
        for j in range(3):
            pk[HEADS, h * SLOT + HEAD_DIM + j] = 1.0
            pk[j * LANES + h, h * SLOT + HEAD_DIM + 3 + j] = -1.0
    pq = np.zeros((AUG, 3 * LANES), np.float32)
    for h in range(HEADS):
        for j in range(3):
            pq[h * SLOT + HEAD_DIM + j, j * LANES + h] = 1.0
            pq[h * SLOT + HEAD_DIM + 3 + j, HEADS] = 1.0
    return jnp.asarray(pk, BF16), jnp.asarray(pq, BF16)


def _rope_placement():
    p1 = np.zeros((LANES, AUG), np.float32)
    p2 = np.zeros((LANES, AUG), np.float32)
    for h in range(HEADS):
        for j in range(HALF_ROPE):
            p1[j, h * SLOT + MLA_NOPE + j] = 1.0
            p2[j, h * SLOT + MLA_NOPE + HALF_ROPE + j] = 1.0
    return jnp.asarray(p1, BF16), jnp.asarray(p2, BF16)


def kernel(x, positions, attn_norm, ffn_norm, w_gate, w_up, w_down, fox_w_in, fox_b_f,
           fox_w_o, kv_norm, w_kv_a, ckv_norm, w_uk, w_uv, mla_w_dq, cq_norm, mla_w_uq,
           mla_w_o, final_norm):
    assert x.shape == (1, SEQ, D_MODEL)
    h = x.reshape(SEQ, D_MODEL)
    W = HEADS * HEAD_DIM
    pk, pqT = _fox_bias_placement()
    p1, p2 = _rope_placement()
    tri = jnp.asarray(np.tril(np.ones((TM, TM), np.float32)), BF16)
    row = lambda v: v.reshape(1, -1)

    def ffn_args(l):
        return (row(ffn_norm[l]), w_gate[l].astype(BF16), w_up[l].astype(BF16),
                w_down[l].astype(BF16), row(final_norm))

    for l in range(N_A):
        w_in = fox_w_in[l]
        wqT = _pad_heads_cols(w_in[:, :W], HEAD_DIM).T.astype(BF16)
        wk = _pad_heads_cols(w_in[:, W:2 * W], HEAD_DIM).astype(BF16)
        wvT = w_in[:, 2 * W:3 * W].T.astype(BF16)
        wf = jnp.pad(w_in[:, 3 * W:], ((0, 0), (0, LANES - HEADS))).astype(BF16)
        bf = jnp.pad(fox_b_f[l], (0, LANES - HEADS)).reshape(1, LANES)
        qT, k, vT = _fox_proj(h, row(attn_norm[l]), wqT, wk, wvT, wf, bf, pk, pqT, tri)
        oT = _attention(qT.reshape(HEADS, SLOT, SEQ), k, vT)
        h = _out_ffn(oT, h, fox_w_o[l].astype(BF16), *ffn_args(l), final=False)

    invf = ROPE_THETA ** (-jnp.arange(0, HALF_ROPE, dtype=F32) * 2.0 / MLA_ROPE)
    invf = jnp.pad(invf, (0, LANES - HALF_ROPE)).reshape(1, LANES)
    pos_col = positions.reshape(SEQ, 1).astype(F32)
    cos, sin, cosT, sinT = _rope_tables(pos_col, invf)
    wa = jnp.concatenate([
        w_kv_a[:, :KV_RANK],
        jnp.pad(w_kv_a[:, KV_RANK:KV_RANK + HALF_ROPE], ((0, 0), (0, LANES - HALF_ROPE))),
        jnp.pad(w_kv_a[:, KV_RANK + HALF_ROPE:], ((0, 0), (0, LANES - HALF_ROPE))),
    ], axis=1).astype(BF16)
    wuk = _pad_heads_cols(w_uk.reshape(KV_RANK, HEADS * MLA_NOPE), MLA_NOPE).astype(BF16)
    wuvT = w_uv.reshape(KV_RANK, HEADS * HEAD_DIM).T.astype(BF16)
    k_sh, vT_sh = _mla_kv(h, row(kv_norm), wa, row(ckv_norm), cos, sin, wuk, p1, p2, wuvT)

    for l in range(N_A, DEPTH):
        j = l - N_A
        wuq = mla_w_uq[j].reshape(Q_RANK, HEADS, MLA_NOPE + MLA_ROPE)
        wnT = wuq[:, :, :MLA_NOPE].reshape(Q_RANK, -1).T.astype(BF16)
        wx1T = wuq[:, :, MLA_NOPE:MLA_NOPE + HALF_ROPE].reshape(Q_RANK, -1).T.astype(BF16)
        wx2T = wuq[:, :, MLA_NOPE + HALF_ROPE:].reshape(Q_RANK, -1).T.astype(BF16)
        qT = _mla_q(h, row(attn_norm[l]), mla_w_dq[j].astype(BF16), row(cq_norm[j]),
                    wnT, wx1T, wx2T, cosT, sinT)
        oT = _attention(qT, k_sh, vT_sh)
        h = _out_ffn(oT, h, mla_w_o[j].astype(BF16), *ffn_args(l), final=(l == DEPTH - 1))

    return h.reshape(1, SEQ, D_MODEL)
```

```python
import functools
import math

import numpy as np
import jax
import jax.numpy as jnp
from jax import lax
from jax.experimental import pallas as pl
from jax.experimental.pallas import tpu as pltpu

F32 = jnp.float32
BF16 = jnp.bfloat16

D_MODEL = 1024
SEQ = 16384
DEPTH = 4
N_A = DEPTH // 2
RMS_EPS = 1e-6
HEADS = 16
HEAD_DIM = 64
MLA_NOPE = 64
MLA_ROPE = 32
HALF_ROPE = MLA_ROPE // 2
KV_RANK = 256
Q_RANK = 768
ROPE_THETA = 10000.0
D_FF = 2816

LANES = 128
SLOT = 128
AUG = HEADS * SLOT
LOG2E = math.log2(math.e)
NEG = -1e30

TM = 512
TQ = 1024
CH = TQ // 2
FF_CHUNK = D_FF // 2
VMEM_LIMIT = 56 * 1024 * 1024


def _dot(a, b):
    return jnp.dot(a, b, preferred_element_type=F32)


def _rms(x, g):
    inv = lax.rsqrt(jnp.mean(x * x, axis=-1, keepdims=True) + RMS_EPS)
    return (x * inv) * g


def _split3(x):
    hi = x.astype(BF16)
    r = x - hi.astype(F32)
    mid = r.astype(BF16)
    lo = (r - mid.astype(F32)).astype(BF16)
    return hi, mid, lo


def _const_spec(shape):
    nd = len(shape)
    return pl.BlockSpec(shape, lambda *_: (0,) * nd, pipeline_mode=pl.Buffered(1))


def _params(sem):
    return pltpu.CompilerParams(dimension_semantics=sem, vmem_limit_bytes=VMEM_LIMIT)


def _fox_proj_kernel(h_ref, g_ref, wqT_ref, wk_ref, wvT_ref, wf_ref, bf_ref,
                     pk_ref, pqT_ref, tri_ref, qT_ref, k_ref, vT_ref, carry_ref):
    @pl.when(pl.program_id(0) == 0)
    def _():
        carry_ref[...] = jnp.zeros_like(carry_ref)

    xn = _rms(h_ref[...], g_ref[...]).astype(BF16)
    xnT = xn.T

    f = _dot(xn, wf_ref[...]) + bf_ref[...]
    logf = (jnp.minimum(f, 0.0) - jnp.log1p(jnp.exp(-jnp.abs(f)))) * LOG2E
    hi, mid, lo = _split3(logf)
    tri = tri_ref[...]
    cum = _dot(tri, hi) + _dot(tri, mid) + _dot(tri, lo) + carry_ref[...]
    carry_ref[...] = cum[TM - 1:TM, :]

    lane = lax.broadcasted_iota(jnp.int32, cum.shape, 1)
    c = jnp.where(lane < HEADS, cum, 1.0)
    c3 = jnp.concatenate(_split3(c), axis=1)
    c3T = jnp.concatenate(_split3(c.T), axis=0)

    k = _dot(xn, wk_ref[...]) + _dot(c3, pk_ref[...])
    k_ref[...] = k.astype(BF16)
    qT = _dot(wqT_ref[...], xnT) * (LOG2E / math.sqrt(HEAD_DIM)) + _dot(pqT_ref[...], c3T)
    qT_ref[...] = qT.astype(BF16)
    vT_ref[...] = _dot(wvT_ref[...], xnT).astype(BF16)


def _fox_proj(h, g, wqT, wk, wvT, wf, bf, pk, pqT, tri):
    n = SEQ // TM
    return pl.pallas_call(
        _fox_proj_kernel,
        grid=(n,),
        in_specs=[
            pl.BlockSpec((TM, D_MODEL), lambda i: (i, 0)),
            _const_spec((1, D_MODEL)),
            _const_spec((AUG, D_MODEL)),
            _const_spec((D_MODEL, AUG)),
            _const_spec((D_MODEL, D_MODEL)),
            _const_spec((D_MODEL, LANES)),
            _const_spec((1, LANES)),
            _const_spec((3 * LANES, AUG)),
            _const_spec((AUG, 3 * LANES)),
            _const_spec((TM, TM)),
        ],
        out_specs=[
            pl.BlockSpec((AUG, TM), lambda i: (0, i)),
            pl.BlockSpec((TM, AUG), lambda i: (i, 0)),
            pl.BlockSpec((D_MODEL, TM), lambda i: (0, i)),
        ],
        out_shape=[
            jax.ShapeDtypeStruct((AUG, SEQ), BF16),
            jax.ShapeDtypeStruct((SEQ, AUG), BF16),
            jax.ShapeDtypeStruct((D_MODEL, SEQ), BF16),
        ],
        scratch_shapes=[pltpu.VMEM((1, LANES), F32)],
        compiler_params=_params(("arbitrary",)),
        name="fox_proj",
    )(h, g, wqT, wk, wvT, wf, bf, pk, pqT, tri)


def _attn_kernel(qT_ref, k_ref, vT_ref, o_ref, sa_scr, sb_scr, pa_scr, pb_scr,
                 smax_a_scr, smax_b_scr, alpha_b_scr, m_scr, l_scr, acc_scr):
    i = pl.program_id(1)
    qT = qT_ref[...]
    groups = CH // 8

    def scores(c, s_ref, diag_row0):
        off = pl.multiple_of(c * CH, CH)
        s = _dot(k_ref[pl.ds(off, CH), :], qT)
        if diag_row0 is not None:
            row = lax.broadcasted_iota(jnp.int32, (CH, TQ), 0)
            col = lax.broadcasted_iota(jnp.int32, (CH, TQ), 1)
            s = jnp.where(row + diag_row0 <= col, s, NEG)
        s_ref[...] = s
        return jnp.max(s.reshape(groups, 8, TQ), axis=0)

    def probs(s_ref, p_ref, smax):
        m_old = m_scr[...]
        m_new = jnp.maximum(m_old, jnp.max(smax, axis=0, keepdims=True))
        alpha = jnp.exp2(m_old - m_new)
        m_scr[...] = m_new
        p = jnp.exp2(s_ref[...] - m_new)
        l_scr[...] = alpha * l_scr[...] + jnp.sum(p.reshape(groups, 8, TQ), axis=0)
        p_ref[...] = p.astype(BF16)
        return alpha

    def pv(c, p_ref, alpha):
        off = pl.multiple_of(c * CH, CH)
        acc_scr[...] = alpha * acc_scr[...] + _dot(vT_ref[:, pl.ds(off, CH)], p_ref[...])

    def score_pair(t, masked):
        smax_a_scr[...] = scores(2 * t, sa_scr, 0 if masked else None)
        smax_b_scr[...] = scores(2 * t + 1, sb_scr, CH if masked else None)

    def pair(t, next_masked):
        pv(jnp.maximum(2 * t - 1, 0), pb_scr, alpha_b_scr[...])
        alpha_a = probs(sa_scr, pa_scr, smax_a_scr[...])
        if next_masked is not None:
            smax_a_scr[...] = scores(2 * t + 2, sa_scr, 0 if next_masked else None)
        pv(2 * t, pa_scr, alpha_a)
        alpha_b_scr[...] = probs(sb_scr, pb_scr, smax_b_scr[...])
        if next_masked is not None:
            smax_b_scr[...] = scores(2 * t + 3, sb_scr, CH if next_masked else None)

    m_scr[...] = jnp.full(m_scr.shape, NEG, F32)
    l_scr[...] = jnp.zeros(l_scr.shape, F32)
    acc_scr[...] = jnp.zeros(acc_scr.shape, F32)
    pb_scr[...] = jnp.zeros(pb_scr.shape, BF16)
    alpha_b_scr[...] = jnp.ones(alpha_b_scr.shape, F32)

    @pl.when(i == 0)
    def _():
        score_pair(0, True)

    @pl.when(i > 0)
    def _():
        score_pair(0, False)

    def body(t, carry):
        pair(t, False)
        return carry

    lax.fori_loop(0, i - 1, body, 0)

    @pl.when(i > 0)
    def _():
        pair(i - 1, True)

    pair(i, None)
    pv(2 * i + 1, pb_scr, alpha_b_scr[...])
    l = jnp.sum(l_scr[...], axis=0, keepdims=True)
    o_ref[...] = (acc_scr[...] / l).astype(BF16)


def _attention(qT, k, vT):
    return pl.pallas_call(
        _attn_kernel,
        grid=(HEADS, SEQ // TQ),
        in_specs=[
            pl.BlockSpec((None, SLOT, TQ), lambda h, i: (h, 0, i)),
            pl.BlockSpec((SEQ, SLOT), lambda h, i: (0, h)),
            pl.BlockSpec((HEAD_DIM, SEQ), lambda h, i: (h, 0)),
        ],
        out_specs=pl.BlockSpec((HEAD_DIM, TQ), lambda h, i: (h, i)),
        out_shape=jax.ShapeDtypeStruct((HEADS * HEAD_DIM, SEQ), BF16),
        scratch_shapes=[
            pltpu.VMEM((CH, TQ), F32),
            pltpu.VMEM((CH, TQ), F32),
            pltpu.VMEM((CH, TQ), BF16),
            pltpu.VMEM((CH, TQ), BF16),
            pltpu.VMEM((8, TQ), F32),
            pltpu.VMEM((8, TQ), F32),
            pltpu.VMEM((1, TQ), F32),
            pltpu.VMEM((1, TQ), F32),
            pltpu.VMEM((8, TQ), F32),
            pltpu.VMEM((HEAD_DIM, TQ), F32),
        ],
        compiler_params=_params(("arbitrary", "arbitrary")),
        name="attention",
    )(qT, k, vT)


def _out_ffn_kernel(oT_ref, h_ref, wo_ref, g_ref, wg_ref, wu_ref, wd_ref, gfin_ref,
                    out_ref, *, final):
    o = oT_ref[...].T
    h1 = h_ref[...] + _dot(o, wo_ref[...])
    xn = _rms(h1, g_ref[...]).astype(BF16)
    acc = h1
    for c in range(D_FF // FF_CHUNK):
        sl = slice(c * FF_CHUNK, (c + 1) * FF_CHUNK)
        gate = _dot(xn, wg_ref[:, sl])
        up = _dot(xn, wu_ref[:, sl])
        act = (gate * (1.0 / (1.0 + jnp.exp(-gate))) * up).astype(BF16)
        acc = acc + _dot(act, wd_ref[sl, :])
    if final:
        acc = _rms(acc, gfin_ref[...])
    out_ref[...] = acc


def _out_ffn(oT, h, wo, g, wg, wu, wd, gfin, final):
    n = SEQ // TM
    return pl.pallas_call(
        functools.partial(_out_ffn_kernel, final=final),
        grid=(n,),
        in_specs=[
            pl.BlockSpec((D_MODEL, TM), lambda i: (0, i)),
            pl.BlockSpec((TM, D_MODEL), lambda i: (i, 0)),
            _const_spec((D_MODEL, D_MODEL)),
            _const_spec((1, D_MODEL)),
            _const_spec((D_MODEL, D_FF)),
            _const_spec((D_MODEL, D_FF)),
            _const_spec((D_FF, D_MODEL)),
            _const_spec((1, D_MODEL)),
        ],
        out_specs=pl.BlockSpec((TM, D_MODEL), lambda i: (i, 0)),
        out_shape=jax.ShapeDtypeStruct((SEQ, D_MODEL), F32),
        compiler_params=_params(("arbitrary",)),
        name="out_ffn",
    )(oT, h, wo, g, wg, wu, wd, gfin)


def _rope_table_kernel(pos_ref, invf_ref, cos_ref, sin_ref, cosT_ref, sinT_ref):
    ang = pos_ref[...] * invf_ref[...]
    cos = jnp.cos(ang)
    sin = jnp.sin(ang)
    cos_ref[...] = cos
    sin_ref[...] = sin
    cosT_ref[...] = cos.T
    sinT_ref[...] = sin.T


def _rope_tables(pos_col, invf):
    n = SEQ // TM
    row = pl.BlockSpec((TM, LANES), lambda i: (i, 0))
    colT = pl.BlockSpec((LANES, TM), lambda i: (0, i))
    return pl.pallas_call(
        _rope_table_kernel,
        grid=(n,),
        in_specs=[pl.BlockSpec((TM, 1), lambda i: (i, 0)), _const_spec((1, LANES))],
        out_specs=[row, row, colT, colT],
        out_shape=[jax.ShapeDtypeStruct((SEQ, LANES), F32)] * 2
        + [jax.ShapeDtypeStruct((LANES, SEQ), F32)] * 2,
        compiler_params=_params(("arbitrary",)),
        name="rope_tables",
    )(pos_col, invf)


def _mla_kv_kernel(h_ref, g_ref, wa_ref, gckv_ref, cos_ref, sin_ref, wuk_ref,
                   p1_ref, p2_ref, wuvT_ref, k_ref, vT_ref):
    hn = _rms(h_ref[...], g_ref[...]).astype(BF16)
    a = _dot(hn, wa_ref[...])
    c = _rms(a[:, :KV_RANK], gckv_ref[...]).astype(BF16)
    x1 = a[:, KV_RANK:KV_RANK + LANES]
    x2 = a[:, KV_RANK + LANES:]
    cos = cos_ref[...]
    sin = sin_ref[...]
    r1 = (x1 * cos - x2 * sin).astype(BF16)
    r2 = (x1 * sin + x2 * cos).astype(BF16)
    k = _dot(c, wuk_ref[...]) + _dot(r1, p1_ref[...]) + _dot(r2, p2_ref[...])
    k_ref[...] = k.astype(BF16)
    vT_ref[...] = _dot(wuvT_ref[...], c.T).astype(BF16)


def _mla_kv(h, g, wa, gckv, cos, sin, wuk, p1, p2, wuvT):
    n = SEQ // TM
    return pl.pallas_call(
        _mla_kv_kernel,
        grid=(n,),
        in_specs=[
            pl.BlockSpec((TM, D_MODEL), lambda i: (i, 0)),
            _const_spec((1, D_MODEL)),
            _const_spec((D_MODEL, 4 * LANES)),
            _const_spec((1, KV_RANK)),
            pl.BlockSpec((TM, LANES), lambda i: (i, 0)),
            pl.BlockSpec((TM, LANES), lambda i: (i, 0)),
            _const_spec((KV_RANK, AUG)),
            _const_spec((LANES, AUG)),
            _const_spec((LANES, AUG)),
            _const_spec((D_MODEL, KV_RANK)),
        ],
        out_specs=[
            pl.BlockSpec((TM, AUG), lambda i: (i, 0)),
            pl.BlockSpec((D_MODEL, TM), lambda i: (0, i)),
        ],
        out_shape=[
            jax.ShapeDtypeStruct((SEQ, AUG), BF16),
            jax.ShapeDtypeStruct((D_MODEL, SEQ), BF16),
        ],
        compiler_params=_params(("arbitrary",)),
        name="mla_kv",
    )(h, g, wa, gckv, cos, sin, wuk, p1, p2, wuvT)


def _mla_q_kernel(h_ref, g_ref, wdq_ref, gcq_ref, wnT_ref, wx1T_ref, wx2T_ref,
                  cosT_ref, sinT_ref, qT_ref):
    scale = LOG2E / math.sqrt(MLA_NOPE + MLA_ROPE)
    xn = _rms(h_ref[...], g_ref[...]).astype(BF16)
    cq = _rms(_dot(xn, wdq_ref[...]), gcq_ref[...]).astype(BF16)
    cqT = cq.T
    nT = _dot(wnT_ref[...], cqT) * scale
    x1T = _dot(wx1T_ref[...], cqT)
    x2T = _dot(wx2T_ref[...], cqT)
    cosT = jnp.concatenate([cosT_ref[...]] * HEADS, axis=0)
    sinT = jnp.concatenate([sinT_ref[...]] * HEADS, axis=0)
    r1 = (x1T * cosT - x2T * sinT) * scale
    r2 = (x1T * sinT + x2T * cosT) * scale
    zeros = jnp.zeros((SLOT - MLA_NOPE - MLA_ROPE, TM), BF16)
    for h in range(HEADS):
        qT_ref[h, 0:MLA_NOPE, :] = nT[h * MLA_NOPE:(h + 1) * MLA_NOPE].astype(BF16)
        qT_ref[h, MLA_NOPE:MLA_NOPE + HALF_ROPE, :] = (
            r1[h * HALF_ROPE:(h + 1) * HALF_ROPE].astype(BF16))
        qT_ref[h, MLA_NOPE + HALF_ROPE:MLA_NOPE + MLA_ROPE, :] = (
            r2[h * HALF_ROPE:(h + 1) * HALF_ROPE].astype(BF16))
        qT_ref[h, MLA_NOPE + MLA_ROPE:, :] = zeros


def _mla_q(h, g, wdq, gcq, wnT, wx1T, wx2T, cosT, sinT):
    n = SEQ // TM
    return pl.pallas_call(
        _mla_q_kernel,
        grid=(n,),
        in_specs=[
            pl.BlockSpec((TM, D_MODEL), lambda i: (i, 0)),
            _const_spec((1, D_MODEL)),
            _const_spec((D_MODEL, Q_RANK)),
            _const_spec((1, Q_RANK)),
            _const_spec((HEADS * MLA_NOPE, Q_RANK)),
            _const_spec((HEADS * HALF_ROPE, Q_RANK)),
            _const_spec((HEADS * HALF_ROPE, Q_RANK)),
            pl.BlockSpec((HALF_ROPE, TM), lambda i: (0, i)),
            pl.BlockSpec((HALF_ROPE, TM), lambda i: (0, i)),
        ],
        out_specs=pl.BlockSpec((HEADS, SLOT, TM), lambda i: (0, 0, i)),
        out_shape=jax.ShapeDtypeStruct((HEADS, SLOT, SEQ), BF16),
        compiler_params=_params(("arbitrary",)),
        name="mla_q",
    )(h, g, wdq, gcq, wnT, wx1T, wx2T, cosT, sinT)


def _pad_heads_cols(w, width):
    k = w.shape[0]
    w = w.reshape(k, HEADS, width)
    w = jnp.pad(w, ((0, 0), (0, 0), (0, SLOT - width)))
    return w.reshape(k, AUG)


def _fox_bias_placement():
    pk = np.zeros((3 * LANES, AUG), np.float32)
    for h in range(HEADS):
        for j in range(3):
            pk[HEADS, h * SLOT + HEAD_DIM + j] = 1.0
            pk[j * LANES + h, h * SLOT + HEAD_DIM + 3 + j] = -1.0
    pq = np.zeros((AUG, 3 * LANES), np.float32)
    for h in range(HEADS):
        for j in range(3):
            pq[h * SLOT + HEAD_DIM + j, j * LANES + h] = 1.0
            pq[h * SLOT + HEAD_DIM + 3 + j, HEADS] = 1.0
    return jnp.asarray(pk, BF16), jnp.asarray(pq, BF16)


def _rope_placement():
    p1 = np.zeros((LANES, AUG), np.float32)
    p2 = np.zeros((LANES, AUG), np.float32)
    for h in range(HEADS):
        for j in range(HALF_ROPE):
            p1[j, h * SLOT + MLA_NOPE + j] = 1.0
            p2[j, h * SLOT + MLA_NOPE + HALF_ROPE + j] = 1.0
    return jnp.asarray(p1, BF16), jnp.asarray(p2, BF16)


def kernel(x, positions, attn_norm, ffn_norm, w_gate, w_up, w_down, fox_w_in, fox_b_f,
           fox_w_o, kv_norm, w_kv_a, ckv_norm, w_uk, w_uv, mla_w_dq, cq_norm, mla_w_uq,
           mla_w_o, final_norm):
    assert x.shape == (1, SEQ, D_MODEL)
    h = x.reshape(SEQ, D_MODEL)
    W = HEADS * HEAD_DIM
    pk, pqT = _fox_bias_placement()
    p1, p2 = _rope_placement()
    tri = jnp.asarray(np.tril(np.ones((TM, TM), np.float32)), BF16)
    row = lambda v: v.reshape(1, -1)

    def ffn_args(l):
        return (row(ffn_norm[l]), w_gate[l].astype(BF16), w_up[l].astype(BF16),
                w_down[l].astype(BF16), row(final_norm))

    for l in range(N_A):
        w_in = fox_w_in[l]
        wqT = _pad_heads_cols(w_in[:, :W], HEAD_DIM).T.astype(BF16)
        wk = _pad_heads_cols(w_in[:, W:2 * W], HEAD_DIM).astype(BF16)
        wvT = w_in[:, 2 * W:3 * W].T.astype(BF16)
        wf = jnp.pad(w_in[:, 3 * W:], ((0, 0), (0, LANES - HEADS))).astype(BF16)
        bf = jnp.pad(fox_b_f[l], (0, LANES - HEADS)).reshape(1, LANES)
        qT, k, vT = _fox_proj(h, row(attn_norm[l]), wqT, wk, wvT, wf, bf, pk, pqT, tri)
        oT = _attention(qT.reshape(HEADS, SLOT, SEQ), k, vT)
        h = _out_ffn(oT, h, fox_w_o[l].astype(BF16), *ffn_args(l), final=False)

    invf = ROPE_THETA ** (-jnp.arange(0, HALF_ROPE, dtype=F32) * 2.0 / MLA_ROPE)
    invf = jnp.pad(invf, (0, LANES - HALF_ROPE)).reshape(1, LANES)
    pos_col = positions.reshape(SEQ, 1).astype(F32)
    cos, sin, cosT, sinT = _rope_tables(pos_col, invf)
    wa = jnp.concatenate([
        w_kv_a[:, :KV_RANK],
        jnp.pad(w_kv_a[:, KV_RANK:KV_RANK + HALF_ROPE], ((0, 0), (0, LANES - HALF_ROPE))),
        jnp.pad(w_kv_a[:, KV_RANK + HALF_ROPE:], ((0, 0), (0, LANES - HALF_ROPE))),
    ], axis=1).astype(BF16)
    wuk = _pad_heads_cols(w_uk.reshape(KV_RANK, HEADS * MLA_NOPE), MLA_NOPE).astype(BF16)
    wuvT = w_uv.reshape(KV_RANK, HEADS * HEAD_DIM).T.astype(BF16)
    k_sh, vT_sh = _mla_kv(h, row(kv_norm), wa, row(ckv_norm), cos, sin, wuk, p1, p2, wuvT)

    for l in range(N_A, DEPTH):
        j = l - N_A
        wuq = mla_w_uq[j].reshape(Q_RANK, HEADS, MLA_NOPE + MLA_ROPE)
        wnT = wuq[:, :, :MLA_NOPE].reshape(Q_RANK, -1).T.astype(BF16)
        wx1T = wuq[:, :, MLA_NOPE:MLA_NOPE + HALF_ROPE].reshape(Q_RANK, -1).T.astype(BF16)
        wx2T = wuq[:, :, MLA_NOPE + HALF_ROPE:].reshape(Q_RANK, -1).T.astype(BF16)
        qT = _mla_q(h, row(attn_norm[l]), mla_w_dq[j].astype(BF16), row(cq_norm[j]),
                    wnT, wx1T, wx2T, cosT, sinT)
        oT = _attention(qT, k_sh, vT_sh)
        h = _out_ffn(oT, h, mla_w_o[j].astype(BF16), *ffn_args(l), final=(l == DEPTH - 1))

    return h.reshape(1, SEQ, D_MODEL)
```

```python
import functools
import math

import numpy as np
import jax
import jax.numpy as jnp
from jax import lax
from jax.experimental import pallas as pl
from jax.experimental.pallas import tpu as pltpu

F32 = jnp.float32
BF16 = jnp.bfloat16

D_MODEL = 1024
SEQ = 16384
DEPTH = 4
N_A = DEPTH // 2
RMS_EPS = 1e-6
HEADS = 16
HEAD_DIM = 64
MLA_NOPE = 64
MLA_ROPE = 32
HALF_ROPE = MLA_ROPE // 2
KV_RANK = 256
Q_RANK = 768
ROPE_THETA = 10000.0
D_FF = 2816

LANES = 128
SLOT = 128
AUG = HEADS * SLOT
LOG2E = math.log2(math.e)
NEG = -1e30

TM = 512
TQ = 1024
CH = TQ // 2
ACC_ROWS = HEAD_DIM + 16
FF_CHUNK = D_FF // 2
VMEM_LIMIT = 56 * 1024 * 1024


def _dot(a, b):
    return jnp.dot(a, b, preferred_element_type=F32)


def _rms(x, g):
    inv = lax.rsqrt(jnp.mean(x * x, axis=-1, keepdims=True) + RMS_EPS)
    return (x * inv) * g


def _split3(x):
    hi = x.astype(BF16)
    r = x - hi.astype(F32)
    mid = r.astype(BF16)
    lo = (r - mid.astype(F32)).astype(BF16)
    return hi, mid, lo


def _const_spec(shape):
    nd = len(shape)
    return pl.BlockSpec(shape, lambda *_: (0,) * nd, pipeline_mode=pl.Buffered(1))


def _params(sem):
    return pltpu.CompilerParams(dimension_semantics=sem, vmem_limit_bytes=VMEM_LIMIT)


def _fox_proj_kernel(h_ref, g_ref, wqT_ref, wk_ref, wvT_ref, wf_ref, bf_ref,
                     pk_ref, pqT_ref, tri_ref, qT_ref, k_ref, vT_ref, carry_ref):
    @pl.when(pl.program_id(0) == 0)
    def _():
        carry_ref[...] = jnp.zeros_like(carry_ref)

    xn = _rms(h_ref[...], g_ref[...]).astype(BF16)
    xnT = xn.T

    f = _dot(xn, wf_ref[...]) + bf_ref[...]
    logf = (jnp.minimum(f, 0.0) - jnp.log1p(jnp.exp(-jnp.abs(f)))) * LOG2E
    hi, mid, lo = _split3(logf)
    tri = tri_ref[...]
    cum = _dot(tri, hi) + _dot(tri, mid) + _dot(tri, lo) + carry_ref[...]
    carry_ref[...] = cum[TM - 1:TM, :]

    lane = lax.broadcasted_iota(jnp.int32, cum.shape, 1)
    c = jnp.where(lane < HEADS, cum, 1.0)
    c3 = jnp.concatenate(_split3(c), axis=1)
    c3T = jnp.concatenate(_split3(c.T), axis=0)

    k = _dot(xn, wk_ref[...]) + _dot(c3, pk_ref[...])
    k_ref[...] = k.astype(BF16)
    qT = _dot(wqT_ref[...], xnT) * (LOG2E / math.sqrt(HEAD_DIM)) + _dot(pqT_ref[...], c3T)
    qT_ref[...] = qT.astype(BF16)
    vT_ref[...] = _dot(wvT_ref[...], xnT).astype(BF16)


def _fox_proj(h, g, wqT, wk, wvT, wf, bf, pk, pqT, tri):
    n = SEQ // TM
    return pl.pallas_call(
        _fox_proj_kernel,
        grid=(n,),
        in_specs=[
            pl.BlockSpec((TM, D_MODEL), lambda i: (i, 0)),
            _const_spec((1, D_MODEL)),
            _const_spec((AUG, D_MODEL)),
            _const_spec((D_MODEL, AUG)),
            _const_spec((D_MODEL, D_MODEL)),
            _const_spec((D_MODEL, LANES)),
            _const_spec((1, LANES)),
            _const_spec((3 * LANES, AUG)),
            _const_spec((AUG, 3 * LANES)),
            _const_spec((TM, TM)),
        ],
        out_specs=[
            pl.BlockSpec((AUG, TM), lambda i: (0, i)),
            pl.BlockSpec((TM, AUG), lambda i: (i, 0)),
            pl.BlockSpec((D_MODEL, TM), lambda i: (0, i)),
        ],
        out_shape=[
            jax.ShapeDtypeStruct((AUG, SEQ), BF16),
            jax.ShapeDtypeStruct((SEQ, AUG), BF16),
            jax.ShapeDtypeStruct((D_MODEL, SEQ), BF16),
        ],
        scratch_shapes=[pltpu.VMEM((1, LANES), F32)],
        compiler_params=_params(("arbitrary",)),
        name="fox_proj",
    )(h, g, wqT, wk, wvT, wf, bf, pk, pqT, tri)


def _attn_kernel(qT_ref, k_ref, vT_ref, o_ref, sa_scr, sb_scr, pa_scr, pb_scr,
                 smax_a_scr, smax_b_scr, alpha_b_scr, m_scr, acc_scr):
    i = pl.program_id(1)
    qT = qT_ref[...]
    groups = CH // 8

    def scores(c, s_ref, diag_row0):
        off = pl.multiple_of(c * CH, CH)
        s = _dot(k_ref[pl.ds(off, CH), :], qT)
        if diag_row0 is not None:
            row = lax.broadcasted_iota(jnp.int32, (CH, TQ), 0)
            col = lax.broadcasted_iota(jnp.int32, (CH, TQ), 1)
            s = jnp.where(row + diag_row0 <= col, s, NEG)
        s_ref[...] = s
        return jnp.max(s.reshape(groups, 8, TQ), axis=0)

    def probs(s_ref, p_ref, smax):
        m_old = m_scr[...]
        m_new = jnp.maximum(m_old, jnp.max(smax, axis=0, keepdims=True))
        alpha = jnp.exp2(m_old - m_new)
        m_scr[...] = m_new
        p_ref[...] = jnp.exp2(s_ref[...] - m_new).astype(BF16)
        return alpha

    ones_rows = jnp.ones((ACC_ROWS - HEAD_DIM, CH), BF16)

    def pv(c, p_ref, alpha):
        off = pl.multiple_of(c * CH, CH)
        lhs = jnp.concatenate([vT_ref[:, pl.ds(off, CH)], ones_rows], axis=0)
        acc_scr[...] = alpha * acc_scr[...] + _dot(lhs, p_ref[...])

    def score_pair(t, masked):
        smax_a_scr[...] = scores(2 * t, sa_scr, 0 if masked else None)
        smax_b_scr[...] = scores(2 * t + 1, sb_scr, CH if masked else None)

    def pair(t, next_masked, first=False):
        if not first:
            pv(jnp.maximum(2 * t - 1, 0), pb_scr, alpha_b_scr[...])
        alpha_a = probs(sa_scr, pa_scr, smax_a_scr[...])
        if next_masked is not None:
            smax_a_scr[...] = scores(2 * t + 2, sa_scr, 0 if next_masked else None)
        pv(2 * t, pa_scr, alpha_a)
        alpha_b_scr[...] = probs(sb_scr, pb_scr, smax_b_scr[...])
        if next_masked is not None:
            smax_b_scr[...] = scores(2 * t + 3, sb_scr, CH if next_masked else None)

    def init():
        m_scr[...] = jnp.full(m_scr.shape, NEG, F32)
        acc_scr[...] = jnp.zeros(acc_scr.shape, F32)

    def finish():
        pv(2 * i + 1, pb_scr, alpha_b_scr[...])
        l = acc_scr[HEAD_DIM:HEAD_DIM + 1, :]
        o_ref[...] = (acc_scr[:HEAD_DIM, :] / l).astype(BF16)

    @pl.when(i == 0)
    def _():
        init()
        score_pair(0, True)
        pair(0, None, first=True)
        finish()

    @pl.when(i > 0)
    def _():
        init()
        pb_scr[...] = jnp.zeros(pb_scr.shape, BF16)
        alpha_b_scr[...] = jnp.ones(alpha_b_scr.shape, F32)
        score_pair(0, False)

        def body(t, carry):
            pair(t, False)
            return carry

        lax.fori_loop(0, i - 1, body, 0)
        pair(i - 1, True)
        pair(i, None)
        finish()


def _attention(qT, k, vT):
    return pl.pallas_call(
        _attn_kernel,
        grid=(HEADS, SEQ // TQ),
        in_specs=[
            pl.BlockSpec((None, SLOT, TQ), lambda h, i: (h, 0, i)),
            pl.BlockSpec((SEQ, SLOT), lambda h, i: (0, h)),
            pl.BlockSpec((HEAD_DIM, SEQ), lambda h, i: (h, 0)),
        ],
        out_specs=pl.BlockSpec((HEAD_DIM, TQ), lambda h, i: (h, i)),
        out_shape=jax.ShapeDtypeStruct((HEADS * HEAD_DIM, SEQ), BF16),
        scratch_shapes=[
            pltpu.VMEM((CH, TQ), F32),
            pltpu.VMEM((CH, TQ), F32),
            pltpu.VMEM((CH, TQ), BF16),
            pltpu.VMEM((CH, TQ), BF16),
            pltpu.VMEM((8, TQ), F32),
            pltpu.VMEM((8, TQ), F32),
            pltpu.VMEM((1, TQ), F32),
            pltpu.VMEM((1, TQ), F32),
            pltpu.VMEM((ACC_ROWS, TQ), F32),
        ],
        compiler_params=_params(("arbitrary", "arbitrary")),
        name="attention",
    )(qT, k, vT)


def _out_ffn_kernel(oT_ref, h_ref, wo_ref, g_ref, wg_ref, wu_ref, wd_ref, gfin_ref,
                    out_ref, *, final):
    o = oT_ref[...].T
    h1 = h_ref[...] + _dot(o, wo_ref[...])
    xn = _rms(h1, g_ref[...]).astype(BF16)
    acc = h1
    for c in range(D_FF // FF_CHUNK):
        sl = slice(c * FF_CHUNK, (c + 1) * FF_CHUNK)
        gate = _dot(xn, wg_ref[:, sl])
        up = _dot(xn, wu_ref[:, sl])
        act = (gate * (1.0 / (1.0 + jnp.exp(-gate))) * up).astype(BF16)
        acc = acc + _dot(act, wd_ref[sl, :])
    if final:
        acc = _rms(acc, gfin_ref[...])
    out_ref[...] = acc


def _out_ffn(oT, h, wo, g, wg, wu, wd, gfin, final):
    n = SEQ // TM
    return pl.pallas_call(
        functools.partial(_out_ffn_kernel, final=final),
        grid=(n,),
        in_specs=[
            pl.BlockSpec((D_MODEL, TM), lambda i: (0, i)),
            pl.BlockSpec((TM, D_MODEL), lambda i: (i, 0)),
            _const_spec((D_MODEL, D_MODEL)),
            _const_spec((1, D_MODEL)),
            _const_spec((D_MODEL, D_FF)),
            _const_spec((D_MODEL, D_FF)),
            _const_spec((D_FF, D_MODEL)),
            _const_spec((1, D_MODEL)),
        ],
        out_specs=pl.BlockSpec((TM, D_MODEL), lambda i: (i, 0)),
        out_shape=jax.ShapeDtypeStruct((SEQ, D_MODEL), F32),
        compiler_params=_params(("arbitrary",)),
        name="out_ffn",
    )(oT, h, wo, g, wg, wu, wd, gfin)


def _rope_table_kernel(pos_ref, invf_ref, cos_ref, sin_ref, cosT_ref, sinT_ref):
    ang = pos_ref[...] * invf_ref[...]
    cos = jnp.cos(ang)
    sin = jnp.sin(ang)
    cos_ref[...] = cos
    sin_ref[...] = sin
    cosT_ref[...] = cos.T
    sinT_ref[...] = sin.T


def _rope_tables(pos_col, invf):
    n = SEQ // TM
    row = pl.BlockSpec((TM, LANES), lambda i: (i, 0))
    colT = pl.BlockSpec((LANES, TM), lambda i: (0, i))
    return pl.pallas_call(
        _rope_table_kernel,
        grid=(n,),
        in_specs=[pl.BlockSpec((TM, 1), lambda i: (i, 0)), _const_spec((1, LANES))],
        out_specs=[row, row, colT, colT],
        out_shape=[jax.ShapeDtypeStruct((SEQ, LANES), F32)] * 2
        + [jax.ShapeDtypeStruct((LANES, SEQ), F32)] * 2,
        compiler_params=_params(("arbitrary",)),
        name="rope_tables",
    )(pos_col, invf)


def _mla_kv_kernel(h_ref, g_ref, wa_ref, gckv_ref, cos_ref, sin_ref, wuk_ref,
                   p1_ref, p2_ref, wuvT_ref, k_ref, vT_ref):
    hn = _rms(h_ref[...], g_ref[...]).astype(BF16)
    a = _dot(hn, wa_ref[...])
    c = _rms(a[:, :KV_RANK], gckv_ref[...]).astype(BF16)
    x1 = a[:, KV_RANK:KV_RANK + LANES]
    x2 = a[:, KV_RANK + LANES:]
    cos = cos_ref[...]
    sin = sin_ref[...]
    r1 = (x1 * cos - x2 * sin).astype(BF16)
    r2 = (x1 * sin + x2 * cos).astype(BF16)
    k = _dot(c, wuk_ref[...]) + _dot(r1, p1_ref[...]) + _dot(r2, p2_ref[...])
    k_ref[...] = k.astype(BF16)
    vT_ref[...] = _dot(wuvT_ref[...], c.T).astype(BF16)


def _mla_kv(h, g, wa, gckv, cos, sin, wuk, p1, p2, wuvT):
    n = SEQ // TM
    return pl.pallas_call(
        _mla_kv_kernel,
        grid=(n,),
        in_specs=[
            pl.BlockSpec((TM, D_MODEL), lambda i: (i, 0)),
            _const_spec((1, D_MODEL)),
            _const_spec((D_MODEL, 4 * LANES)),
            _const_spec((1, KV_RANK)),
            pl.BlockSpec((TM, LANES), lambda i: (i, 0)),
            pl.BlockSpec((TM, LANES), lambda i: (i, 0)),
            _const_spec((KV_RANK, AUG)),
            _const_spec((LANES, AUG)),
            _const_spec((LANES, AUG)),
            _const_spec((D_MODEL, KV_RANK)),
        ],
        out_specs=[
            pl.BlockSpec((TM, AUG), lambda i: (i, 0)),
            pl.BlockSpec((D_MODEL, TM), lambda i: (0, i)),
        ],
        out_shape=[
            jax.ShapeDtypeStruct((SEQ, AUG), BF16),
            jax.ShapeDtypeStruct((D_MODEL, SEQ), BF16),
        ],
        compiler_params=_params(("arbitrary",)),
        name="mla_kv",
    )(h, g, wa, gckv, cos, sin, wuk, p1, p2, wuvT)


def _mla_q_kernel(h_ref, g_ref, wdq_ref, gcq_ref, wnT_ref, wx1T_ref, wx2T_ref,
                  cosT_ref, sinT_ref, qT_ref):
    scale = LOG2E / math.sqrt(MLA_NOPE + MLA_ROPE)
    xn = _rms(h_ref[...], g_ref[...]).astype(BF16)
    cq = _rms(_dot(xn, wdq_ref[...]), gcq_ref[...]).astype(BF16)
    cqT = cq.T
    nT = _dot(wnT_ref[...], cqT) * scale
    x1T = _dot(wx1T_ref[...], cqT)
    x2T = _dot(wx2T_ref[...], cqT)
    cosT = jnp.concatenate([cosT_ref[...]] * HEADS, axis=0)
    sinT = jnp.concatenate([sinT_ref[...]] * HEADS, axis=0)
    r1 = (x1T * cosT - x2T * sinT) * scale
    r2 = (x1T * sinT + x2T * cosT) * scale
    zeros = jnp.zeros((SLOT - MLA_NOPE - MLA_ROPE, TM), BF16)
    for h in range(HEADS):
        qT_ref[h, 0:MLA_NOPE, :] = nT[h * MLA_NOPE:(h + 1) * MLA_NOPE].astype(BF16)
        qT_ref[h, MLA_NOPE:MLA_NOPE + HALF_ROPE, :] = (
            r1[h * HALF_ROPE:(h + 1) * HALF_ROPE].astype(BF16))
        qT_ref[h, MLA_NOPE + HALF_ROPE:MLA_NOPE + MLA_ROPE, :] = (
            r2[h * HALF_ROPE:(h + 1) * HALF_ROPE].astype(BF16))
        qT_ref[h, MLA_NOPE + MLA_ROPE:, :] = zeros


def _mla_q(h, g, wdq, gcq, wnT, wx1T, wx2T, cosT, sinT):
    n = SEQ // TM
    return pl.pallas_call(
        _mla_q_kernel,
        grid=(n,),
        in_specs=[
            pl.BlockSpec((TM, D_MODEL), lambda i: (i, 0)),
            _const_spec((1, D_MODEL)),
            _const_spec((D_MODEL, Q_RANK)),
            _const_spec((1, Q_RANK)),
            _const_spec((HEADS * MLA_NOPE, Q_RANK)),
            _const_spec((HEADS * HALF_ROPE, Q_RANK)),
            _const_spec((HEADS * HALF_ROPE, Q_RANK)),
            pl.BlockSpec((HALF_ROPE, TM), lambda i: (0, i)),
            pl.BlockSpec((HALF_ROPE, TM), lambda i: (0, i)),
        ],
        out_specs=pl.BlockSpec((HEADS, SLOT, TM), lambda i: (0, 0, i)),
        out_shape=jax.ShapeDtypeStruct((HEADS, SLOT, SEQ), BF16),
        compiler_params=_params(("arbitrary",)),
        name="mla_q",
    )(h, g, wdq, gcq, wnT, wx1T, wx2T, cosT, sinT)


def _pad_heads_cols(w, width):
    k = w.shape[0]
    w = w.reshape(k, HEADS, width)
    w = jnp.pad(w, ((0, 0), (0, 0), (0, SLOT - width)))
    return w.reshape(k, AUG)


def _fox_bias_placement():
    pk = np.zeros((3 * LANES, AUG), np.float32)
    for h in range(HEADS):
        for j in range(3):
            pk[HEADS, h * SLOT + HEAD_DIM + j] = 1.0
            pk[j * LANES + h, h * SLOT + HEAD_DIM + 3 + j] = -1.0
    pq = np.zeros((AUG, 3 * LANES), np.float32)
    for h in range(HEADS):
        for j in range(3):
            pq[h * SLOT + HEAD_DIM + j, j * LANES + h] = 1.0
            pq[h * SLOT + HEAD_DIM + 3 + j, HEADS] = 1.0
    return jnp.asarray(pk, BF16), jnp.asarray(pq, BF16)


def _rope_placement():
    p1 = np.zeros((LANES, AUG), np.float32)
    p2 = np.zeros((LANES, AUG), np.float32)
    for h in range(HEADS):
        for j in range(HALF_ROPE):
            p1[j, h * SLOT + MLA_NOPE + j] = 1.0
            p2[j, h * SLOT + MLA_NOPE + HALF_ROPE + j] = 1.0
    return jnp.asarray(p1, BF16), jnp.asarray(p2, BF16)


def kernel(x, positions, attn_norm, ffn_norm, w_gate, w_up, w_down, fox_w_in, fox_b_f,
           fox_w_o, kv_norm, w_kv_a, ckv_norm, w_uk, w_uv, mla_w_dq, cq_norm, mla_w_uq,
           mla_w_o, final_norm):
    assert x.shape == (1, SEQ, D_MODEL)
    h = x.reshape(SEQ, D_MODEL)
    W = HEADS * HEAD_DIM
    pk, pqT = _fox_bias_placement()
    p1, p2 = _rope_placement()
    tri = jnp.asarray(np.tril(np.ones((TM, TM), np.float32)), BF16)
    row = lambda v: v.reshape(1, -1)

    def ffn_args(l):
        return (row(ffn_norm[l]), w_gate[l].astype(BF16), w_up[l].astype(BF16),
                w_down[l].astype(BF16), row(final_norm))

    for l in range(N_A):
        w_in = fox_w_in[l]
        wqT = _pad_heads_cols(w_in[:, :W], HEAD_DIM).T.astype(BF16)
        wk = _pad_heads_cols(w_in[:, W:2 * W], HEAD_DIM).astype(BF16)
        wvT = w_in[:, 2 * W:3 * W].T.astype(BF16)
        wf = jnp.pad(w_in[:, 3 * W:], ((0, 0), (0, LANES - HEADS))).astype(BF16)
        bf = jnp.pad(fox_b_f[l], (0, LANES - HEADS)).reshape(1, LANES)
        qT, k, vT = _fox_proj(h, row(attn_norm[l]), wqT, wk, wvT, wf, bf, pk, pqT, tri)
        oT = _attention(qT.reshape(HEADS, SLOT, SEQ), k, vT)
        h = _out_ffn(oT, h, fox_w_o[l].astype(BF16), *ffn_args(l), final=False)

    invf = ROPE_THETA ** (-jnp.arange(0, HALF_ROPE, dtype=F32) * 2.0 / MLA_ROPE)
    invf = jnp.pad(invf, (0, LANES - HALF_ROPE)).reshape(1, LANES)
    pos_col = positions.reshape(SEQ, 1).astype(F32)
    cos, sin, cosT, sinT = _rope_tables(pos_col, invf)
    wa = jnp.concatenate([
        w_kv_a[:, :KV_RANK],
        jnp.pad(w_kv_a[:, KV_RANK:KV_RANK + HALF_ROPE], ((0, 0), (0, LANES - HALF_ROPE))),
        jnp.pad(w_kv_a[:, KV_RANK + HALF_ROPE:], ((0, 0), (0, LANES - HALF_ROPE))),
    ], axis=1).astype(BF16)
    wuk = _pad_heads_cols(w_uk.reshape(KV_RANK, HEADS * MLA_NOPE), MLA_NOPE).astype(BF16)
    wuvT = w_uv.reshape(KV_RANK, HEADS * HEAD_DIM).T.astype(BF16)
    k_sh, vT_sh = _mla_kv(h, row(kv_norm), wa, row(ckv_norm), cos, sin, wuk, p1, p2, wuvT)

    for l in range(N_A, DEPTH):
        j = l - N_A
        wuq = mla_w_uq[j].reshape(Q_RANK, HEADS, MLA_NOPE + MLA_ROPE)
        wnT = wuq[:, :, :MLA_NOPE].reshape(Q_RANK, -1).T.astype(BF16)
        wx1T = wuq[:, :, MLA_NOPE:MLA_NOPE + HALF_ROPE].reshape(Q_RANK, -1).T.astype(BF16)
        wx2T = wuq[:, :, MLA_NOPE + HALF_ROPE:].reshape(Q_RANK, -1).T.astype(BF16)
        qT = _mla_q(h, row(attn_norm[l]), mla_w_dq[j].astype(BF16), row(cq_norm[j]),
                    wnT, wx1T, wx2T, cosT, sinT)
        oT = _attention(qT, k_sh, vT_sh)
        h = _out_ffn(oT, h, mla_w_o[j].astype(BF16), *ffn_args(l), final=(l == DEPTH - 1))

    return h.reshape(1, SEQ, D_MODEL)
```

```python
import functools
import math

import numpy as np
import jax
import jax.numpy as jnp
from jax import lax
from jax.experimental import pallas as pl
from jax.experimental.pallas import tpu as pltpu

F32 = jnp.float32
BF16 = jnp.bfloat16

D_MODEL = 1024
SEQ = 16384
DEPTH = 4
N_A = DEPTH // 2
RMS_EPS = 1e-6
HEADS = 16
HEAD_DIM = 64
MLA_NOPE = 64
MLA_ROPE = 32
HALF_ROPE = MLA_ROPE // 2
KV_RANK = 256
Q_RANK = 768
ROPE_THETA = 10000.0
D_FF = 2816

LANES = 128
SLOT = 128
AUG = HEADS * SLOT
LOG2E = math.log2(math.e)
NEG = -1e30

TM = 512
TQ = 1024
CH = TQ // 2
ACC_ROWS = HEAD_DIM + 16
FF_CHUNK = D_FF // 2
VMEM_LIMIT = 56 * 1024 * 1024


def _dot(a, b):
    return jnp.dot(a, b, preferred_element_type=F32)


def _rms(x, g):
    inv = lax.rsqrt(jnp.mean(x * x, axis=-1, keepdims=True) + RMS_EPS)
    return (x * inv) * g


def _split3(x):
    hi = x.astype(BF16)
    r = x - hi.astype(F32)
    mid = r.astype(BF16)
    lo = (r - mid.astype(F32)).astype(BF16)
    return hi, mid, lo


def _store_heads(k_ref, k):
    for h in range(HEADS):
        k_ref[h] = k[:, h * SLOT:(h + 1) * SLOT]


def _const_spec(shape):
    nd = len(shape)
    return pl.BlockSpec(shape, lambda *_: (0,) * nd, pipeline_mode=pl.Buffered(1))


def _params(sem):
    return pltpu.CompilerParams(dimension_semantics=sem, vmem_limit_bytes=VMEM_LIMIT)


def _fox_proj_kernel(h_ref, g_ref, wqT_ref, wk_ref, wvT_ref, wf_ref, bf_ref,
                     pk_ref, pqT_ref, tri_ref, qT_ref, k_ref, vT_ref, carry_ref):
    @pl.when(pl.program_id(0) == 0)
    def _():
        carry_ref[...] = jnp.zeros_like(carry_ref)

    xn = _rms(h_ref[...], g_ref[...]).astype(BF16)
    xnT = xn.T

    f = _dot(xn, wf_ref[...]) + bf_ref[...]
    logf = (jnp.minimum(f, 0.0) - jnp.log1p(jnp.exp(-jnp.abs(f)))) * LOG2E
    hi, mid, lo = _split3(logf)
    tri = tri_ref[...]
    cum = _dot(tri, hi) + _dot(tri, mid) + _dot(tri, lo) + carry_ref[...]
    carry_ref[...] = cum[TM - 1:TM, :]

    lane = lax.broadcasted_iota(jnp.int32, cum.shape, 1)
    c = jnp.where(lane < HEADS, cum, 1.0)
    c3 = jnp.concatenate(_split3(c), axis=1)
    c3T = jnp.concatenate(_split3(c.T), axis=0)

    k = _dot(xn, wk_ref[...]) + _dot(c3, pk_ref[...])
    _store_heads(k_ref, k.astype(BF16))
    qT =_dot(wqT_ref[...], xnT) * (LOG2E / math.sqrt(HEAD_DIM)) + _dot(pqT_ref[...], c3T)
    qT_ref[...] = qT.astype(BF16)
    vT_ref[...] = _dot(wvT_ref[...], xnT).astype(BF16)


def _fox_proj(h, g, wqT, wk, wvT, wf, bf, pk, pqT, tri):
    n = SEQ // TM
    return pl.pallas_call(
        _fox_proj_kernel,
        grid=(n,),
        in_specs=[
            pl.BlockSpec((TM, D_MODEL), lambda i: (i, 0)),
            _const_spec((1, D_MODEL)),
            _const_spec((AUG, D_MODEL)),
            _const_spec((D_MODEL, AUG)),
            _const_spec((D_MODEL, D_MODEL)),
            _const_spec((D_MODEL, LANES)),
            _const_spec((1, LANES)),
            _const_spec((3 * LANES, AUG)),
            _const_spec((AUG, 3 * LANES)),
            _const_spec((TM, TM)),
        ],
        out_specs=[
            pl.BlockSpec((AUG, TM), lambda i: (0, i)),
            pl.BlockSpec((HEADS, TM, SLOT), lambda i: (0, i, 0)),
            pl.BlockSpec((D_MODEL, TM), lambda i: (0, i)),
        ],
        out_shape=[
            jax.ShapeDtypeStruct((AUG, SEQ), BF16),
            jax.ShapeDtypeStruct((HEADS, SEQ, SLOT), BF16),
            jax.ShapeDtypeStruct((D_MODEL, SEQ), BF16),
        ],
        scratch_shapes=[pltpu.VMEM((1, LANES), F32)],
        compiler_params=_params(("arbitrary",)),
        name="fox_proj",
    )(h, g, wqT, wk, wvT, wf, bf, pk, pqT, tri)


def _attn_kernel(qT_ref, k_ref, vT_ref, o_ref, sa_scr, sb_scr, pa_scr, pb_scr,
                 smax_a_scr, smax_b_scr, alpha_b_scr, m_scr, acc_scr):
    i = pl.program_id(1)
    groups = CH // 8
    ALL = slice(0, TQ)
    HI = slice(CH, TQ)

    def scores(c, s_ref, smax_ref, cols=ALL, diag=False):
        n = cols.stop - cols.start
        off = pl.multiple_of(c * CH, CH)
        s = _dot(k_ref[pl.ds(off, CH), :], qT_ref[:, cols])
        if diag:
            row = lax.broadcasted_iota(jnp.int32, (CH, n), 0)
            col = lax.broadcasted_iota(jnp.int32, (CH, n), 1)
            s = jnp.where(row <= col, s, NEG)
        s_ref[:, cols] = s
        smax_ref[:, cols] = jnp.max(s.reshape(groups, 8, n), axis=0)

    def probs(s_ref, p_ref, smax_ref, cols=ALL):
        m_old = m_scr[:, cols]
        m_new = jnp.maximum(m_old, jnp.max(smax_ref[:, cols], axis=0, keepdims=True))
        alpha = jnp.exp2(m_old - m_new)
        m_scr[:, cols] = m_new
        p_ref[:, cols] = jnp.exp2(s_ref[:, cols] - m_new).astype(BF16)
        return alpha

    ones_rows = jnp.ones((ACC_ROWS - HEAD_DIM, CH), BF16)

    def pv(c, p_ref, alpha, cols=ALL):
        off = pl.multiple_of(c * CH, CH)
        lhs = jnp.concatenate([vT_ref[:, pl.ds(off, CH)], ones_rows], axis=0)
        acc_scr[:, cols] = alpha * acc_scr[:, cols] + _dot(lhs, p_ref[:, cols])

    def score_pair(t, diag):
        scores(2 * t, sa_scr, smax_a_scr, ALL, diag)
        scores(2 * t + 1, sb_scr, smax_b_scr, HI if diag else ALL, diag)

    def pair(t, next_diag, first=False, diag=False):
        cols_b = HI if diag else ALL
        if not first:
            pv(jnp.maximum(2 * t - 1, 0), pb_scr, alpha_b_scr[...])
        alpha_a = probs(sa_scr, pa_scr, smax_a_scr)
        if next_diag is not None:
            scores(2 * t + 2, sa_scr, smax_a_scr, ALL, next_diag)
        pv(2 * t, pa_scr, alpha_a)
        alpha_b_scr[:, cols_b] = probs(sb_scr, pb_scr, smax_b_scr, cols_b)
        if next_diag is not None:
            scores(2 * t + 3, sb_scr, smax_b_scr, HI if next_diag else ALL, next_diag)

    def init():
        m_scr[...] = jnp.full(m_scr.shape, NEG, F32)
        acc_scr[...] = jnp.zeros(acc_scr.shape, F32)

    def finish():
        pv(2 * i + 1, pb_scr, alpha_b_scr[:, HI], HI)
        l = acc_scr[HEAD_DIM:HEAD_DIM + 1, :]
        o_ref[...] = (acc_scr[:HEAD_DIM, :] / l).astype(BF16)

    @pl.when(i == 0)
    def _():
        init()
        score_pair(0, True)
        pair(0, None, first=True, diag=True)
        finish()

    @pl.when(i > 0)
    def _():
        init()
        pb_scr[...] = jnp.zeros(pb_scr.shape, BF16)
        alpha_b_scr[...] = jnp.ones(alpha_b_scr.shape, F32)
        score_pair(0, False)

        def body(t, carry):
            pair(t, False)
            return carry

        lax.fori_loop(0, i - 1, body, 0)
        pair(i - 1, True)
        pair(i, None, diag=True)
        finish()


def _attention(qT, k, vT):
    return pl.pallas_call(
        _attn_kernel,
        grid=(HEADS, SEQ // TQ),
        in_specs=[
            pl.BlockSpec((None, SLOT, TQ), lambda h, i: (h, 0, i)),
            pl.BlockSpec((None, SEQ, SLOT), lambda h, i: (h, 0, 0)),
            pl.BlockSpec((HEAD_DIM, SEQ), lambda h, i: (h, 0)),
        ],
        out_specs=pl.BlockSpec((HEAD_DIM, TQ), lambda h, i: (h, i)),
        out_shape=jax.ShapeDtypeStruct((HEADS * HEAD_DIM, SEQ), BF16),
        scratch_shapes=[
            pltpu.VMEM((CH, TQ), F32),
            pltpu.VMEM((CH, TQ), F32),
            pltpu.VMEM((CH, TQ), BF16),
            pltpu.VMEM((CH, TQ), BF16),
            pltpu.VMEM((8, TQ), F32),
            pltpu.VMEM((8, TQ), F32),
            pltpu.VMEM((1, TQ), F32),
            pltpu.VMEM((1, TQ), F32),
            pltpu.VMEM((ACC_ROWS, TQ), F32),
        ],
        compiler_params=_params(("arbitrary", "arbitrary")),
        name="attention",
    )(qT, k, vT)


def _out_ffn_kernel(oT_ref, h_ref, wo_ref, g_ref, wg_ref, wu_ref, wd_ref, gfin_ref,
                    out_ref, *, final):
    o = oT_ref[...].T
    h1 = h_ref[...] + _dot(o, wo_ref[...])
    xn = _rms(h1, g_ref[...]).astype(BF16)
    acc = h1
    for c in range(D_FF // FF_CHUNK):
        sl = slice(c * FF_CHUNK, (c + 1) * FF_CHUNK)
        gate = _dot(xn, wg_ref[:, sl])
        up = _dot(xn, wu_ref[:, sl])
        act = (gate * (1.0 / (1.0 + jnp.exp(-gate))) * up).astype(BF16)
        acc = acc + _dot(act, wd_ref[sl, :])
    if final:
        acc = _rms(acc, gfin_ref[...])
    out_ref[...] = acc


def _out_ffn(oT, h, wo, g, wg, wu, wd, gfin, final):
    n = SEQ // TM
    return pl.pallas_call(
        functools.partial(_out_ffn_kernel, final=final),
        grid=(n,),
        in_specs=[
            pl.BlockSpec((D_MODEL, TM), lambda i: (0, i)),
            pl.BlockSpec((TM, D_MODEL), lambda i: (i, 0)),
            _const_spec((D_MODEL, D_MODEL)),
            _const_spec((1, D_MODEL)),
            _const_spec((D_MODEL, D_FF)),
            _const_spec((D_MODEL, D_FF)),
            _const_spec((D_FF, D_MODEL)),
            _const_spec((1, D_MODEL)),
        ],
        out_specs=pl.BlockSpec((TM, D_MODEL), lambda i: (i, 0)),
        out_shape=jax.ShapeDtypeStruct((SEQ, D_MODEL), F32),
        compiler_params=_params(("arbitrary",)),
        name="out_ffn",
    )(oT, h, wo, g, wg, wu, wd, gfin)


def _rope_table_kernel(pos_ref, invf_ref, cos_ref, sin_ref, cosT_ref, sinT_ref):
    ang = pos_ref[...] * invf_ref[...]
    cos = jnp.cos(ang)
    sin = jnp.sin(ang)
    cos_ref[...] = cos
    sin_ref[...] = sin
    cosT_ref[...] = cos.T
    sinT_ref[...] = sin.T


def _rope_tables(pos_col, invf):
    n = SEQ // TM
    row = pl.BlockSpec((TM, LANES), lambda i: (i, 0))
    colT = pl.BlockSpec((LANES, TM), lambda i: (0, i))
    return pl.pallas_call(
        _rope_table_kernel,
        grid=(n,),
        in_specs=[pl.BlockSpec((TM, 1), lambda i: (i, 0)), _const_spec((1, LANES))],
        out_specs=[row, row, colT, colT],
        out_shape=[jax.ShapeDtypeStruct((SEQ, LANES), F32)] * 2
        + [jax.ShapeDtypeStruct((LANES, SEQ), F32)] * 2,
        compiler_params=_params(("arbitrary",)),
        name="rope_tables",
    )(pos_col, invf)


def _mla_kv_kernel(h_ref, g_ref, wa_ref, gckv_ref, cos_ref, sin_ref, wuk_ref,
                   p1_ref, p2_ref, wuvT_ref, k_ref, vT_ref):
    hn = _rms(h_ref[...], g_ref[...]).astype(BF16)
    a = _dot(hn, wa_ref[...])
    c = _rms(a[:, :KV_RANK], gckv_ref[...]).astype(BF16)
    x1 = a[:, KV_RANK:KV_RANK + LANES]
    x2 = a[:, KV_RANK + LANES:]
    cos = cos_ref[...]
    sin = sin_ref[...]
    r1 = (x1 * cos - x2 * sin).astype(BF16)
    r2 = (x1 * sin + x2 * cos).astype(BF16)
    k = _dot(c, wuk_ref[...]) + _dot(r1, p1_ref[...]) + _dot(r2, p2_ref[...])
    _store_heads(k_ref, k.astype(BF16))
    vT_ref[...] = _dot(wuvT_ref[...], c.T).astype(BF16)


def _mla_kv(h, g, wa, gckv, cos, sin, wuk, p1, p2, wuvT):
    n = SEQ // TM
    return pl.pallas_call(
        _mla_kv_kernel,
        grid=(n,),
        in_specs=[
            pl.BlockSpec((TM, D_MODEL), lambda i: (i, 0)),
            _const_spec((1, D_MODEL)),
            _const_spec((D_MODEL, 4 * LANES)),
            _const_spec((1, KV_RANK)),
            pl.BlockSpec((TM, LANES), lambda i: (i, 0)),
            pl.BlockSpec((TM, LANES), lambda i: (i, 0)),
            _const_spec((KV_RANK, AUG)),
            _const_spec((LANES, AUG)),
            _const_spec((LANES, AUG)),
            _const_spec((D_MODEL, KV_RANK)),
        ],
        out_specs=[
            pl.BlockSpec((HEADS, TM, SLOT), lambda i: (0, i, 0)),
            pl.BlockSpec((D_MODEL, TM), lambda i: (0, i)),
        ],
        out_shape=[
            jax.ShapeDtypeStruct((HEADS, SEQ, SLOT), BF16),
            jax.ShapeDtypeStruct((D_MODEL, SEQ), BF16),
        ],
        compiler_params=_params(("arbitrary",)),
        name="mla_kv",
    )(h, g, wa, gckv, cos, sin, wuk, p1, p2, wuvT)


def _mla_q_kernel(h_ref, g_ref, wdq_ref, gcq_ref, wnT_ref, wx1T_ref, wx2T_ref,
                  cosT_ref, sinT_ref, qT_ref):
    scale = LOG2E / math.sqrt(MLA_NOPE + MLA_ROPE)
    xn = _rms(h_ref[...], g_ref[...]).astype(BF16)
    cq = _rms(_dot(xn, wdq_ref[...]), gcq_ref[...]).astype(BF16)
    cqT = cq.T
    nT = _dot(wnT_ref[...], cqT) * scale
    x1T = _dot(wx1T_ref[...], cqT)
    x2T = _dot(wx2T_ref[...], cqT)
    cosT = jnp.concatenate([cosT_ref[...]] * HEADS, axis=0)
    sinT = jnp.concatenate([sinT_ref[...]] * HEADS, axis=0)
    r1 = (x1T * cosT - x2T * sinT) * scale
    r2 = (x1T * sinT + x2T * cosT) * scale
    zeros = jnp.zeros((SLOT - MLA_NOPE - MLA_ROPE, TM), BF16)
    for h in range(HEADS):
        qT_ref[h, 0:MLA_NOPE, :] = nT[h * MLA_NOPE:(h + 1) * MLA_NOPE].astype(BF16)
        qT_ref[h, MLA_NOPE:MLA_NOPE + HALF_ROPE, :] = (
            r1[h * HALF_ROPE:(h + 1) * HALF_ROPE].astype(BF16))
        qT_ref[h, MLA_NOPE + HALF_ROPE:MLA_NOPE + MLA_ROPE, :] = (
            r2[h * HALF_ROPE:(h + 1) * HALF_ROPE].astype(BF16))
        qT_ref[h, MLA_NOPE + MLA_ROPE:, :] = zeros


def _mla_q(h, g, wdq, gcq, wnT, wx1T, wx2T, cosT, sinT):
    n = SEQ // TM
    return pl.pallas_call(
        _mla_q_kernel,
        grid=(n,),
        in_specs=[
            pl.BlockSpec((TM, D_MODEL), lambda i: (i, 0)),
            _const_spec((1, D_MODEL)),
            _const_spec((D_MODEL, Q_RANK)),
            _const_spec((1, Q_RANK)),
            _const_spec((HEADS * MLA_NOPE, Q_RANK)),
            _const_spec((HEADS * HALF_ROPE, Q_RANK)),
            _const_spec((HEADS * HALF_ROPE, Q_RANK)),
            pl.BlockSpec((HALF_ROPE, TM), lambda i: (0, i)),
            pl.BlockSpec((HALF_ROPE, TM), lambda i: (0, i)),
        ],
        out_specs=pl.BlockSpec((HEADS, SLOT, TM), lambda i: (0, 0, i)),
        out_shape=jax.ShapeDtypeStruct((HEADS, SLOT, SEQ), BF16),
        compiler_params=_params(("arbitrary",)),
        name="mla_q",
    )(h, g, wdq, gcq, wnT, wx1T, wx2T, cosT, sinT)


def _pad_heads_cols(w, width):
    k = w.shape[0]
    w = w.reshape(k, HEADS, width)
    w = jnp.pad(w, ((0, 0), (0, 0), (0, SLOT - width)))
    return w.reshape(k, AUG)


def _fox_bias_placement():
    pk = np.zeros((3 * LANES, AUG), np.float32)
    for h in range(HEADS):
        for j in range(3):
            pk[HEADS, h * SLOT + HEAD_DIM + j] = 1.0
            pk[j * LANES + h, h * SLOT + HEAD_DIM + 3 + j] = -1.0
    pq = np.zeros((AUG, 3 * LANES), np.float32)
    for h in range(HEADS):
        for j in range(3):
            pq[h * SLOT + HEAD_DIM + j, j * LANES + h] = 1.0
            pq[h * SLOT + HEAD_DIM + 3 + j, HEADS] = 1.0
    return jnp.asarray(pk, BF16), jnp.asarray(pq, BF16)


def _rope_placement():
    p1 = np.zeros((LANES, AUG), np.float32)
    p2 = np.zeros((LANES, AUG), np.float32)
    for h in range(HEADS):
        for j in range(HALF_ROPE):
            p1[j, h * SLOT + MLA_NOPE + j] = 1.0
            p2[j, h * SLOT + MLA_NOPE + HALF_ROPE + j] = 1.0
    return jnp.asarray(p1, BF16), jnp.asarray(p2, BF16)


def kernel(x, positions, attn_norm, ffn_norm, w_gate, w_up, w_down, fox_w_in, fox_b_f,
           fox_w_o, kv_norm, w_kv_a, ckv_norm, w_uk, w_uv, mla_w_dq, cq_norm, mla_w_uq,
           mla_w_o, final_norm):
    assert x.shape == (1, SEQ, D_MODEL)
    h = x.reshape(SEQ, D_MODEL)
    W = HEADS * HEAD_DIM
    pk, pqT = _fox_bias_placement()
    p1, p2 = _rope_placement()
    tri = jnp.asarray(np.tril(np.ones((TM, TM), np.float32)), BF16)
    row = lambda v: v.reshape(1, -1)

    def ffn_args(l):
        return (row(ffn_norm[l]), w_gate[l].astype(BF16), w_up[l].astype(BF16),
                w_down[l].astype(BF16), row(final_norm))

    for l in range(N_A):
        w_in = fox_w_in[l]
        wqT = _pad_heads_cols(w_in[:, :W], HEAD_DIM).T.astype(BF16)
        wk = _pad_heads_cols(w_in[:, W:2 * W], HEAD_DIM).astype(BF16)
        wvT = w_in[:, 2 * W:3 * W].T.astype(BF16)
        wf = jnp.pad(w_in[:, 3 * W:], ((0, 0), (0, LANES - HEADS))).astype(BF16)
        bf = jnp.pad(fox_b_f[l], (0, LANES - HEADS)).reshape(1, LANES)
        qT, k, vT = _fox_proj(h, row(attn_norm[l]), wqT, wk, wvT, wf, bf, pk, pqT, tri)
        oT = _attention(qT.reshape(HEADS, SLOT, SEQ), k, vT)
        h = _out_ffn(oT, h, fox_w_o[l].astype(BF16), *ffn_args(l), final=False)

    invf = ROPE_THETA ** (-jnp.arange(0, HALF_ROPE, dtype=F32) * 2.0 / MLA_ROPE)
    invf = jnp.pad(invf, (0, LANES - HALF_ROPE)).reshape(1, LANES)
    pos_col = positions.reshape(SEQ, 1).astype(F32)
    cos, sin, cosT, sinT = _rope_tables(pos_col, invf)
    wa = jnp.concatenate([
        w_kv_a[:, :KV_RANK],
        jnp.pad(w_kv_a[:, KV_RANK:KV_RANK + HALF_ROPE], ((0, 0), (0, LANES - HALF_ROPE))),
        jnp.pad(w_kv_a[:, KV_RANK + HALF_ROPE:], ((0, 0), (0, LANES - HALF_ROPE))),
    ], axis=1).astype(BF16)
    wuk = _pad_heads_cols(w_uk.reshape(KV_RANK, HEADS * MLA_NOPE), MLA_NOPE).astype(BF16)
    wuvT = w_uv.reshape(KV_RANK, HEADS * HEAD_DIM).T.astype(BF16)
    k_sh, vT_sh = _mla_kv(h, row(kv_norm), wa, row(ckv_norm), cos, sin, wuk, p1, p2, wuvT)

    for l in range(N_A, DEPTH):
        j = l - N_A
        wuq = mla_w_uq[j].reshape(Q_RANK, HEADS, MLA_NOPE + MLA_ROPE)
        wnT = wuq[:, :, :MLA_NOPE].reshape(Q_RANK, -1).T.astype(BF16)
        wx1T = wuq[:, :, MLA_NOPE:MLA_NOPE + HALF_ROPE].reshape(Q_RANK, -1).T.astype(BF16)
        wx2T = wuq[:, :, MLA_NOPE + HALF_ROPE:].reshape(Q_RANK, -1).T.astype(BF16)
        qT = _mla_q(h, row(attn_norm[l]), mla_w_dq[j].astype(BF16), row(cq_norm[j]),
                    wnT, wx1T, wx2T, cosT, sinT)
        oT = _attention(qT, k_sh, vT_sh)
        h = _out_ffn(oT, h, mla_w_o[j].astype(BF16), *ffn_args(l), final=(l == DEPTH - 1))

    return h.reshape(1, SEQ, D_MODEL)
```

```python
import functools
import math

import numpy as np
import jax
import jax.numpy as jnp
from jax import lax
from jax.experimental import pallas as pl
from jax.experimental.pallas import tpu as pltpu

F32 = jnp.float32
BF16 = jnp.bfloat16

D_MODEL = 1024
SEQ = 16384
DEPTH = 4
N_A = DEPTH // 2
RMS_EPS = 1e-6
HEADS = 16
HEAD_DIM = 64
MLA_NOPE = 64
MLA_ROPE = 32
HALF_ROPE = MLA_ROPE // 2
KV_RANK = 256
Q_RANK = 768
ROPE_THETA = 10000.0
D_FF = 2816

LANES = 128
SLOT = 128
AUG = HEADS * SLOT
LOG2E = math.log2(math.e)
NEG = -1e30

TM = 512
TQ = 1024
CH = TQ // 2
ACC_ROWS = HEAD_DIM + 16
FF_CHUNK = D_FF // 2
VMEM_LIMIT = 56 * 1024 * 1024


def _dot(a, b):
    return jnp.dot(a, b, preferred_element_type=F32)


def _rms(x, g):
    inv = lax.rsqrt(jnp.mean(x * x, axis=-1, keepdims=True) + RMS_EPS)
    return (x * inv) * g


def _split3(x):
    hi = x.astype(BF16)
    r = x - hi.astype(F32)
    mid = r.astype(BF16)
    lo = (r - mid.astype(F32)).astype(BF16)
    return hi, mid, lo


def _store_heads(k_ref, k):
    for h in range(HEADS):
        k_ref[h] = k[:, h * SLOT:(h + 1) * SLOT]


def _const_spec(shape):
    nd = len(shape)
    return pl.BlockSpec(shape, lambda *_: (0,) * nd, pipeline_mode=pl.Buffered(1))


def _params(sem):
    return pltpu.CompilerParams(dimension_semantics=sem, vmem_limit_bytes=VMEM_LIMIT)


def _fox_proj_kernel(h_ref, g_ref, wqT_ref, wk_ref, wvT_ref, wf_ref, bf_ref,
                     pk_ref, pqT_ref, tri_ref, qT_ref, k_ref, vT_ref, carry_ref):
    @pl.when(pl.program_id(0) == 0)
    def _():
        carry_ref[...] = jnp.zeros_like(carry_ref)

    xn = _rms(h_ref[...], g_ref[...]).astype(BF16)
    xnT = xn.T

    f = _dot(xn, wf_ref[...]) + bf_ref[...]
    logf = (jnp.minimum(f, 0.0) - jnp.log1p(jnp.exp(-jnp.abs(f)))) * LOG2E
    hi, mid, lo = _split3(logf)
    tri = tri_ref[...]
    cum = _dot(tri, hi) + _dot(tri, mid) + _dot(tri, lo) + carry_ref[...]
    carry_ref[...] = cum[TM - 1:TM, :]

    lane = lax.broadcasted_iota(jnp.int32, cum.shape, 1)
    c = jnp.where(lane < HEADS, cum, 1.0)
    c3 = jnp.concatenate(_split3(c), axis=1)
    c3T = jnp.concatenate(_split3(c.T), axis=0)

    k = _dot(xn, wk_ref[...]) + _dot(c3, pk_ref[...])
    _store_heads(k_ref, k.astype(BF16))
    qT =_dot(wqT_ref[...], xnT) * (LOG2E / math.sqrt(HEAD_DIM)) + _dot(pqT_ref[...], c3T)
    qT_ref[...] = qT.astype(BF16)
    vT_ref[...] = _dot(wvT_ref[...], xnT).astype(BF16)


def _fox_proj(h, g, wqT, wk, wvT, wf, bf, pk, pqT, tri):
    n = SEQ // TM
    return pl.pallas_call(
        _fox_proj_kernel,
        grid=(n,),
        in_specs=[
            pl.BlockSpec((TM, D_MODEL), lambda i: (i, 0)),
            _const_spec((1, D_MODEL)),
            _const_spec((AUG, D_MODEL)),
            _const_spec((D_MODEL, AUG)),
            _const_spec((D_MODEL, D_MODEL)),
            _const_spec((D_MODEL, LANES)),
            _const_spec((1, LANES)),
            _const_spec((3 * LANES, AUG)),
            _const_spec((AUG, 3 * LANES)),
            _const_spec((TM, TM)),
        ],
        out_specs=[
            pl.BlockSpec((AUG, TM), lambda i: (0, i)),
            pl.BlockSpec((HEADS, TM, SLOT), lambda i: (0, i, 0)),
            pl.BlockSpec((D_MODEL, TM), lambda i: (0, i)),
        ],
        out_shape=[
            jax.ShapeDtypeStruct((AUG, SEQ), BF16),
            jax.ShapeDtypeStruct((HEADS, SEQ, SLOT), BF16),
            jax.ShapeDtypeStruct((D_MODEL, SEQ), BF16),
        ],
        scratch_shapes=[pltpu.VMEM((1, LANES), F32)],
        compiler_params=_params(("arbitrary",)),
        name="fox_proj",
    )(h, g, wqT, wk, wvT, wf, bf, pk, pqT, tri)


def _attn_kernel(qT_ref, k_ref, vT_ref, o_ref, sa_scr, sb_scr, pa_scr, pb_scr,
                 smax_a_scr, smax_b_scr, alpha_b_scr, m_scr, acc_scr):
    _attn_tile(jnp.int32(0), True, qT_ref, k_ref, vT_ref, o_ref, sa_scr, sb_scr, pa_scr, pb_scr,
               smax_a_scr, smax_b_scr, alpha_b_scr, m_scr, acc_scr)

    def tile(i, carry):
        _attn_tile(i, False, qT_ref, k_ref, vT_ref, o_ref, sa_scr, sb_scr, pa_scr, pb_scr,
                   smax_a_scr, smax_b_scr, alpha_b_scr, m_scr, acc_scr)
        return carry

    lax.fori_loop(1, SEQ // TQ, tile, 0)


def _attn_tile(i, is_first, qT_ref, k_ref, vT_ref, o_ref, sa_scr, sb_scr, pa_scr, pb_scr,
               smax_a_scr, smax_b_scr, alpha_b_scr, m_scr, acc_scr):
    groups = CH // 8
    ALL = slice(0, TQ)
    HI = slice(CH, TQ)
    q0 = pl.multiple_of(i * TQ, TQ)

    def scores(c, s_ref, smax_ref, cols=ALL, diag=False):
        n = cols.stop - cols.start
        off = pl.multiple_of(c * CH, CH)
        q_cols = pl.ds(pl.multiple_of(q0 + cols.start, CH), n)
        s = _dot(k_ref[pl.ds(off, CH), :], qT_ref[:, q_cols])
        if diag:
            row = lax.broadcasted_iota(jnp.int32, (CH, n), 0)
            col = lax.broadcasted_iota(jnp.int32, (CH, n), 1)
            s = jnp.where(row <= col, s, NEG)
        s_ref[:, cols] = s
        smax_ref[:, cols] = jnp.max(s.reshape(groups, 8, n), axis=0)

    def probs(s_ref, p_ref, smax_ref, cols=ALL):
        m_old = m_scr[:, cols]
        m_new = jnp.maximum(m_old, jnp.max(smax_ref[:, cols], axis=0, keepdims=True))
        alpha = jnp.exp2(m_old - m_new)
        m_scr[:, cols] = m_new
        p_ref[:, cols] = jnp.exp2(s_ref[:, cols] - m_new).astype(BF16)
        return alpha

    ones_rows = jnp.ones((ACC_ROWS - HEAD_DIM, CH), BF16)

    def pv(c, p_ref, alpha, cols=ALL):
        off = pl.multiple_of(c * CH, CH)
        lhs = jnp.concatenate([vT_ref[:, pl.ds(off, CH)], ones_rows], axis=0)
        acc_scr[:, cols] = alpha * acc_scr[:, cols] + _dot(lhs, p_ref[:, cols])

    def score_pair(t, diag):
        scores(2 * t, sa_scr, smax_a_scr, ALL, diag)
        scores(2 * t + 1, sb_scr, smax_b_scr, HI if diag else ALL, diag)

    def pair(t, next_diag, first=False, diag=False):
        cols_b = HI if diag else ALL
        if not first:
            pv(jnp.maximum(2 * t - 1, 0), pb_scr, alpha_b_scr[...])
        alpha_a = probs(sa_scr, pa_scr, smax_a_scr)
        if next_diag is not None:
            scores(2 * t + 2, sa_scr, smax_a_scr, ALL, next_diag)
        pv(2 * t, pa_scr, alpha_a)
        alpha_b_scr[:, cols_b] = probs(sb_scr, pb_scr, smax_b_scr, cols_b)
        if next_diag is not None:
            scores(2 * t + 3, sb_scr, smax_b_scr, HI if next_diag else ALL, next_diag)

    def init():
        m_scr[...] = jnp.full(m_scr.shape, NEG, F32)
        acc_scr[...] = jnp.zeros(acc_scr.shape, F32)

    def finish():
        pv(2 * i + 1, pb_scr, alpha_b_scr[:, HI], HI)
        l = acc_scr[HEAD_DIM:HEAD_DIM + 1, :]
        o_ref[:, pl.ds(q0, TQ)] = (acc_scr[:HEAD_DIM, :] / l).astype(BF16)

    if is_first:
        init()
        score_pair(0, True)
        pair(0, None, first=True, diag=True)
        finish()
    else:
        init()
        pb_scr[...] = jnp.zeros(pb_scr.shape, BF16)
        alpha_b_scr[...] = jnp.ones(alpha_b_scr.shape, F32)
        score_pair(0, False)

        def body(t, carry):
            pair(t, False)
            return carry

        lax.fori_loop(0, i - 1, body, 0)
        pair(i - 1, True)
        pair(i, None, diag=True)
        finish()


def _attention(qT, k, vT):
    return pl.pallas_call(
        _attn_kernel,
        grid=(HEADS,),
        in_specs=[
            pl.BlockSpec((None, SLOT, SEQ), lambda h: (h, 0, 0)),
            pl.BlockSpec((None, SEQ, SLOT), lambda h: (h, 0, 0)),
            pl.BlockSpec((HEAD_DIM, SEQ), lambda h: (h, 0)),
        ],
        out_specs=pl.BlockSpec((HEAD_DIM, SEQ), lambda h: (h, 0)),
        out_shape=jax.ShapeDtypeStruct((HEADS * HEAD_DIM, SEQ), BF16),
        scratch_shapes=[
            pltpu.VMEM((CH, TQ), F32),
            pltpu.VMEM((CH, TQ), F32),
            pltpu.VMEM((CH, TQ), BF16),
            pltpu.VMEM((CH, TQ), BF16),
            pltpu.VMEM((8, TQ), F32),
            pltpu.VMEM((8, TQ), F32),
            pltpu.VMEM((1, TQ), F32),
            pltpu.VMEM((1, TQ), F32),
            pltpu.VMEM((ACC_ROWS, TQ), F32),
        ],
        compiler_params=_params(("arbitrary",)),
        name="attention",
    )(qT, k, vT)


def _out_ffn_kernel(oT_ref, h_ref, wo_ref, g_ref, wg_ref, wu_ref, wd_ref, gfin_ref,
                    out_ref, *, final):
    o = oT_ref[...].T
    h1 = h_ref[...] + _dot(o, wo_ref[...])
    xn = _rms(h1, g_ref[...]).astype(BF16)
    acc = h1
    for c in range(D_FF // FF_CHUNK):
        sl = slice(c * FF_CHUNK, (c + 1) * FF_CHUNK)
        gate = _dot(xn, wg_ref[:, sl])
        up = _dot(xn, wu_ref[:, sl])
        act = (gate * (1.0 / (1.0 + jnp.exp(-gate))) * up).astype(BF16)
        acc = acc + _dot(act, wd_ref[sl, :])
    if final:
        acc = _rms(acc, gfin_ref[...])
    out_ref[...] = acc


def _out_ffn(oT, h, wo, g, wg, wu, wd, gfin, final):
    n = SEQ // TM
    return pl.pallas_call(
        functools.partial(_out_ffn_kernel, final=final),
        grid=(n,),
        in_specs=[
            pl.BlockSpec((D_MODEL, TM), lambda i: (0, i)),
            pl.BlockSpec((TM, D_MODEL), lambda i: (i, 0)),
            _const_spec((D_MODEL, D_MODEL)),
            _const_spec((1, D_MODEL)),
            _const_spec((D_MODEL, D_FF)),
            _const_spec((D_MODEL, D_FF)),
            _const_spec((D_FF, D_MODEL)),
            _const_spec((1, D_MODEL)),
        ],
        out_specs=pl.BlockSpec((TM, D_MODEL), lambda i: (i, 0)),
        out_shape=jax.ShapeDtypeStruct((SEQ, D_MODEL), F32),
        compiler_params=_params(("arbitrary",)),
        name="out_ffn",
    )(oT, h, wo, g, wg, wu, wd, gfin)


def _rope_table_kernel(pos_ref, invf_ref, cos_ref, sin_ref, cosT_ref, sinT_ref):
    ang = pos_ref[...] * invf_ref[...]
    cos = jnp.cos(ang)
    sin = jnp.sin(ang)
    cos_ref[...] = cos
    sin_ref[...] = sin
    cosT_ref[...] = cos.T
    sinT_ref[...] = sin.T


def _rope_tables(pos_col, invf):
    n = SEQ // TM
    row = pl.BlockSpec((TM, LANES), lambda i: (i, 0))
    colT = pl.BlockSpec((LANES, TM), lambda i: (0, i))
    return pl.pallas_call(
        _rope_table_kernel,
        grid=(n,),
        in_specs=[pl.BlockSpec((TM, 1), lambda i: (i, 0)), _const_spec((1, LANES))],
        out_specs=[row, row, colT, colT],
        out_shape=[jax.ShapeDtypeStruct((SEQ, LANES), F32)] * 2
        + [jax.ShapeDtypeStruct((LANES, SEQ), F32)] * 2,
        compiler_params=_params(("arbitrary",)),
        name="rope_tables",
    )(pos_col, invf)


def _mla_kv_kernel(h_ref, g_ref, wa_ref, gckv_ref, cos_ref, sin_ref, wuk_ref,
                   p1_ref, p2_ref, wuvT_ref, k_ref, vT_ref):
    hn = _rms(h_ref[...], g_ref[...]).astype(BF16)
    a = _dot(hn, wa_ref[...])
    c = _rms(a[:, :KV_RANK], gckv_ref[...]).astype(BF16)
    x1 = a[:, KV_RANK:KV_RANK + LANES]
    x2 = a[:, KV_RANK + LANES:]
    cos = cos_ref[...]
    sin = sin_ref[...]
    r1 = (x1 * cos - x2 * sin).astype(BF16)
    r2 = (x1 * sin + x2 * cos).astype(BF16)
    k = _dot(c, wuk_ref[...]) + _dot(r1, p1_ref[...]) + _dot(r2, p2_ref[...])
    _store_heads(k_ref, k.astype(BF16))
    vT_ref[...] = _dot(wuvT_ref[...], c.T).astype(BF16)


def _mla_kv(h, g, wa, gckv, cos, sin, wuk, p1, p2, wuvT):
    n = SEQ // TM
    return pl.pallas_call(
        _mla_kv_kernel,
        grid=(n,),
        in_specs=[
            pl.BlockSpec((TM, D_MODEL), lambda i: (i, 0)),
            _const_spec((1, D_MODEL)),
            _const_spec((D_MODEL, 4 * LANES)),
            _const_spec((1, KV_RANK)),
            pl.BlockSpec((TM, LANES), lambda i: (i, 0)),
            pl.BlockSpec((TM, LANES), lambda i: (i, 0)),
            _const_spec((KV_RANK, AUG)),
            _const_spec((LANES, AUG)),
            _const_spec((LANES, AUG)),
            _const_spec((D_MODEL, KV_RANK)),
        ],
        out_specs=[
            pl.BlockSpec((HEADS, TM, SLOT), lambda i: (0, i, 0)),
            pl.BlockSpec((D_MODEL, TM), lambda i: (0, i)),
        ],
        out_shape=[
            jax.ShapeDtypeStruct((HEADS, SEQ, SLOT), BF16),
            jax.ShapeDtypeStruct((D_MODEL, SEQ), BF16),
        ],
        compiler_params=_params(("arbitrary",)),
        name="mla_kv",
    )(h, g, wa, gckv, cos, sin, wuk, p1, p2, wuvT)


def _mla_q_kernel(h_ref, g_ref, wdq_ref, gcq_ref, wnT_ref, wx1T_ref, wx2T_ref,
                  cosT_ref, sinT_ref, qT_ref):
    scale = LOG2E / math.sqrt(MLA_NOPE + MLA_ROPE)
    xn = _rms(h_ref[...], g_ref[...]).astype(BF16)
    cq = _rms(_dot(xn, wdq_ref[...]), gcq_ref[...]).astype(BF16)
    cqT = cq.T
    nT = _dot(wnT_ref[...], cqT) * scale
    x1T = _dot(wx1T_ref[...], cqT)
    x2T = _dot(wx2T_ref[...], cqT)
    cosT = jnp.concatenate([cosT_ref[...]] * HEADS, axis=0)
    sinT = jnp.concatenate([sinT_ref[...]] * HEADS, axis=0)
    r1 = (x1T * cosT - x2T * sinT) * scale
    r2 = (x1T * sinT + x2T * cosT) * scale
    zeros = jnp.zeros((SLOT - MLA_NOPE - MLA_ROPE, TM), BF16)
    for h in range(HEADS):
        qT_ref[h, 0:MLA_NOPE, :] = nT[h * MLA_NOPE:(h + 1) * MLA_NOPE].astype(BF16)
        qT_ref[h, MLA_NOPE:MLA_NOPE + HALF_ROPE, :] = (
            r1[h * HALF_ROPE:(h + 1) * HALF_ROPE].astype(BF16))
        qT_ref[h, MLA_NOPE + HALF_ROPE:MLA_NOPE + MLA_ROPE, :] = (
            r2[h * HALF_ROPE:(h + 1) * HALF_ROPE].astype(BF16))
        qT_ref[h, MLA_NOPE + MLA_ROPE:, :] = zeros


def _mla_q(h, g, wdq, gcq, wnT, wx1T, wx2T, cosT, sinT):
    n = SEQ // TM
    return pl.pallas_call(
        _mla_q_kernel,
        grid=(n,),
        in_specs=[
            pl.BlockSpec((TM, D_MODEL), lambda i: (i, 0)),
            _const_spec((1, D_MODEL)),
            _const_spec((D_MODEL, Q_RANK)),
            _const_spec((1, Q_RANK)),
            _const_spec((HEADS * MLA_NOPE, Q_RANK)),
            _const_spec((HEADS * HALF_ROPE, Q_RANK)),
            _const_spec((HEADS * HALF_ROPE, Q_RANK)),
            pl.BlockSpec((HALF_ROPE, TM), lambda i: (0, i)),
            pl.BlockSpec((HALF_ROPE, TM), lambda i: (0, i)),
        ],
        out_specs=pl.BlockSpec((HEADS, SLOT, TM), lambda i: (0, 0, i)),
        out_shape=jax.ShapeDtypeStruct((HEADS, SLOT, SEQ), BF16),
        compiler_params=_params(("arbitrary",)),
        name="mla_q",
    )(h, g, wdq, gcq, wnT, wx1T, wx2T, cosT, sinT)


def _pad_heads_cols(w, width):
    k = w.shape[0]
    w = w.reshape(k, HEADS, width)
    w = jnp.pad(w, ((0, 0), (0, 0), (0, SLOT - width)))
    return w.reshape(k, AUG)


def _fox_bias_placement():
    pk = np.zeros((3 * LANES, AUG), np.float32)
    for h in range(HEADS):
        for j in range(3):
            pk[HEADS, h * SLOT + HEAD_DIM + j] = 1.0
            pk[j * LANES + h, h * SLOT + HEAD_DIM + 3 + j] = -1.0
    pq = np.zeros((AUG, 3 * LANES), np.float32)
    for h in range(HEADS):
        for j in range(3):
            pq[h * SLOT + HEAD_DIM + j, j * LANES + h] = 1.0
            pq[h * SLOT + HEAD_DIM + 3 + j, HEADS] = 1.0
    return jnp.asarray(pk, BF16), jnp.asarray(pq, BF16)


def _rope_placement():
    p1 = np.zeros((LANES, AUG), np.float32)
    p2 = np.zeros((LANES, AUG), np.float32)
    for h in range(HEADS):
        for j in range(HALF_ROPE):
            p1[j, h * SLOT + MLA_NOPE + j] = 1.0
            p2[j, h * SLOT + MLA_NOPE + HALF_ROPE + j] = 1.0
    return jnp.asarray(p1, BF16), jnp.asarray(p2, BF16)


def kernel(x, positions, attn_norm, ffn_norm, w_gate, w_up, w_down, fox_w_in, fox_b_f,
           fox_w_o, kv_norm, w_kv_a, ckv_norm, w_uk, w_uv, mla_w_dq, cq_norm, mla_w_uq,
           mla_w_o, final_norm):
    assert x.shape == (1, SEQ, D_MODEL)
    h = x.reshape(SEQ, D_MODEL)
    W = HEADS * HEAD_DIM
    pk, pqT = _fox_bias_placement()
    p1, p2 = _rope_placement()
    tri = jnp.asarray(np.tril(np.ones((TM, TM), np.float32)), BF16)
    row = lambda v: v.reshape(1, -1)

    def ffn_args(l):
        return (row(ffn_norm[l]), w_gate[l].astype(BF16), w_up[l].astype(BF16),
                w_down[l].astype(BF16), row(final_norm))

    for l in range(N_A):
        w_in = fox_w_in[l]
        wqT = _pad_heads_cols(w_in[:, :W], HEAD_DIM).T.astype(BF16)
        wk = _pad_heads_cols(w_in[:, W:2 * W], HEAD_DIM).astype(BF16)
        wvT = w_in[:, 2 * W:3 * W].T.astype(BF16)
        wf = jnp.pad(w_in[:, 3 * W:], ((0, 0), (0, LANES - HEADS))).astype(BF16)
        bf = jnp.pad(fox_b_f[l], (0, LANES - HEADS)).reshape(1, LANES)
        qT, k, vT = _fox_proj(h, row(attn_norm[l]), wqT, wk, wvT, wf, bf, pk, pqT, tri)
        oT = _attention(qT.reshape(HEADS, SLOT, SEQ), k, vT)
        h = _out_ffn(oT, h, fox_w_o[l].astype(BF16), *ffn_args(l), final=False)

    invf = ROPE_THETA ** (-jnp.arange(0, HALF_ROPE, dtype=F32) * 2.0 / MLA_ROPE)
    invf = jnp.pad(invf, (0, LANES - HALF_ROPE)).reshape(1, LANES)
    pos_col = positions.reshape(SEQ, 1).astype(F32)
    cos, sin, cosT, sinT = _rope_tables(pos_col, invf)
    wa = jnp.concatenate([
        w_kv_a[:, :KV_RANK],
        jnp.pad(w_kv_a[:, KV_RANK:KV_RANK + HALF_ROPE], ((0, 0), (0, LANES - HALF_ROPE))),
        jnp.pad(w_kv_a[:, KV_RANK + HALF_ROPE:], ((0, 0), (0, LANES - HALF_ROPE))),
    ], axis=1).astype(BF16)
    wuk = _pad_heads_cols(w_uk.reshape(KV_RANK, HEADS * MLA_NOPE), MLA_NOPE).astype(BF16)
    wuvT = w_uv.reshape(KV_RANK, HEADS * HEAD_DIM).T.astype(BF16)
    k_sh, vT_sh = _mla_kv(h, row(kv_norm), wa, row(ckv_norm), cos, sin, wuk, p1, p2, wuvT)

    for l in range(N_A, DEPTH):
        j = l - N_A
        wuq = mla_w_uq[j].reshape(Q_RANK, HEADS, MLA_NOPE + MLA_ROPE)
        wnT = wuq[:, :, :MLA_NOPE].reshape(Q_RANK, -1).T.astype(BF16)
        wx1T = wuq[:, :, MLA_NOPE:MLA_NOPE + HALF_ROPE].reshape(Q_RANK, -1).T.astype(BF16)
        wx2T = wuq[:, :, MLA_NOPE + HALF_ROPE:].reshape(Q_RANK, -1).T.astype(BF16)
        qT = _mla_q(h, row(attn_norm[l]), mla_w_dq[j].astype(BF16), row(cq_norm[j]),
                    wnT, wx1T, wx2T, cosT, sinT)
        oT = _attention(qT, k_sh, vT_sh)
        h = _out_ffn(oT, h, mla_w_o[j].astype(BF16), *ffn_args(l), final=(l == DEPTH - 1))

    return h.reshape(1, SEQ, D_MODEL)
```

```python
import functools
import math

import numpy as np
import jax
import jax.numpy as jnp
from jax import lax
from jax.experimental import pallas as pl
from jax.experimental.pallas import tpu as pltpu

F32 = jnp.float32
BF16 = jnp.bfloat16

D_MODEL = 1024
SEQ = 16384
DEPTH = 4
N_A = DEPTH // 2
RMS_EPS = 1e-6
HEADS = 16
HEAD_DIM = 64
MLA_NOPE = 64
MLA_ROPE = 32
HALF_ROPE = MLA_ROPE // 2
KV_RANK = 256
Q_RANK = 768
ROPE_THETA = 10000.0
D_FF = 2816

LANES = 128
SLOT = 128
AUG = HEADS * SLOT
LOG2E = math.log2(math.e)
NEG = -1e30

TM = 512
TQ = 1024
CH = TQ // 2
ACC_ROWS = HEAD_DIM + 16
FF_CHUNK = D_FF // 2
VMEM_LIMIT = 56 * 1024 * 1024


def _dot(a, b):
    return jnp.dot(a, b, preferred_element_type=F32)


def _rms(x, g):
    inv = lax.rsqrt(jnp.mean(x * x, axis=-1, keepdims=True) + RMS_EPS)
    return (x * inv) * g


def _split3(x):
    hi = x.astype(BF16)
    r = x - hi.astype(F32)
    mid = r.astype(BF16)
    lo = (r - mid.astype(F32)).astype(BF16)
    return hi, mid, lo


def _store_heads(k_ref, k):
    for h in range(HEADS):
        k_ref[h] = k[:, h * SLOT:(h + 1) * SLOT]


def _const_spec(shape):
    nd = len(shape)
    return pl.BlockSpec(shape, lambda *_: (0,) * nd, pipeline_mode=pl.Buffered(1))


def _params(sem):
    return pltpu.CompilerParams(dimension_semantics=sem, vmem_limit_bytes=VMEM_LIMIT)


def _fox_proj_kernel(h_ref, g_ref, wqT_ref, wk_ref, wvT_ref, wf_ref, bf_ref,
                     pk_ref, pqT_ref, tri_ref, qT_ref, k_ref, vT_ref, carry_ref):
    @pl.when(pl.program_id(0) == 0)
    def _():
        carry_ref[...] = jnp.zeros_like(carry_ref)

    xn = _rms(h_ref[...], g_ref[...]).astype(BF16)
    xnT = xn.T

    f = _dot(xn, wf_ref[...]) + bf_ref[...]
    logf = (jnp.minimum(f, 0.0) - jnp.log1p(jnp.exp(-jnp.abs(f)))) * LOG2E
    hi, mid, lo = _split3(logf)
    tri = tri_ref[...]
    cum = _dot(tri, hi) + _dot(tri, mid) + _dot(tri, lo) + carry_ref[...]
    carry_ref[...] = cum[TM - 1:TM, :]

    lane = lax.broadcasted_iota(jnp.int32, cum.shape, 1)
    c = jnp.where(lane < HEADS, cum, 1.0)
    c3 = jnp.concatenate(_split3(c), axis=1)
    c3T = jnp.concatenate(_split3(c.T), axis=0)

    k = _dot(xn, wk_ref[...]) + _dot(c3, pk_ref[...])
    _store_heads(k_ref, k.astype(BF16))
    qT =_dot(wqT_ref[...], xnT) * (LOG2E / math.sqrt(HEAD_DIM)) + _dot(pqT_ref[...], c3T)
    qT_ref[...] = qT.astype(BF16)
    vT_ref[...] = _dot(wvT_ref[...], xnT).astype(BF16)


def _fox_proj(h, g, wqT, wk, wvT, wf, bf, pk, pqT, tri):
    n = SEQ // TM
    return pl.pallas_call(
        _fox_proj_kernel,
        grid=(n,),
        in_specs=[
            pl.BlockSpec((TM, D_MODEL), lambda i: (i, 0)),
            _const_spec((1, D_MODEL)),
            _const_spec((AUG, D_MODEL)),
            _const_spec((D_MODEL, AUG)),
            _const_spec((D_MODEL, D_MODEL)),
            _const_spec((D_MODEL, LANES)),
            _const_spec((1, LANES)),
            _const_spec((3 * LANES, AUG)),
            _const_spec((AUG, 3 * LANES)),
            _const_spec((TM, TM)),
        ],
        out_specs=[
            pl.BlockSpec((AUG, TM), lambda i: (0, i)),
            pl.BlockSpec((HEADS, TM, SLOT), lambda i: (0, i, 0)),
            pl.BlockSpec((D_MODEL, TM), lambda i: (0, i)),
        ],
        out_shape=[
            jax.ShapeDtypeStruct((AUG, SEQ), BF16),
            jax.ShapeDtypeStruct((HEADS, SEQ, SLOT), BF16),
            jax.ShapeDtypeStruct((D_MODEL, SEQ), BF16),
        ],
        scratch_shapes=[pltpu.VMEM((1, LANES), F32)],
        compiler_params=_params(("arbitrary",)),
        name="fox_proj",
    )(h, g, wqT, wk, wvT, wf, bf, pk, pqT, tri)


def _attn_kernel(qT_ref, k_ref, vT_ref, o_ref, sa_scr, sb_scr, pa_scr, pb_scr,
                 smax_a_scr, smax_b_scr, alpha_b_scr, m_scr, acc_scr):
    _attn_tile(jnp.int32(0), True, qT_ref, k_ref, vT_ref, o_ref, sa_scr, sb_scr, pa_scr, pb_scr,
               smax_a_scr, smax_b_scr, alpha_b_scr, m_scr, acc_scr)

    def tile(i, carry):
        _attn_tile(i, False, qT_ref, k_ref, vT_ref, o_ref, sa_scr, sb_scr, pa_scr, pb_scr,
                   smax_a_scr, smax_b_scr, alpha_b_scr, m_scr, acc_scr)
        return carry

    lax.fori_loop(1, SEQ // TQ, tile, 0)


def _attn_tile(i, is_first, qT_ref, k_ref, vT_ref, o_ref, sa_scr, sb_scr, pa_scr, pb_scr,
               smax_a_scr, smax_b_scr, alpha_b_scr, m_scr, acc_scr):
    groups = CH // 8
    ALL = slice(0, TQ)
    HI = slice(CH, TQ)
    q0 = pl.multiple_of(i * TQ, TQ)

    def scores(c, s_ref, smax_ref, cols=ALL, diag=False):
        n = cols.stop - cols.start
        off = pl.multiple_of(c * CH, CH)
        q_cols = pl.ds(pl.multiple_of(q0 + cols.start, CH), n)
        s = _dot(k_ref[pl.ds(off, CH), :], qT_ref[:, q_cols])
        if diag:
            row = lax.broadcasted_iota(jnp.int32, (CH, n), 0)
            col = lax.broadcasted_iota(jnp.int32, (CH, n), 1)
            s = jnp.where(row <= col, s, NEG)
        s_ref[:, cols] = s
        smax_ref[:, cols] = jnp.max(s.reshape(groups, 8, n), axis=0)

    def probs(s_ref, p_ref, smax_ref, cols=ALL):
        m_old = m_scr[:, cols]
        m_new = jnp.maximum(m_old, jnp.max(smax_ref[:, cols], axis=0, keepdims=True))
        alpha = jnp.exp2(m_old - m_new)
        m_scr[:, cols] = m_new
        p_ref[:, cols] = jnp.exp2((s_ref[:, cols] - m_new).astype(BF16))
        return alpha

    ones_rows = jnp.ones((ACC_ROWS - HEAD_DIM, CH), BF16)

    def pv(c, p_ref, alpha, cols=ALL):
        off = pl.multiple_of(c * CH, CH)
        lhs = jnp.concatenate([vT_ref[:, pl.ds(off, CH)], ones_rows], axis=0)
        acc_scr[:, cols] = alpha * acc_scr[:, cols] + _dot(lhs, p_ref[:, cols])

    def score_pair(t, diag):
        scores(2 * t, sa_scr, smax_a_scr, ALL, diag)
        scores(2 * t + 1, sb_scr, smax_b_scr, HI if diag else ALL, diag)

    def pair(t, next_diag, first=False, diag=False):
        cols_b = HI if diag else ALL
        if not first:
            pv(jnp.maximum(2 * t - 1, 0), pb_scr, alpha_b_scr[...])
        alpha_a = probs(sa_scr, pa_scr, smax_a_scr)
        if next_diag is not None:
            scores(2 * t + 2, sa_scr, smax_a_scr, ALL, next_diag)
        pv(2 * t, pa_scr, alpha_a)
        alpha_b_scr[:, cols_b] = probs(sb_scr, pb_scr, smax_b_scr, cols_b)
        if next_diag is not None:
            scores(2 * t + 3, sb_scr, smax_b_scr, HI if next_diag else ALL, next_diag)

    def init():
        m_scr[...] = jnp.full(m_scr.shape, NEG, F32)
        acc_scr[...] = jnp.zeros(acc_scr.shape, F32)

    def finish():
        pv(2 * i + 1, pb_scr, alpha_b_scr[:, HI], HI)
        l = acc_scr[HEAD_DIM:HEAD_DIM + 1, :]
        o_ref[:, pl.ds(q0, TQ)] = (acc_scr[:HEAD_DIM, :] / l).astype(BF16)

    if is_first:
        init()
        score_pair(0, True)
        pair(0, None, first=True, diag=True)
        finish()
    else:
        init()
        pb_scr[...] = jnp.zeros(pb_scr.shape, BF16)
        alpha_b_scr[...] = jnp.ones(alpha_b_scr.shape, F32)
        score_pair(0, False)

        def body(t, carry):
            pair(t, False)
            return carry

        lax.fori_loop(0, i - 1, body, 0)
        pair(i - 1, True)
        pair(i, None, diag=True)
        finish()


def _attention(qT, k, vT):
    return pl.pallas_call(
        _attn_kernel,
        grid=(HEADS,),
        in_specs=[
            pl.BlockSpec((None, SLOT, SEQ), lambda h: (h, 0, 0)),
            pl.BlockSpec((None, SEQ, SLOT), lambda h: (h, 0, 0)),
            pl.BlockSpec((HEAD_DIM, SEQ), lambda h: (h, 0)),
        ],
        out_specs=pl.BlockSpec((HEAD_DIM, SEQ), lambda h: (h, 0)),
        out_shape=jax.ShapeDtypeStruct((HEADS * HEAD_DIM, SEQ), BF16),
        scratch_shapes=[
            pltpu.VMEM((CH, TQ), F32),
            pltpu.VMEM((CH, TQ), F32),
            pltpu.VMEM((CH, TQ), BF16),
            pltpu.VMEM((CH, TQ), BF16),
            pltpu.VMEM((8, TQ), F32),
            pltpu.VMEM((8, TQ), F32),
            pltpu.VMEM((1, TQ), F32),
            pltpu.VMEM((1, TQ), F32),
            pltpu.VMEM((ACC_ROWS, TQ), F32),
        ],
        compiler_params=_params(("arbitrary",)),
        name="attention",
    )(qT, k, vT)


def _out_ffn_kernel(oT_ref, h_ref, wo_ref, g_ref, wg_ref, wu_ref, wd_ref, gfin_ref,
                    out_ref, *, final):
    o = oT_ref[...].T
    h1 = h_ref[...] + _dot(o, wo_ref[...])
    xn = _rms(h1, g_ref[...]).astype(BF16)
    acc = h1
    for c in range(D_FF // FF_CHUNK):
        sl = slice(c * FF_CHUNK, (c + 1) * FF_CHUNK)
        gate = _dot(xn, wg_ref[:, sl])
        up = _dot(xn, wu_ref[:, sl])
        act = (gate * (1.0 / (1.0 + jnp.exp(-gate))) * up).astype(BF16)
        acc = acc + _dot(act, wd_ref[sl, :])
    if final:
        acc = _rms(acc, gfin_ref[...])
    out_ref[...] = acc


def _out_ffn(oT, h, wo, g, wg, wu, wd, gfin, final):
    n = SEQ // TM
    return pl.pallas_call(
        functools.partial(_out_ffn_kernel, final=final),
        grid=(n,),
        in_specs=[
            pl.BlockSpec((D_MODEL, TM), lambda i: (0, i)),
            pl.BlockSpec((TM, D_MODEL), lambda i: (i, 0)),
            _const_spec((D_MODEL, D_MODEL)),
            _const_spec((1, D_MODEL)),
            _const_spec((D_MODEL, D_FF)),
            _const_spec((D_MODEL, D_FF)),
            _const_spec((D_FF, D_MODEL)),
            _const_spec((1, D_MODEL)),
        ],
        out_specs=pl.BlockSpec((TM, D_MODEL), lambda i: (i, 0)),
        out_shape=jax.ShapeDtypeStruct((SEQ, D_MODEL), F32),
        compiler_params=_params(("arbitrary",)),
        name="out_ffn",
    )(oT, h, wo, g, wg, wu, wd, gfin)


def _rope_table_kernel(pos_ref, invf_ref, cos_ref, sin_ref, cosT_ref, sinT_ref):
    ang = pos_ref[...] * invf_ref[...]
    cos = jnp.cos(ang)
    sin = jnp.sin(ang)
    cos_ref[...] = cos
    sin_ref[...] = sin
    cosT_ref[...] = cos.T
    sinT_ref[...] = sin.T


def _rope_tables(pos_col, invf):
    n = SEQ // TM
    row = pl.BlockSpec((TM, LANES), lambda i: (i, 0))
    colT = pl.BlockSpec((LANES, TM), lambda i: (0, i))
    return pl.pallas_call(
        _rope_table_kernel,
        grid=(n,),
        in_specs=[pl.BlockSpec((TM, 1), lambda i: (i, 0)), _const_spec((1, LANES))],
        out_specs=[row, row, colT, colT],
        out_shape=[jax.ShapeDtypeStruct((SEQ, LANES), F32)] * 2
        + [jax.ShapeDtypeStruct((LANES, SEQ), F32)] * 2,
        compiler_params=_params(("arbitrary",)),
        name="rope_tables",
    )(pos_col, invf)


def _mla_kv_kernel(h_ref, g_ref, wa_ref, gckv_ref, cos_ref, sin_ref, wuk_ref,
                   p1_ref, p2_ref, wuvT_ref, k_ref, vT_ref):
    hn = _rms(h_ref[...], g_ref[...]).astype(BF16)
    a = _dot(hn, wa_ref[...])
    c = _rms(a[:, :KV_RANK], gckv_ref[...]).astype(BF16)
    x1 = a[:, KV_RANK:KV_RANK + LANES]
    x2 = a[:, KV_RANK + LANES:]
    cos = cos_ref[...]
    sin = sin_ref[...]
    r1 = (x1 * cos - x2 * sin).astype(BF16)
    r2 = (x1 * sin + x2 * cos).astype(BF16)
    k = _dot(c, wuk_ref[...]) + _dot(r1, p1_ref[...]) + _dot(r2, p2_ref[...])
    _store_heads(k_ref, k.astype(BF16))
    vT_ref[...] = _dot(wuvT_ref[...], c.T).astype(BF16)


def _mla_kv(h, g, wa, gckv, cos, sin, wuk, p1, p2, wuvT):
    n = SEQ // TM
    return pl.pallas_call(
        _mla_kv_kernel,
        grid=(n,),
        in_specs=[
            pl.BlockSpec((TM, D_MODEL), lambda i: (i, 0)),
            _const_spec((1, D_MODEL)),
            _const_spec((D_MODEL, 4 * LANES)),
            _const_spec((1, KV_RANK)),
            pl.BlockSpec((TM, LANES), lambda i: (i, 0)),
            pl.BlockSpec((TM, LANES), lambda i: (i, 0)),
            _const_spec((KV_RANK, AUG)),
            _const_spec((LANES, AUG)),
            _const_spec((LANES, AUG)),
            _const_spec((D_MODEL, KV_RANK)),
        ],
        out_specs=[
            pl.BlockSpec((HEADS, TM, SLOT), lambda i: (0, i, 0)),
            pl.BlockSpec((D_MODEL, TM), lambda i: (0, i)),
        ],
        out_shape=[
            jax.ShapeDtypeStruct((HEADS, SEQ, SLOT), BF16),
            jax.ShapeDtypeStruct((D_MODEL, SEQ), BF16),
        ],
        compiler_params=_params(("arbitrary",)),
        name="mla_kv",
    )(h, g, wa, gckv, cos, sin, wuk, p1, p2, wuvT)


def _mla_q_kernel(h_ref, g_ref, wdq_ref, gcq_ref, wnT_ref, wx1T_ref, wx2T_ref,
                  cosT_ref, sinT_ref, qT_ref):
    scale = LOG2E / math.sqrt(MLA_NOPE + MLA_ROPE)
    xn = _rms(h_ref[...], g_ref[...]).astype(BF16)
    cq = _rms(_dot(xn, wdq_ref[...]), gcq_ref[...]).astype(BF16)
    cqT = cq.T
    nT = _dot(wnT_ref[...], cqT) * scale
    x1T = _dot(wx1T_ref[...], cqT)
    x2T = _dot(wx2T_ref[...], cqT)
    cosT = jnp.concatenate([cosT_ref[...]] * HEADS, axis=0)
    sinT = jnp.concatenate([sinT_ref[...]] * HEADS, axis=0)
    r1 = (x1T * cosT - x2T * sinT) * scale
    r2 = (x1T * sinT + x2T * cosT) * scale
    zeros = jnp.zeros((SLOT - MLA_NOPE - MLA_ROPE, TM), BF16)
    for h in range(HEADS):
        qT_ref[h, 0:MLA_NOPE, :] = nT[h * MLA_NOPE:(h + 1) * MLA_NOPE].astype(BF16)
        qT_ref[h, MLA_NOPE:MLA_NOPE + HALF_ROPE, :] = (
            r1[h * HALF_ROPE:(h + 1) * HALF_ROPE].astype(BF16))
        qT_ref[h, MLA_NOPE + HALF_ROPE:MLA_NOPE + MLA_ROPE, :] = (
            r2[h * HALF_ROPE:(h + 1) * HALF_ROPE].astype(BF16))
        qT_ref[h, MLA_NOPE + MLA_ROPE:, :] = zeros


def _mla_q(h, g, wdq, gcq, wnT, wx1T, wx2T, cosT, sinT):
    n = SEQ // TM
    return pl.pallas_call(
        _mla_q_kernel,
        grid=(n,),
        in_specs=[
            pl.BlockSpec((TM, D_MODEL), lambda i: (i, 0)),
            _const_spec((1, D_MODEL)),
            _const_spec((D_MODEL, Q_RANK)),
            _const_spec((1, Q_RANK)),
            _const_spec((HEADS * MLA_NOPE, Q_RANK)),
            _const_spec((HEADS * HALF_ROPE, Q_RANK)),
            _const_spec((HEADS * HALF_ROPE, Q_RANK)),
            pl.BlockSpec((HALF_ROPE, TM), lambda i: (0, i)),
            pl.BlockSpec((HALF_ROPE, TM), lambda i: (0, i)),
        ],
        out_specs=pl.BlockSpec((HEADS, SLOT, TM), lambda i: (0, 0, i)),
        out_shape=jax.ShapeDtypeStruct((HEADS, SLOT, SEQ), BF16),
        compiler_params=_params(("arbitrary",)),
        name="mla_q",
    )(h, g, wdq, gcq, wnT, wx1T, wx2T, cosT, sinT)


def _pad_heads_cols(w, width):
    k = w.shape[0]
    w = w.reshape(k, HEADS, width)
    w = jnp.pad(w, ((0, 0), (0, 0), (0, SLOT - width)))
    return w.reshape(k, AUG)


def _fox_bias_placement():
    pk = np.zeros((3 * LANES, AUG), np.float32)
    for h in range(HEADS):
        for j in range(3):
            pk[HEADS, h * SLOT + HEAD_DIM + j] = 1.0
            pk[j * LANES + h, h * SLOT + HEAD_DIM + 3 + j] = -1.0
    pq = np.zeros((AUG, 3 * LANES), np.float32)
    for h in range(HEADS):
        for j in range(3):
            pq[h * SLOT + HEAD_DIM + j, j * LANES + h] = 1.0
            pq[h * SLOT + HEAD_DIM + 3 + j, HEADS] = 1.0
    return jnp.asarray(pk, BF16), jnp.asarray(pq, BF16)


def _rope_placement():
    p1 = np.zeros((LANES, AUG), np.float32)
    p2 = np.zeros((LANES, AUG), np.float32)
    for h in range(HEADS):
        for j in range(HALF_ROPE):
            p1[j, h * SLOT + MLA_NOPE + j] = 1.0
            p2[j, h * SLOT + MLA_NOPE + HALF_ROPE + j] = 1.0
    return jnp.asarray(p1, BF16), jnp.asarray(p2, BF16)


def kernel(x, positions, attn_norm, ffn_norm, w_gate, w_up, w_down, fox_w_in, fox_b_f,
           fox_w_o, kv_norm, w_kv_a, ckv_norm, w_uk, w_uv, mla_w_dq, cq_norm, mla_w_uq,
           mla_w_o, final_norm):
    assert x.shape == (1, SEQ, D_MODEL)
    h = x.reshape(SEQ, D_MODEL)
    W = HEADS * HEAD_DIM
    pk, pqT = _fox_bias_placement()
    p1, p2 = _rope_placement()
    tri = jnp.asarray(np.tril(np.ones((TM, TM), np.float32)), BF16)
    row = lambda v: v.reshape(1, -1)

    def ffn_args(l):
        return (row(ffn_norm[l]), w_gate[l].astype(BF16), w_up[l].astype(BF16),
                w_down[l].astype(BF16), row(final_norm))

    for l in range(N_A):
        w_in = fox_w_in[l]
        wqT = _pad_heads_cols(w_in[:, :W], HEAD_DIM).T.astype(BF16)
        wk = _pad_heads_cols(w_in[:, W:2 * W], HEAD_DIM).astype(BF16)
        wvT = w_in[:, 2 * W:3 * W].T.astype(BF16)
        wf = jnp.pad(w_in[:, 3 * W:], ((0, 0), (0, LANES - HEADS))).astype(BF16)
        bf = jnp.pad(fox_b_f[l], (0, LANES - HEADS)).reshape(1, LANES)
        qT, k, vT = _fox_proj(h, row(attn_norm[l]), wqT, wk, wvT, wf, bf, pk, pqT, tri)
        oT = _attention(qT.reshape(HEADS, SLOT, SEQ), k, vT)
        h = _out_ffn(oT, h, fox_w_o[l].astype(BF16), *ffn_args(l), final=False)

    invf = ROPE_THETA ** (-jnp.arange(0, HALF_ROPE, dtype=F32) * 2.0 / MLA_ROPE)
    invf = jnp.pad(invf, (0, LANES - HALF_ROPE)).reshape(1, LANES)
    pos_col = positions.reshape(SEQ, 1).astype(F32)
    cos, sin, cosT, sinT = _rope_tables(pos_col, invf)
    wa = jnp.concatenate([
        w_kv_a[:, :KV_RANK],
        jnp.pad(w_kv_a[:, KV_RANK:KV_RANK + HALF_ROPE], ((0, 0), (0, LANES - HALF_ROPE))),
        jnp.pad(w_kv_a[:, KV_RANK + HALF_ROPE:], ((0, 0), (0, LANES - HALF_ROPE))),
    ], axis=1).astype(BF16)
    wuk = _pad_heads_cols(w_uk.reshape(KV_RANK, HEADS * MLA_NOPE), MLA_NOPE).astype(BF16)
    wuvT = w_uv.reshape(KV_RANK, HEADS * HEAD_DIM).T.astype(BF16)
    k_sh, vT_sh = _mla_kv(h, row(kv_norm), wa, row(ckv_norm), cos, sin, wuk, p1, p2, wuvT)

    for l in range(N_A, DEPTH):
        j = l - N_A
        wuq = mla_w_uq[j].reshape(Q_RANK, HEADS, MLA_NOPE + MLA_ROPE)
        wnT = wuq[:, :, :MLA_NOPE].reshape(Q_RANK, -1).T.astype(BF16)
        wx1T = wuq[:, :, MLA_NOPE:MLA_NOPE + HALF_ROPE].reshape(Q_RANK, -1).T.astype(BF16)
        wx2T = wuq[:, :, MLA_NOPE + HALF_ROPE:].reshape(Q_RANK, -1).T.astype(BF16)
        qT = _mla_q(h, row(attn_norm[l]), mla_w_dq[j].astype(BF16), row(cq_norm[j]),
                    wnT, wx1T, wx2T, cosT, sinT)
        oT = _attention(qT, k_sh, vT_sh)
        h = _out_ffn(oT, h, mla_w_o[j].astype(BF16), *ffn_args(l), final=(l == DEPTH - 1))

    return h.reshape(1, SEQ, D_MODEL)
```

```python
import functools
import math

import numpy as np
import jax
import jax.numpy as jnp
from jax import lax
from jax.experimental import pallas as pl
from jax.experimental.pallas import tpu as pltpu

F32 = jnp.float32
BF16 = jnp.bfloat16

D_MODEL = 1024
SEQ = 16384
DEPTH = 4
N_A = DEPTH // 2
RMS_EPS = 1e-6
HEADS = 16
HEAD_DIM = 64
MLA_NOPE = 64
MLA_ROPE = 32
HALF_ROPE = MLA_ROPE // 2
KV_RANK = 256
Q_RANK = 768
ROPE_THETA = 10000.0
D_FF = 2816

LANES = 128
SLOT = 128
AUG = HEADS * SLOT
LOG2E = math.log2(math.e)
NEG = -1e30

TM = 512
TQ = 1024
CH = TQ // 2
ACC_ROWS = HEAD_DIM + 16
FF_CHUNK = D_FF // 2
VMEM_LIMIT = 56 * 1024 * 1024


def _dot(a, b):
    return jnp.dot(a, b, preferred_element_type=F32)


def _rms(x, g):
    inv = lax.rsqrt(jnp.mean(x * x, axis=-1, keepdims=True) + RMS_EPS)
    return (x * inv) * g


def _split3(x):
    hi = x.astype(BF16)
    r = x - hi.astype(F32)
    mid = r.astype(BF16)
    lo = (r - mid.astype(F32)).astype(BF16)
    return hi, mid, lo


def _store_heads(k_ref, k):
    for h in range(HEADS):
        k_ref[h] = k[:, h * SLOT:(h + 1) * SLOT]


def _const_spec(shape):
    nd = len(shape)
    return pl.BlockSpec(shape, lambda *_: (0,) * nd, pipeline_mode=pl.Buffered(1))


def _params(sem):
    return pltpu.CompilerParams(dimension_semantics=sem, vmem_limit_bytes=VMEM_LIMIT)


def _fox_proj_kernel(h_ref, g_ref, wqT_ref, wk_ref, wvT_ref, wf_ref, bf_ref,
                     pk_ref, pqT_ref, tri_ref, qT_ref, k_ref, vT_ref, carry_ref):
    @pl.when(pl.program_id(0) == 0)
    def _():
        carry_ref[...] = jnp.zeros_like(carry_ref)

    xn = _rms(h_ref[...], g_ref[...]).astype(BF16)
    xnT = xn.T

    f = _dot(xn, wf_ref[...]) + bf_ref[...]
    logf = (jnp.minimum(f, 0.0) - jnp.log1p(jnp.exp(-jnp.abs(f)))) * LOG2E
    hi, mid, lo = _split3(logf)
    tri = tri_ref[...]
    cum = _dot(tri, hi) + _dot(tri, mid) + _dot(tri, lo) + carry_ref[...]
    carry_ref[...] = cum[TM - 1:TM, :]

    lane = lax.broadcasted_iota(jnp.int32, cum.shape, 1)
    c = jnp.where(lane < HEADS, cum, 1.0)
    c3 = jnp.concatenate(_split3(c), axis=1)
    c3T = jnp.concatenate(_split3(c.T), axis=0)

    k = _dot(xn, wk_ref[...]) + _dot(c3, pk_ref[...])
    _store_heads(k_ref, k.astype(BF16))
    qT =_dot(wqT_ref[...], xnT) * (LOG2E / math.sqrt(HEAD_DIM)) + _dot(pqT_ref[...], c3T)
    qT_ref[...] = qT.astype(BF16)
    vT_ref[...] = _dot(wvT_ref[...], xnT).astype(BF16)


def _fox_proj(h, g, wqT, wk, wvT, wf, bf, pk, pqT, tri):
    n = SEQ // TM
    return pl.pallas_call(
        _fox_proj_kernel,
        grid=(n,),
        in_specs=[
            pl.BlockSpec((TM, D_MODEL), lambda i: (i, 0)),
            _const_spec((1, D_MODEL)),
            _const_spec((AUG, D_MODEL)),
            _const_spec((D_MODEL, AUG)),
            _const_spec((D_MODEL, D_MODEL)),
            _const_spec((D_MODEL, LANES)),
            _const_spec((1, LANES)),
            _const_spec((3 * LANES, AUG)),
            _const_spec((AUG, 3 * LANES)),
            _const_spec((TM, TM)),
        ],
        out_specs=[
            pl.BlockSpec((AUG, TM), lambda i: (0, i)),
            pl.BlockSpec((HEADS, TM, SLOT), lambda i: (0, i, 0)),
            pl.BlockSpec((D_MODEL, TM), lambda i: (0, i)),
        ],
        out_shape=[
            jax.ShapeDtypeStruct((AUG, SEQ), BF16),
            jax.ShapeDtypeStruct((HEADS, SEQ, SLOT), BF16),
            jax.ShapeDtypeStruct((D_MODEL, SEQ), BF16),
        ],
        scratch_shapes=[pltpu.VMEM((1, LANES), F32)],
        compiler_params=_params(("arbitrary",)),
        name="fox_proj",
    )(h, g, wqT, wk, wvT, wf, bf, pk, pqT, tri)


def _attn_kernel(qT_ref, k_ref, vT_ref, o_ref, sa_scr, sb_scr, pa_scr, pb_scr,
                 smax_a_scr, smax_b_scr, alpha_b_scr, m_scr, acc_scr):
    _attn_tile(jnp.int32(0), True, qT_ref, k_ref, vT_ref, o_ref, sa_scr, sb_scr, pa_scr, pb_scr,
               smax_a_scr, smax_b_scr, alpha_b_scr, m_scr, acc_scr)

    def tile(i, carry):
        _attn_tile(i, False, qT_ref, k_ref, vT_ref, o_ref, sa_scr, sb_scr, pa_scr, pb_scr,
                   smax_a_scr, smax_b_scr, alpha_b_scr, m_scr, acc_scr)
        return carry

    lax.fori_loop(1, SEQ // TQ, tile, 0)


def _attn_tile(i, is_first, qT_ref, k_ref, vT_ref, o_ref, sa_scr, sb_scr, pa_scr, pb_scr,
               smax_a_scr, smax_b_scr, alpha_b_scr, m_scr, acc_scr):
    groups = CH // 8
    ALL = slice(0, TQ)
    HI = slice(CH, TQ)
    q0 = pl.multiple_of(i * TQ, TQ)

    def scores(c, s_ref, smax_ref, cols=ALL, diag=False, qbase=q0):
        n = cols.stop - cols.start
        off = pl.multiple_of(c * CH, CH)
        q_cols = pl.ds(pl.multiple_of(qbase + cols.start, CH), n)
        s = _dot(k_ref[pl.ds(off, CH), :], qT_ref[:, q_cols])
        if diag:
            row = lax.broadcasted_iota(jnp.int32, (CH, n), 0)
            col = lax.broadcasted_iota(jnp.int32, (CH, n), 1)
            s = jnp.where(row <= col, s, NEG)
        s_ref[:, cols] = s
        smax_ref[:, cols] = jnp.max(s.reshape(groups, 8, n), axis=0)

    def probs(s_ref, p_ref, smax_ref, cols=ALL):
        m_old = m_scr[:, cols]
        m_new = jnp.maximum(m_old, jnp.max(smax_ref[:, cols], axis=0, keepdims=True))
        alpha = jnp.exp2(m_old - m_new)
        m_scr[:, cols] = m_new
        p_ref[:, cols] = jnp.exp2(s_ref[:, cols] - m_new).astype(BF16)
        return alpha

    ones_rows = jnp.ones((ACC_ROWS - HEAD_DIM, CH), BF16)

    def pv(c, p_ref, alpha, cols=ALL):
        off = pl.multiple_of(c * CH, CH)
        lhs = jnp.concatenate([vT_ref[:, pl.ds(off, CH)], ones_rows], axis=0)
        acc_scr[:, cols] = alpha * acc_scr[:, cols] + _dot(lhs, p_ref[:, cols])

    def score_pair(t, diag):
        scores(2 * t, sa_scr, smax_a_scr, ALL, diag)
        scores(2 * t + 1, sb_scr, smax_b_scr, HI if diag else ALL, diag)

    def pair(t, next_diag, first=False, diag=False):
        cols_b = HI if diag else ALL
        if not first:
            pv(jnp.maximum(2 * t - 1, 0), pb_scr, alpha_b_scr[...])
        alpha_a = probs(sa_scr, pa_scr, smax_a_scr)
        if next_diag is not None:
            scores(2 * t + 2, sa_scr, smax_a_scr, ALL, next_diag)
        pv(2 * t, pa_scr, alpha_a)
        alpha_b_scr[:, cols_b] = probs(sb_scr, pb_scr, smax_b_scr, cols_b)
        if next_diag is not None:
            scores(2 * t + 3, sb_scr, smax_b_scr, HI if next_diag else ALL, next_diag)

    def plain_pair(t):
        halves = (slice(0, CH), HI)
        alpha_a = [None, None]
        for j, cols in enumerate(halves):
            alpha_a[j] = probs(sa_scr, pa_scr, smax_a_scr, cols)
            scores(2 * t + 2, sa_scr, smax_a_scr, cols)
            pv(jnp.maximum(2 * t - 1, 0), pb_scr, alpha_b_scr[:, cols], cols)
        for j, cols in enumerate(halves):
            alpha_b_scr[:, cols] = probs(sb_scr, pb_scr, smax_b_scr, cols)
            scores(2 * t + 3, sb_scr, smax_b_scr, cols)
            pv(2 * t, pa_scr, alpha_a[j], cols)

    def init():
        m_scr[...] = jnp.full(m_scr.shape, NEG, F32)
        acc_scr[...] = jnp.zeros(acc_scr.shape, F32)

    def finish():
        pv(2 * i + 1, pb_scr, alpha_b_scr[:, HI], HI)
        l = acc_scr[HEAD_DIM:HEAD_DIM + 1, :]
        o_ref[:, pl.ds(q0, TQ)] = (acc_scr[:HEAD_DIM, :] / l).astype(BF16)

    def next_tile_scores():
        q_next = pl.multiple_of(jnp.minimum(i + 1, SEQ // TQ - 1) * TQ, TQ)
        scores(0, sa_scr, smax_a_scr, ALL, False, q_next)
        scores(1, sb_scr, smax_b_scr, ALL, False, q_next)

    if is_first:
        init()
        score_pair(0, True)
        pair(0, None, first=True, diag=True)
        next_tile_scores()
        finish()
    else:
        init()
        pb_scr[...] = jnp.zeros(pb_scr.shape, BF16)
        alpha_b_scr[...] = jnp.ones(alpha_b_scr.shape, F32)

        def body(t, carry):
            plain_pair(t)
            return carry

        lax.fori_loop(0, i - 1, body, 0)
        pair(i - 1, True)
        pair(i, None, diag=True)
        next_tile_scores()
        finish()


def _attention(qT, k, vT):
    return pl.pallas_call(
        _attn_kernel,
        grid=(HEADS,),
        in_specs=[
            pl.BlockSpec((None, SLOT, SEQ), lambda h: (h, 0, 0)),
            pl.BlockSpec((None, SEQ, SLOT), lambda h: (h, 0, 0)),
            pl.BlockSpec((HEAD_DIM, SEQ), lambda h: (h, 0)),
        ],
        out_specs=pl.BlockSpec((HEAD_DIM, SEQ), lambda h: (h, 0)),
        out_shape=jax.ShapeDtypeStruct((HEADS * HEAD_DIM, SEQ), BF16),
        scratch_shapes=[
            pltpu.VMEM((CH, TQ), F32),
            pltpu.VMEM((CH, TQ), F32),
            pltpu.VMEM((CH, TQ), BF16),
            pltpu.VMEM((CH, TQ), BF16),
            pltpu.VMEM((8, TQ), F32),
            pltpu.VMEM((8, TQ), F32),
            pltpu.VMEM((1, TQ), F32),
            pltpu.VMEM((1, TQ), F32),
            pltpu.VMEM((ACC_ROWS, TQ), F32),
        ],
        compiler_params=_params(("arbitrary",)),
        name="attention",
    )(qT, k, vT)


def _out_ffn_kernel(oT_ref, h_ref, wo_ref, g_ref, wg_ref, wu_ref, wd_ref, gfin_ref,
                    out_ref, *, final):
    o = oT_ref[...].T
    h1 = h_ref[...] + _dot(o, wo_ref[...])
    xn = _rms(h1, g_ref[...]).astype(BF16)
    acc = h1
    for c in range(D_FF // FF_CHUNK):
        sl = slice(c * FF_CHUNK, (c + 1) * FF_CHUNK)
        gate = _dot(xn, wg_ref[:, sl])
        up = _dot(xn, wu_ref[:, sl])
        act = (gate * (1.0 / (1.0 + jnp.exp(-gate))) * up).astype(BF16)
        acc = acc + _dot(act, wd_ref[sl, :])
    if final:
        acc = _rms(acc, gfin_ref[...])
    out_ref[...] = acc


def _out_ffn(oT, h, wo, g, wg, wu, wd, gfin, final):
    n = SEQ // TM
    return pl.pallas_call(
        functools.partial(_out_ffn_kernel, final=final),
        grid=(n,),
        in_specs=[
            pl.BlockSpec((D_MODEL, TM), lambda i: (0, i)),
            pl.BlockSpec((TM, D_MODEL), lambda i: (i, 0)),
            _const_spec((D_MODEL, D_MODEL)),
            _const_spec((1, D_MODEL)),
            _const_spec((D_MODEL, D_FF)),
            _const_spec((D_MODEL, D_FF)),
            _const_spec((D_FF, D_MODEL)),
            _const_spec((1, D_MODEL)),
        ],
        out_specs=pl.BlockSpec((TM, D_MODEL), lambda i: (i, 0)),
        out_shape=jax.ShapeDtypeStruct((SEQ, D_MODEL), F32),
        compiler_params=_params(("arbitrary",)),
        name="out_ffn",
    )(oT, h, wo, g, wg, wu, wd, gfin)


def _rope_table_kernel(pos_ref, invf_ref, cos_ref, sin_ref, cosT_ref, sinT_ref):
    ang = pos_ref[...] * invf_ref[...]
    cos = jnp.cos(ang)
    sin = jnp.sin(ang)
    cos_ref[...] = cos
    sin_ref[...] = sin
    cosT_ref[...] = cos.T
    sinT_ref[...] = sin.T


def _rope_tables(pos_col, invf):
    n = SEQ // TM
    row = pl.BlockSpec((TM, LANES), lambda i: (i, 0))
    colT = pl.BlockSpec((LANES, TM), lambda i: (0, i))
    return pl.pallas_call(
        _rope_table_kernel,
        grid=(n,),
        in_specs=[pl.BlockSpec((TM, 1), lambda i: (i, 0)), _const_spec((1, LANES))],
        out_specs=[row, row, colT, colT],
        out_shape=[jax.ShapeDtypeStruct((SEQ, LANES), F32)] * 2
        + [jax.ShapeDtypeStruct((LANES, SEQ), F32)] * 2,
        compiler_params=_params(("arbitrary",)),
        name="rope_tables",
    )(pos_col, invf)


def _mla_kv_kernel(h_ref, g_ref, wa_ref, gckv_ref, cos_ref, sin_ref, wuk_ref,
                   p1_ref, p2_ref, wuvT_ref, k_ref, vT_ref):
    hn = _rms(h_ref[...], g_ref[...]).astype(BF16)
    a = _dot(hn, wa_ref[...])
    c = _rms(a[:, :KV_RANK], gckv_ref[...]).astype(BF16)
    x1 = a[:, KV_RANK:KV_RANK + LANES]
    x2 = a[:, KV_RANK + LANES:]
    cos = cos_ref[...]
    sin = sin_ref[...]
    r1 = (x1 * cos - x2 * sin).astype(BF16)
    r2 = (x1 * sin + x2 * cos).astype(BF16)
    k = _dot(c, wuk_ref[...]) + _dot(r1, p1_ref[...]) + _dot(r2, p2_ref[...])
    _store_heads(k_ref, k.astype(BF16))
    vT_ref[...] = _dot(wuvT_ref[...], c.T).astype(BF16)


def _mla_kv(h, g, wa, gckv, cos, sin, wuk, p1, p2, wuvT):
    n = SEQ // TM
    return pl.pallas_call(
        _mla_kv_kernel,
        grid=(n,),
        in_specs=[
            pl.BlockSpec((TM, D_MODEL), lambda i: (i, 0)),
            _const_spec((1, D_MODEL)),
            _const_spec((D_MODEL, 4 * LANES)),
            _const_spec((1, KV_RANK)),
            pl.BlockSpec((TM, LANES), lambda i: (i, 0)),
            pl.BlockSpec((TM, LANES), lambda i: (i, 0)),
            _const_spec((KV_RANK, AUG)),
            _const_spec((LANES, AUG)),
            _const_spec((LANES, AUG)),
            _const_spec((D_MODEL, KV_RANK)),
        ],
        out_specs=[
            pl.BlockSpec((HEADS, TM, SLOT), lambda i: (0, i, 0)),
            pl.BlockSpec((D_MODEL, TM), lambda i: (0, i)),
        ],
        out_shape=[
            jax.ShapeDtypeStruct((HEADS, SEQ, SLOT), BF16),
            jax.ShapeDtypeStruct((D_MODEL, SEQ), BF16),
        ],
        compiler_params=_params(("arbitrary",)),
        name="mla_kv",
    )(h, g, wa, gckv, cos, sin, wuk, p1, p2, wuvT)


def _mla_q_kernel(h_ref, g_ref, wdq_ref, gcq_ref, wnT_ref, wx1T_ref, wx2T_ref,
                  cosT_ref, sinT_ref, qT_ref):
    scale = LOG2E / math.sqrt(MLA_NOPE + MLA_ROPE)
    xn = _rms(h_ref[...], g_ref[...]).astype(BF16)
    cq = _rms(_dot(xn, wdq_ref[...]), gcq_ref[...]).astype(BF16)
    cqT = cq.T
    nT = _dot(wnT_ref[...], cqT) * scale
    x1T = _dot(wx1T_ref[...], cqT)
    x2T = _dot(wx2T_ref[...], cqT)
    cosT = jnp.concatenate([cosT_ref[...]] * HEADS, axis=0)
    sinT = jnp.concatenate([sinT_ref[...]] * HEADS, axis=0)
    r1 = (x1T * cosT - x2T * sinT) * scale
    r2 = (x1T * sinT + x2T * cosT) * scale
    zeros = jnp.zeros((SLOT - MLA_NOPE - MLA_ROPE, TM), BF16)
    for h in range(HEADS):
        qT_ref[h, 0:MLA_NOPE, :] = nT[h * MLA_NOPE:(h + 1) * MLA_NOPE].astype(BF16)
        qT_ref[h, MLA_NOPE:MLA_NOPE + HALF_ROPE, :] = (
            r1[h * HALF_ROPE:(h + 1) * HALF_ROPE].astype(BF16))
        qT_ref[h, MLA_NOPE + HALF_ROPE:MLA_NOPE + MLA_ROPE, :] = (
            r2[h * HALF_ROPE:(h + 1) * HALF_ROPE].astype(BF16))
        qT_ref[h, MLA_NOPE + MLA_ROPE:, :] = zeros


def _mla_q(h, g, wdq, gcq, wnT, wx1T, wx2T, cosT, sinT):
    n = SEQ // TM
    return pl.pallas_call(
        _mla_q_kernel,
        grid=(n,),
        in_specs=[
            pl.BlockSpec((TM, D_MODEL), lambda i: (i, 0)),
            _const_spec((1, D_MODEL)),
            _const_spec((D_MODEL, Q_RANK)),
            _const_spec((1, Q_RANK)),
            _const_spec((HEADS * MLA_NOPE, Q_RANK)),
            _const_spec((HEADS * HALF_ROPE, Q_RANK)),
            _const_spec((HEADS * HALF_ROPE, Q_RANK)),
            pl.BlockSpec((HALF_ROPE, TM), lambda i: (0, i)),
            pl.BlockSpec((HALF_ROPE, TM), lambda i: (0, i)),
        ],
        out_specs=pl.BlockSpec((HEADS, SLOT, TM), lambda i: (0, 0, i)),
        out_shape=jax.ShapeDtypeStruct((HEADS, SLOT, SEQ), BF16),
        compiler_params=_params(("arbitrary",)),
        name="mla_q",
    )(h, g, wdq, gcq, wnT, wx1T, wx2T, cosT, sinT)


def _pad_heads_cols(w, width):
    k = w.shape[0]
    w = w.reshape(k, HEADS, width)
    w = jnp.pad(w, ((0, 0), (0, 0), (0, SLOT - width)))
    return w.reshape(k, AUG)


def _fox_bias_placement():
    pk = np.zeros((3 * LANES, AUG), np.float32)
    for h in range(HEADS):
        for j in range(3):
            pk[HEADS, h * SLOT + HEAD_DIM + j] = 1.0
            pk[j * LANES + h, h * SLOT + HEAD_DIM + 3 + j] = -1.0
    pq = np.zeros((AUG, 3 * LANES), np.float32)
    for h in range(HEADS):
        for j in range(3):
            pq[h * SLOT + HEAD_DIM + j, j * LANES + h] = 1.0
            pq[h * SLOT + HEAD_DIM + 3 + j, HEADS] = 1.0
    return jnp.asarray(pk, BF16), jnp.asarray(pq, BF16)


def _rope_placement():
    p1 = np.zeros((LANES, AUG), np.float32)
    p2 = np.zeros((LANES, AUG), np.float32)
    for h in range(HEADS):
        for j in range(HALF_ROPE):
            p1[j, h * SLOT + MLA_NOPE + j] = 1.0
            p2[j, h * SLOT + MLA_NOPE + HALF_ROPE + j] = 1.0
    return jnp.asarray(p1, BF16), jnp.asarray(p2, BF16)


def kernel(x, positions, attn_norm, ffn_norm, w_gate, w_up, w_down, fox_w_in, fox_b_f,
           fox_w_o, kv_norm, w_kv_a, ckv_norm, w_uk, w_uv, mla_w_dq, cq_norm, mla_w_uq,
           mla_w_o, final_norm):
    assert x.shape == (1, SEQ, D_MODEL)
    h = x.reshape(SEQ, D_MODEL)
    W = HEADS * HEAD_DIM
    pk, pqT = _fox_bias_placement()
    p1, p2 = _rope_placement()
    tri = jnp.asarray(np.tril(np.ones((TM, TM), np.float32)), BF16)
    row = lambda v: v.reshape(1, -1)

    def ffn_args(l):
        return (row(ffn_norm[l]), w_gate[l].astype(BF16), w_up[l].astype(BF16),
                w_down[l].astype(BF16), row(final_norm))

    for l in range(N_A):
        w_in = fox_w_in[l]
        wqT = _pad_heads_cols(w_in[:, :W], HEAD_DIM).T.astype(BF16)
        wk = _pad_heads_cols(w_in[:, W:2 * W], HEAD_DIM).astype(BF16)
        wvT = w_in[:, 2 * W:3 * W].T.astype(BF16)
        wf = jnp.pad(w_in[:, 3 * W:], ((0, 0), (0, LANES - HEADS))).astype(BF16)
        bf = jnp.pad(fox_b_f[l], (0, LANES - HEADS)).reshape(1, LANES)
        qT, k, vT = _fox_proj(h, row(attn_norm[l]), wqT, wk, wvT, wf, bf, pk, pqT, tri)
        oT = _attention(qT.reshape(HEADS, SLOT, SEQ), k, vT)
        h = _out_ffn(oT, h, fox_w_o[l].astype(BF16), *ffn_args(l), final=False)

    invf = ROPE_THETA ** (-jnp.arange(0, HALF_ROPE, dtype=F32) * 2.0 / MLA_ROPE)
    invf = jnp.pad(invf, (0, LANES - HALF_ROPE)).reshape(1, LANES)
    pos_col = positions.reshape(SEQ, 1).astype(F32)
    cos, sin, cosT, sinT = _rope_tables(pos_col, invf)
    wa = jnp.concatenate([
        w_kv_a[:, :KV_RANK],
        jnp.pad(w_kv_a[:, KV_RANK:KV_RANK + HALF_ROPE], ((0, 0), (0, LANES - HALF_ROPE))),
        jnp.pad(w_kv_a[:, KV_RANK + HALF_ROPE:], ((0, 0), (0, LANES - HALF_ROPE))),
    ], axis=1).astype(BF16)
    wuk = _pad_heads_cols(w_uk.reshape(KV_RANK, HEADS * MLA_NOPE), MLA_NOPE).astype(BF16)
    wuvT = w_uv.reshape(KV_RANK, HEADS * HEAD_DIM).T.astype(BF16)
    k_sh, vT_sh = _mla_kv(h, row(kv_norm), wa, row(ckv_norm), cos, sin, wuk, p1, p2, wuvT)

    for l in range(N_A, DEPTH):
        j = l - N_A
        wuq = mla_w_uq[j].reshape(Q_RANK, HEADS, MLA_NOPE + MLA_ROPE)
        wnT = wuq[:, :, :MLA_NOPE].reshape(Q_RANK, -1).T.astype(BF16)
        wx1T = wuq[:, :, MLA_NOPE:MLA_NOPE + HALF_ROPE].reshape(Q_RANK, -1).T.astype(BF16)
        wx2T = wuq[:, :, MLA_NOPE + HALF_ROPE:].reshape(Q_RANK, -1).T.astype(BF16)
        qT = _mla_q(h, row(attn_norm[l]), mla_w_dq[j].astype(BF16), row(cq_norm[j]),
                    wnT, wx1T, wx2T, cosT, sinT)
        oT = _attention(qT, k_sh, vT_sh)
        h = _out_ffn(oT, h, mla_w_o[j].astype(BF16), *ffn_args(l), final=(l == DEPTH - 1))

    return h.reshape(1, SEQ, D_MODEL)
```

```python
import functools
import math

import numpy as np
import jax
import jax.numpy as jnp
from jax import lax
from jax.experimental import pallas as pl
from jax.experimental.pallas import tpu as pltpu

F32 = jnp.float32
BF16 = jnp.bfloat16

D_MODEL = 1024
SEQ = 16384
DEPTH = 4
N_A = DEPTH // 2
RMS_EPS = 1e-6
HEADS = 16
HEAD_DIM = 64
MLA_NOPE = 64
MLA_ROPE = 32
HALF_ROPE = MLA_ROPE // 2
KV_RANK = 256
Q_RANK = 768
ROPE_THETA = 10000.0
D_FF = 2816

LANES = 128
SLOT = 128
AUG = HEADS * SLOT
LOG2E = math.log2(math.e)
NEG = -1e30

TM = 512
TQ = 1024
CH = TQ // 2
ACC_ROWS = HEAD_DIM + 16
FF_CHUNK = D_FF // 2
VMEM_LIMIT = 56 * 1024 * 1024


def _dot(a, b):
    return jnp.dot(a, b, preferred_element_type=F32)


def _rms(x, g):
    inv = lax.rsqrt(jnp.mean(x * x, axis=-1, keepdims=True) + RMS_EPS)
    return (x * inv) * g


def _split3(x):
    hi = x.astype(BF16)
    r = x - hi.astype(F32)
    mid = r.astype(BF16)
    lo = (r - mid.astype(F32)).astype(BF16)
    return hi, mid, lo


def _store_heads(k_ref, k):
    for h in range(HEADS):
        k_ref[h] = k[:, h * SLOT:(h + 1) * SLOT]


def _const_spec(shape):
    nd = len(shape)
    return pl.BlockSpec(shape, lambda *_: (0,) * nd, pipeline_mode=pl.Buffered(1))


def _params(sem):
    return pltpu.CompilerParams(dimension_semantics=sem, vmem_limit_bytes=VMEM_LIMIT)


def _fox_proj_kernel(h_ref, g_ref, wqT_ref, wk_ref, wvT_ref, wf_ref, bf_ref,
                     pk_ref, pqT_ref, tri_ref, qT_ref, k_ref, vT_ref, carry_ref):
    @pl.when(pl.program_id(0) == 0)
    def _():
        carry_ref[...] = jnp.zeros_like(carry_ref)

    xn = _rms(h_ref[...], g_ref[...]).astype(BF16)
    xnT = xn.T

    f = _dot(xn, wf_ref[...]) + bf_ref[...]
    logf = (jnp.minimum(f, 0.0) - jnp.log1p(jnp.exp(-jnp.abs(f)))) * LOG2E
    hi, mid, lo = _split3(logf)
    tri = tri_ref[...]
    cum = _dot(tri, hi) + _dot(tri, mid) + _dot(tri, lo) + carry_ref[...]
    carry_ref[...] = cum[TM - 1:TM, :]

    lane = lax.broadcasted_iota(jnp.int32, cum.shape, 1)
    c = jnp.where(lane < HEADS, cum, 1.0)
    c3 = jnp.concatenate(_split3(c), axis=1)
    c3T = jnp.concatenate(_split3(c.T), axis=0)

    kd = _dot(xn, wk_ref[...])
    kb = _dot(c3, pk_ref[...])
    low = lane < HEAD_DIM
    for g in range(HEADS // 2):
        kg = kd[:, g * LANES:(g + 1) * LANES]
        bg = kb[:, g * LANES:(g + 1) * LANES]
        k_ref[2 * g] = jnp.where(low, kg, bg).astype(BF16)
        k_ref[2 * g + 1] = jnp.where(low, bg, kg).astype(BF16)
    qd = (_dot(wqT_ref[...], xnT) * (LOG2E / math.sqrt(HEAD_DIM))).astype(BF16)
    qb = _dot(pqT_ref[...], c3T).astype(BF16)
    for h in range(HEADS):
        val = qd[h * HEAD_DIM:(h + 1) * HEAD_DIM]
        bias = qb[h * HEAD_DIM:(h + 1) * HEAD_DIM]
        first, second = (val, bias) if h % 2 == 0 else (bias, val)
        qT_ref[h * SLOT:h * SLOT + HEAD_DIM, :] = first
        qT_ref[h * SLOT + HEAD_DIM:(h + 1) * SLOT, :] = second
    vT_ref[...] = _dot(wvT_ref[...], xnT).astype(BF16)


def _fox_proj(h, g, wqT, wk, wvT, wf, bf, pk, pqT, tri):
    n = SEQ // TM
    return pl.pallas_call(
        _fox_proj_kernel,
        grid=(n,),
        in_specs=[
            pl.BlockSpec((TM, D_MODEL), lambda i: (i, 0)),
            _const_spec((1, D_MODEL)),
            _const_spec((D_MODEL, D_MODEL)),
            _const_spec((D_MODEL, D_MODEL)),
            _const_spec((D_MODEL, D_MODEL)),
            _const_spec((D_MODEL, LANES)),
            _const_spec((1, LANES)),
            _const_spec((3 * LANES, D_MODEL)),
            _const_spec((D_MODEL, 3 * LANES)),
            _const_spec((TM, TM)),
        ],
        out_specs=[
            pl.BlockSpec((AUG, TM), lambda i: (0, i)),
            pl.BlockSpec((HEADS, TM, SLOT), lambda i: (0, i, 0)),
            pl.BlockSpec((D_MODEL, TM), lambda i: (0, i)),
        ],
        out_shape=[
            jax.ShapeDtypeStruct((AUG, SEQ), BF16),
            jax.ShapeDtypeStruct((HEADS, SEQ, SLOT), BF16),
            jax.ShapeDtypeStruct((D_MODEL, SEQ), BF16),
        ],
        scratch_shapes=[pltpu.VMEM((1, LANES), F32)],
        compiler_params=_params(("arbitrary",)),
        name="fox_proj",
    )(h, g, wqT, wk, wvT, wf, bf, pk, pqT, tri)


def _attn_kernel(qT_ref, k_ref, vT_ref, o_ref, sa_scr, sb_scr, pa_scr, pb_scr,
                 smax_a_scr, smax_b_scr, alpha_b_scr, m_scr, acc_scr):
    _attn_tile(jnp.int32(0), True, qT_ref, k_ref, vT_ref, o_ref, sa_scr, sb_scr, pa_scr, pb_scr,
               smax_a_scr, smax_b_scr, alpha_b_scr, m_scr, acc_scr)

    def tile(i, carry):
        _attn_tile(i, False, qT_ref, k_ref, vT_ref, o_ref, sa_scr, sb_scr, pa_scr, pb_scr,
                   smax_a_scr, smax_b_scr, alpha_b_scr, m_scr, acc_scr)
        return carry

    lax.fori_loop(1, SEQ // TQ, tile, 0)


def _attn_tile(i, is_first, qT_ref, k_ref, vT_ref, o_ref, sa_scr, sb_scr, pa_scr, pb_scr,
               smax_a_scr, smax_b_scr, alpha_b_scr, m_scr, acc_scr):
    groups = CH // 8
    ALL = slice(0, TQ)
    HI = slice(CH, TQ)
    q0 = pl.multiple_of(i * TQ, TQ)

    def scores(c, s_ref, smax_ref, cols=ALL, diag=False, qbase=q0):
        n = cols.stop - cols.start
        off = pl.multiple_of(c * CH, CH)
        q_cols = pl.ds(pl.multiple_of(qbase + cols.start, CH), n)
        s = _dot(k_ref[pl.ds(off, CH), :], qT_ref[:, q_cols])
        if diag:
            row = lax.broadcasted_iota(jnp.int32, (CH, n), 0)
            col = lax.broadcasted_iota(jnp.int32, (CH, n), 1)
            s = jnp.where(row <= col, s, NEG)
        s_ref[:, cols] = s
        smax_ref[:, cols] = jnp.max(s.reshape(groups, 8, n), axis=0)

    def probs(s_ref, p_ref, smax_ref, cols=ALL):
        m_old = m_scr[:, cols]
        m_new = jnp.maximum(m_old, jnp.max(smax_ref[:, cols], axis=0, keepdims=True))
        alpha = jnp.exp2(m_old - m_new)
        m_scr[:, cols] = m_new
        p_ref[:, cols] = jnp.exp2(s_ref[:, cols] - m_new).astype(BF16)
        return alpha

    ones_rows = jnp.ones((ACC_ROWS - HEAD_DIM, CH), BF16)

    def pv(c, p_ref, alpha, cols=ALL):
        off = pl.multiple_of(c * CH, CH)
        lhs = jnp.concatenate([vT_ref[:, pl.ds(off, CH)], ones_rows], axis=0)
        acc_scr[:, cols] = alpha * acc_scr[:, cols] + _dot(lhs, p_ref[:, cols])

    def score_pair(t, diag):
        scores(2 * t, sa_scr, smax_a_scr, ALL, diag)
        scores(2 * t + 1, sb_scr, smax_b_scr, HI if diag else ALL, diag)

    def pair(t, next_diag, first=False, diag=False):
        cols_b = HI if diag else ALL
        if not first:
            pv(jnp.maximum(2 * t - 1, 0), pb_scr, alpha_b_scr[...])
        alpha_a = probs(sa_scr, pa_scr, smax_a_scr)
        if next_diag is not None:
            scores(2 * t + 2, sa_scr, smax_a_scr, ALL, next_diag)
        pv(2 * t, pa_scr, alpha_a)
        alpha_b_scr[:, cols_b] = probs(sb_scr, pb_scr, smax_b_scr, cols_b)
        if next_diag is not None:
            scores(2 * t + 3, sb_scr, smax_b_scr, HI if next_diag else ALL, next_diag)

    def plain_pair(t):
        halves = (slice(0, CH), HI)
        alpha_a = [None, None]
        for j, cols in enumerate(halves):
            alpha_a[j] = probs(sa_scr, pa_scr, smax_a_scr, cols)
            scores(2 * t + 2, sa_scr, smax_a_scr, cols)
            pv(jnp.maximum(2 * t - 1, 0), pb_scr, alpha_b_scr[:, cols], cols)
        for j, cols in enumerate(halves):
            alpha_b_scr[:, cols] = probs(sb_scr, pb_scr, smax_b_scr, cols)
            scores(2 * t + 3, sb_scr, smax_b_scr, cols)
            pv(2 * t, pa_scr, alpha_a[j], cols)

    def init():
        m_scr[...] = jnp.full(m_scr.shape, NEG, F32)
        acc_scr[...] = jnp.zeros(acc_scr.shape, F32)

    def finish():
        pv(2 * i + 1, pb_scr, alpha_b_scr[:, HI], HI)
        l = acc_scr[HEAD_DIM:HEAD_DIM + 1, :]
        o_ref[:, pl.ds(q0, TQ)] = (acc_scr[:HEAD_DIM, :] / l).astype(BF16)

    def next_tile_scores():
        q_next = pl.multiple_of(jnp.minimum(i + 1, SEQ // TQ - 1) * TQ, TQ)
        scores(0, sa_scr, smax_a_scr, ALL, False, q_next)
        scores(1, sb_scr, smax_b_scr, ALL, False, q_next)

    if is_first:
        init()
        score_pair(0, True)
        pair(0, None, first=True, diag=True)
        next_tile_scores()
        finish()
    else:
        init()
        pb_scr[...] = jnp.zeros(pb_scr.shape, BF16)
        alpha_b_scr[...] = jnp.ones(alpha_b_scr.shape, F32)

        def body(t, carry):
            plain_pair(t)
            return carry

        lax.fori_loop(0, i - 1, body, 0)
        pair(i - 1, True)
        pair(i, None, diag=True)
        next_tile_scores()
        finish()


def _attention(qT, k, vT):
    return pl.pallas_call(
        _attn_kernel,
        grid=(HEADS,),
        in_specs=[
            pl.BlockSpec((None, SLOT, SEQ), lambda h: (h, 0, 0)),
            pl.BlockSpec((None, SEQ, SLOT), lambda h: (h, 0, 0)),
            pl.BlockSpec((HEAD_DIM, SEQ), lambda h: (h, 0)),
        ],
        out_specs=pl.BlockSpec((HEAD_DIM, SEQ), lambda h: (h, 0)),
        out_shape=jax.ShapeDtypeStruct((HEADS * HEAD_DIM, SEQ), BF16),
        scratch_shapes=[
            pltpu.VMEM((CH, TQ), F32),
            pltpu.VMEM((CH, TQ), F32),
            pltpu.VMEM((CH, TQ), BF16),
            pltpu.VMEM((CH, TQ), BF16),
            pltpu.VMEM((8, TQ), F32),
            pltpu.VMEM((8, TQ), F32),
            pltpu.VMEM((1, TQ), F32),
            pltpu.VMEM((1, TQ), F32),
            pltpu.VMEM((ACC_ROWS, TQ), F32),
        ],
        compiler_params=_params(("arbitrary",)),
        name="attention",
    )(qT, k, vT)


def _out_ffn_kernel(oT_ref, h_ref, wo_ref, g_ref, wg_ref, wu_ref, wd_ref, gfin_ref,
                    out_ref, *, final):
    o = oT_ref[...].T
    h1 = h_ref[...] + _dot(o, wo_ref[...])
    xn = _rms(h1, g_ref[...]).astype(BF16)
    acc = h1
    for c in range(D_FF // FF_CHUNK):
        sl = slice(c * FF_CHUNK, (c + 1) * FF_CHUNK)
        gate = _dot(xn, wg_ref[:, sl])
        up = _dot(xn, wu_ref[:, sl])
        act = (gate * (1.0 / (1.0 + jnp.exp(-gate))) * up).astype(BF16)
        acc = acc + _dot(act, wd_ref[sl, :])
    if final:
        acc = _rms(acc, gfin_ref[...])
    out_ref[...] = acc


def _out_ffn(oT, h, wo, g, wg, wu, wd, gfin, final):
    n = SEQ // TM
    return pl.pallas_call(
        functools.partial(_out_ffn_kernel, final=final),
        grid=(n,),
        in_specs=[
            pl.BlockSpec((D_MODEL, TM), lambda i: (0, i)),
            pl.BlockSpec((TM, D_MODEL), lambda i: (i, 0)),
            _const_spec((D_MODEL, D_MODEL)),
            _const_spec((1, D_MODEL)),
            _const_spec((D_MODEL, D_FF)),
            _const_spec((D_MODEL, D_FF)),
            _const_spec((D_FF, D_MODEL)),
            _const_spec((1, D_MODEL)),
        ],
        out_specs=pl.BlockSpec((TM, D_MODEL), lambda i: (i, 0)),
        out_shape=jax.ShapeDtypeStruct((SEQ, D_MODEL), F32),
        compiler_params=_params(("arbitrary",)),
        name="out_ffn",
    )(oT, h, wo, g, wg, wu, wd, gfin)


def _rope_table_kernel(pos_ref, invf_ref, cos_ref, sin_ref, cosT_ref, sinT_ref):
    ang = pos_ref[...] * invf_ref[...]
    cos = jnp.cos(ang)
    sin = jnp.sin(ang)
    cos_ref[...] = cos
    sin_ref[...] = sin
    cosT_ref[...] = cos.T
    sinT_ref[...] = sin.T


def _rope_tables(pos_col, invf):
    n = SEQ // TM
    row = pl.BlockSpec((TM, LANES), lambda i: (i, 0))
    colT = pl.BlockSpec((LANES, TM), lambda i: (0, i))
    return pl.pallas_call(
        _rope_table_kernel,
        grid=(n,),
        in_specs=[pl.BlockSpec((TM, 1), lambda i: (i, 0)), _const_spec((1, LANES))],
        out_specs=[row, row, colT, colT],
        out_shape=[jax.ShapeDtypeStruct((SEQ, LANES), F32)] * 2
        + [jax.ShapeDtypeStruct((LANES, SEQ), F32)] * 2,
        compiler_params=_params(("arbitrary",)),
        name="rope_tables",
    )(pos_col, invf)


def _mla_kv_kernel(h_ref, g_ref, wa_ref, gckv_ref, cos_ref, sin_ref, wuk_ref,
                   p1_ref, p2_ref, wuvT_ref, k_ref, vT_ref):
    hn = _rms(h_ref[...], g_ref[...]).astype(BF16)
    a = _dot(hn, wa_ref[...])
    c = _rms(a[:, :KV_RANK], gckv_ref[...]).astype(BF16)
    x1 = a[:, KV_RANK:KV_RANK + LANES]
    x2 = a[:, KV_RANK + LANES:]
    cos = cos_ref[...]
    sin = sin_ref[...]
    r1 = (x1 * cos - x2 * sin).astype(BF16)
    r2 = (x1 * sin + x2 * cos).astype(BF16)
    k = _dot(c, wuk_ref[...]) + _dot(r1, p1_ref[...]) + _dot(r2, p2_ref[...])
    _store_heads(k_ref, k.astype(BF16))
    vT_ref[...] = _dot(wuvT_ref[...], c.T).astype(BF16)


def _mla_kv(h, g, wa, gckv, cos, sin, wuk, p1, p2, wuvT):
    n = SEQ // TM
    return pl.pallas_call(
        _mla_kv_kernel,
        grid=(n,),
        in_specs=[
            pl.BlockSpec((TM, D_MODEL), lambda i: (i, 0)),
            _const_spec((1, D_MODEL)),
            _const_spec((D_MODEL, 4 * LANES)),
            _const_spec((1, KV_RANK)),
            pl.BlockSpec((TM, LANES), lambda i: (i, 0)),
            pl.BlockSpec((TM, LANES), lambda i: (i, 0)),
            _const_spec((KV_RANK, AUG)),
            _const_spec((LANES, AUG)),
            _const_spec((LANES, AUG)),
            _const_spec((D_MODEL, KV_RANK)),
        ],
        out_specs=[
            pl.BlockSpec((HEADS, TM, SLOT), lambda i: (0, i, 0)),
            pl.BlockSpec((D_MODEL, TM), lambda i: (0, i)),
        ],
        out_shape=[
            jax.ShapeDtypeStruct((HEADS, SEQ, SLOT), BF16),
            jax.ShapeDtypeStruct((D_MODEL, SEQ), BF16),
        ],
        compiler_params=_params(("arbitrary",)),
        name="mla_kv",
    )(h, g, wa, gckv, cos, sin, wuk, p1, p2, wuvT)


def _mla_q_kernel(h_ref, g_ref, wdq_ref, gcq_ref, wnT_ref, wx1T_ref, wx2T_ref,
                  cosT_ref, sinT_ref, qT_ref):
    scale = LOG2E / math.sqrt(MLA_NOPE + MLA_ROPE)
    xn = _rms(h_ref[...], g_ref[...]).astype(BF16)
    cq = _rms(_dot(xn, wdq_ref[...]), gcq_ref[...]).astype(BF16)
    cqT = cq.T
    nT = _dot(wnT_ref[...], cqT) * scale
    x1T = _dot(wx1T_ref[...], cqT)
    x2T = _dot(wx2T_ref[...], cqT)
    cosT = jnp.concatenate([cosT_ref[...]] * HEADS, axis=0)
    sinT = jnp.concatenate([sinT_ref[...]] * HEADS, axis=0)
    r1 = (x1T * cosT - x2T * sinT) * scale
    r2 = (x1T * sinT + x2T * cosT) * scale
    zeros = jnp.zeros((SLOT - MLA_NOPE - MLA_ROPE, TM), BF16)
    for h in range(HEADS):
        qT_ref[h, 0:MLA_NOPE, :] = nT[h * MLA_NOPE:(h + 1) * MLA_NOPE].astype(BF16)
        qT_ref[h, MLA_NOPE:MLA_NOPE + HALF_ROPE, :] = (
            r1[h * HALF_ROPE:(h + 1) * HALF_ROPE].astype(BF16))
        qT_ref[h, MLA_NOPE + HALF_ROPE:MLA_NOPE + MLA_ROPE, :] = (
            r2[h * HALF_ROPE:(h + 1) * HALF_ROPE].astype(BF16))
        qT_ref[h, MLA_NOPE + MLA_ROPE:, :] = zeros


def _mla_q(h, g, wdq, gcq, wnT, wx1T, wx2T, cosT, sinT):
    n = SEQ // TM
    return pl.pallas_call(
        _mla_q_kernel,
        grid=(n,),
        in_specs=[
            pl.BlockSpec((TM, D_MODEL), lambda i: (i, 0)),
            _const_spec((1, D_MODEL)),
            _const_spec((D_MODEL, Q_RANK)),
            _const_spec((1, Q_RANK)),
            _const_spec((HEADS * MLA_NOPE, Q_RANK)),
            _const_spec((HEADS * HALF_ROPE, Q_RANK)),
            _const_spec((HEADS * HALF_ROPE, Q_RANK)),
            pl.BlockSpec((HALF_ROPE, TM), lambda i: (0, i)),
            pl.BlockSpec((HALF_ROPE, TM), lambda i: (0, i)),
        ],
        out_specs=pl.BlockSpec((HEADS, SLOT, TM), lambda i: (0, 0, i)),
        out_shape=jax.ShapeDtypeStruct((HEADS, SLOT, SEQ), BF16),
        compiler_params=_params(("arbitrary",)),
        name="mla_q",
    )(h, g, wdq, gcq, wnT, wx1T, wx2T, cosT, sinT)


def _pad_heads_cols(w, width):
    k = w.shape[0]
    w = w.reshape(k, HEADS, width)
    w = jnp.pad(w, ((0, 0), (0, 0), (0, SLOT - width)))
    return w.reshape(k, AUG)


def _fox_bias_placement():
    w = HEADS * HEAD_DIM
    pk = np.zeros((3 * LANES, w), np.float32)
    for h in range(HEADS):
        base = (h // 2) * LANES + (HEAD_DIM if h % 2 == 0 else 0)
        for j in range(3):
            pk[HEADS, base + j] = 1.0
            pk[j * LANES + h, base + 3 + j] = -1.0
    pq = np.zeros((w, 3 * LANES), np.float32)
    for h in range(HEADS):
        for j in range(3):
            pq[h * HEAD_DIM + j, j * LANES + h] = 1.0
            pq[h * HEAD_DIM + 3 + j, HEADS] = 1.0
    return jnp.asarray(pk, BF16), jnp.asarray(pq, BF16)


def _rope_placement():
    p1 = np.zeros((LANES, AUG), np.float32)
    p2 = np.zeros((LANES, AUG), np.float32)
    for h in range(HEADS):
        for j in range(HALF_ROPE):
            p1[j, h * SLOT + MLA_NOPE + j] = 1.0
            p2[j, h * SLOT + MLA_NOPE + HALF_ROPE + j] = 1.0
    return jnp.asarray(p1, BF16), jnp.asarray(p2, BF16)


def kernel(x, positions, attn_norm, ffn_norm, w_gate, w_up, w_down, fox_w_in, fox_b_f,
           fox_w_o, kv_norm, w_kv_a, ckv_norm, w_uk, w_uv, mla_w_dq, cq_norm, mla_w_uq,
           mla_w_o, final_norm):
    assert x.shape == (1, SEQ, D_MODEL)
    h = x.reshape(SEQ, D_MODEL)
    W = HEADS * HEAD_DIM
    pk, pqT = _fox_bias_placement()
    p1, p2 = _rope_placement()
    tri = jnp.asarray(np.tril(np.ones((TM, TM), np.float32)), BF16)
    row = lambda v: v.reshape(1, -1)

    def ffn_args(l):
        return (row(ffn_norm[l]), w_gate[l].astype(BF16), w_up[l].astype(BF16),
                w_down[l].astype(BF16), row(final_norm))

    for l in range(N_A):
        w_in = fox_w_in[l]
        wqT = w_in[:, :W].T.astype(BF16)
        wk = w_in[:, W:2 * W].astype(BF16)
        wvT = w_in[:, 2 * W:3 * W].T.astype(BF16)
        wf = jnp.pad(w_in[:, 3 * W:], ((0, 0), (0, LANES - HEADS))).astype(BF16)
        bf = jnp.pad(fox_b_f[l], (0, LANES - HEADS)).reshape(1, LANES)
        qT, k, vT = _fox_proj(h, row(attn_norm[l]), wqT, wk, wvT, wf, bf, pk, pqT, tri)
        oT = _attention(qT.reshape(HEADS, SLOT, SEQ), k, vT)
        h = _out_ffn(oT, h, fox_w_o[l].astype(BF16), *ffn_args(l), final=False)

    invf = ROPE_THETA ** (-jnp.arange(0, HALF_ROPE, dtype=F32) * 2.0 / MLA_ROPE)
    invf = jnp.pad(invf, (0, LANES - HALF_ROPE)).reshape(1, LANES)
    pos_col = positions.reshape(SEQ, 1).astype(F32)
    cos, sin, cosT, sinT = _rope_tables(pos_col, invf)
    wa = jnp.concatenate([
        w_kv_a[:, :KV_RANK],
        jnp.pad(w_kv_a[:, KV_RANK:KV_RANK + HALF_ROPE], ((0, 0), (0, LANES - HALF_ROPE))),
        jnp.pad(w_kv_a[:, KV_RANK + HALF_ROPE:], ((0, 0), (0, LANES - HALF_ROPE))),
    ], axis=1).astype(BF16)
    wuk = _pad_heads_cols(w_uk.reshape(KV_RANK, HEADS * MLA_NOPE), MLA_NOPE).astype(BF16)
    wuvT = w_uv.reshape(KV_RANK, HEADS * HEAD_DIM).T.astype(BF16)
    k_sh, vT_sh = _mla_kv(h, row(kv_norm), wa, row(ckv_norm), cos, sin, wuk, p1, p2, wuvT)

    for l in range(N_A, DEPTH):
        j = l - N_A
        wuq = mla_w_uq[j].reshape(Q_RANK, HEADS, MLA_NOPE + MLA_ROPE)
        wnT = wuq[:, :, :MLA_NOPE].reshape(Q_RANK, -1).T.astype(BF16)
        wx1T = wuq[:, :, MLA_NOPE:MLA_NOPE + HALF_ROPE].reshape(Q_RANK, -1).T.astype(BF16)
        wx2T = wuq[:, :, MLA_NOPE + HALF_ROPE:].reshape(Q_RANK, -1).T.astype(BF16)
        qT = _mla_q(h, row(attn_norm[l]), mla_w_dq[j].astype(BF16), row(cq_norm[j]),
                    wnT, wx1T, wx2T, cosT, sinT)
        oT = _attention(qT, k_sh, vT_sh)
        h = _out_ffn(oT, h, mla_w_o[j].astype(BF16), *ffn_args(l), final=(l == DEPTH - 1))

    return h.reshape(1, SEQ, D_MODEL)
```

```python
import functools
import math

import numpy as np
import jax
import jax.numpy as jnp
from jax import lax
from jax.experimental import pallas as pl
from jax.experimental.pallas import tpu as pltpu

F32 = jnp.float32
BF16 = jnp.bfloat16

D_MODEL = 1024
SEQ = 16384
DEPTH = 4
N_A = DEPTH // 2
RMS_EPS = 1e-6
HEADS = 16
HEAD_DIM = 64
MLA_NOPE = 64
MLA_ROPE = 32
HALF_ROPE = MLA_ROPE // 2
KV_RANK = 256
Q_RANK = 768
ROPE_THETA = 10000.0
D_FF = 2816

LANES = 128
SLOT = 128
AUG = HEADS * SLOT
LOG2E = math.log2(math.e)
NEG = -1e30

TM = 512
TQ = 1024
CH = TQ // 2
ACC_ROWS = HEAD_DIM + 16
FF_CHUNK = 256
VMEM_LIMIT = 56 * 1024 * 1024


def _dot(a, b):
    return jnp.dot(a, b, preferred_element_type=F32)


def _rms(x, g):
    inv = lax.rsqrt(jnp.mean(x * x, axis=-1, keepdims=True) + RMS_EPS)
    return (x * inv) * g


def _split3(x):
    hi = x.astype(BF16)
    r = x - hi.astype(F32)
    mid = r.astype(BF16)
    lo = (r - mid.astype(F32)).astype(BF16)
    return hi, mid, lo


def _store_heads(k_ref, k):
    for h in range(HEADS):
        k_ref[h] = k[:, h * SLOT:(h + 1) * SLOT]


def _const_spec(shape):
    nd = len(shape)
    return pl.BlockSpec(shape, lambda *_: (0,) * nd, pipeline_mode=pl.Buffered(1))


def _params(sem):
    return pltpu.CompilerParams(dimension_semantics=sem, vmem_limit_bytes=VMEM_LIMIT)


def _fox_proj_kernel(h_ref, g_ref, wqT_ref, wk_ref, wvT_ref, wf_ref, bf_ref,
                     pk_ref, pqT_ref, tri_ref, qT_ref, k_ref, vT_ref, carry_ref):
    @pl.when(pl.program_id(0) == 0)
    def _():
        carry_ref[...] = jnp.zeros_like(carry_ref)

    xn = _rms(h_ref[...], g_ref[...]).astype(BF16)
    xnT = xn.T

    f = _dot(xn, wf_ref[...]) + bf_ref[...]
    logf = (jnp.minimum(f, 0.0) - jnp.log1p(jnp.exp(-jnp.abs(f)))) * LOG2E
    hi, mid, lo = _split3(logf)
    tri = tri_ref[...]
    cum = _dot(tri, hi) + _dot(tri, mid) + _dot(tri, lo) + carry_ref[...]
    carry_ref[...] = cum[TM - 1:TM, :]

    lane = lax.broadcasted_iota(jnp.int32, cum.shape, 1)
    c = jnp.where(lane < HEADS, cum, 1.0)
    c3 = jnp.concatenate(_split3(c), axis=1)
    c3T = jnp.concatenate(_split3(c.T), axis=0)

    kd = _dot(xn, wk_ref[...])
    kb = _dot(c3, pk_ref[...])
    low = lane < HEAD_DIM
    for g in range(HEADS // 2):
        kg = kd[:, g * LANES:(g + 1) * LANES]
        bg = kb[:, g * LANES:(g + 1) * LANES]
        k_ref[2 * g] = jnp.where(low, kg, bg).astype(BF16)
        k_ref[2 * g + 1] = jnp.where(low, bg, kg).astype(BF16)
    qd = (_dot(wqT_ref[...], xnT) * (LOG2E / math.sqrt(HEAD_DIM))).astype(BF16)
    qb = _dot(pqT_ref[...], c3T).astype(BF16)
    for h in range(HEADS):
        val = qd[h * HEAD_DIM:(h + 1) * HEAD_DIM]
        bias = qb[h * HEAD_DIM:(h + 1) * HEAD_DIM]
        first, second = (val, bias) if h % 2 == 0 else (bias, val)
        qT_ref[h * SLOT:h * SLOT + HEAD_DIM, :] = first
        qT_ref[h * SLOT + HEAD_DIM:(h + 1) * SLOT, :] = second
    vT_ref[...] = _dot(wvT_ref[...], xnT).astype(BF16)


def _fox_proj(h, g, wqT, wk, wvT, wf, bf, pk, pqT, tri):
    n = SEQ // TM
    return pl.pallas_call(
        _fox_proj_kernel,
        grid=(n,),
        in_specs=[
            pl.BlockSpec((TM, D_MODEL), lambda i: (i, 0)),
            _const_spec((1, D_MODEL)),
            _const_spec((D_MODEL, D_MODEL)),
            _const_spec((D_MODEL, D_MODEL)),
            _const_spec((D_MODEL, D_MODEL)),
            _const_spec((D_MODEL, LANES)),
            _const_spec((1, LANES)),
            _const_spec((3 * LANES, D_MODEL)),
            _const_spec((D_MODEL, 3 * LANES)),
            _const_spec((TM, TM)),
        ],
        out_specs=[
            pl.BlockSpec((AUG, TM), lambda i: (0, i)),
            pl.BlockSpec((HEADS, TM, SLOT), lambda i: (0, i, 0)),
            pl.BlockSpec((D_MODEL, TM), lambda i: (0, i)),
        ],
        out_shape=[
            jax.ShapeDtypeStruct((AUG, SEQ), BF16),
            jax.ShapeDtypeStruct((HEADS, SEQ, SLOT), BF16),
            jax.ShapeDtypeStruct((D_MODEL, SEQ), BF16),
        ],
        scratch_shapes=[pltpu.VMEM((1, LANES), F32)],
        compiler_params=_params(("arbitrary",)),
        name="fox_proj",
    )(h, g, wqT, wk, wvT, wf, bf, pk, pqT, tri)


def _attn_kernel(qT_ref, k_ref, vT_ref, o_ref, sa_scr, sb_scr, pa_scr, pb_scr,
                 smax_a_scr, smax_b_scr, alpha_b_scr, m_scr, acc_scr):
    _attn_tile(jnp.int32(0), True, qT_ref, k_ref, vT_ref, o_ref, sa_scr, sb_scr, pa_scr, pb_scr,
               smax_a_scr, smax_b_scr, alpha_b_scr, m_scr, acc_scr)

    def tile(i, carry):
        _attn_tile(i, False, qT_ref, k_ref, vT_ref, o_ref, sa_scr, sb_scr, pa_scr, pb_scr,
                   smax_a_scr, smax_b_scr, alpha_b_scr, m_scr, acc_scr)
        return carry

    lax.fori_loop(1, SEQ // TQ, tile, 0)


def _attn_tile(i, is_first, qT_ref, k_ref, vT_ref, o_ref, sa_scr, sb_scr, pa_scr, pb_scr,
               smax_a_scr, smax_b_scr, alpha_b_scr, m_scr, acc_scr):
    groups = CH // 8
    ALL = slice(0, TQ)
    HI = slice(CH, TQ)
    q0 = pl.multiple_of(i * TQ, TQ)

    def scores(c, s_ref, smax_ref, cols=ALL, diag=False, qbase=q0):
        n = cols.stop - cols.start
        off = pl.multiple_of(c * CH, CH)
        q_cols = pl.ds(pl.multiple_of(qbase + cols.start, CH), n)
        s = _dot(k_ref[pl.ds(off, CH), :], qT_ref[:, q_cols])
        if diag:
            row = lax.broadcasted_iota(jnp.int32, (CH, n), 0)
            col = lax.broadcasted_iota(jnp.int32, (CH, n), 1)
            s = jnp.where(row <= col, s, NEG)
        s_ref[:, cols] = s
        smax_ref[:, cols] = jnp.max(s.reshape(groups, 8, n), axis=0)

    def probs(s_ref, p_ref, smax_ref, cols=ALL):
        m_old = m_scr[:, cols]
        m_new = jnp.maximum(m_old, jnp.max(smax_ref[:, cols], axis=0, keepdims=True))
        alpha = jnp.exp2(m_old - m_new)
        m_scr[:, cols] = m_new
        p_ref[:, cols] = jnp.exp2(s_ref[:, cols] - m_new).astype(BF16)
        return alpha

    ones_rows = jnp.ones((ACC_ROWS - HEAD_DIM, CH), BF16)

    def pv(c, p_ref, alpha, cols=ALL):
        off = pl.multiple_of(c * CH, CH)
        lhs = jnp.concatenate([vT_ref[:, pl.ds(off, CH)], ones_rows], axis=0)
        acc_scr[:, cols] = alpha * acc_scr[:, cols] + _dot(lhs, p_ref[:, cols])

    def score_pair(t, diag):
        scores(2 * t, sa_scr, smax_a_scr, ALL, diag)
        scores(2 * t + 1, sb_scr, smax_b_scr, HI if diag else ALL, diag)

    def pair(t, next_diag, first=False, diag=False):
        cols_b = HI if diag else ALL
        if not first:
            pv(jnp.maximum(2 * t - 1, 0), pb_scr, alpha_b_scr[...])
        alpha_a = probs(sa_scr, pa_scr, smax_a_scr)
        if next_diag is not None:
            scores(2 * t + 2, sa_scr, smax_a_scr, ALL, next_diag)
        pv(2 * t, pa_scr, alpha_a)
        alpha_b_scr[:, cols_b] = probs(sb_scr, pb_scr, smax_b_scr, cols_b)
        if next_diag is not None:
            scores(2 * t + 3, sb_scr, smax_b_scr, HI if next_diag else ALL, next_diag)

    def plain_pair(t):
        halves = (slice(0, CH), HI)
        alpha_a = [None, None]
        for j, cols in enumerate(halves):
            alpha_a[j] = probs(sa_scr, pa_scr, smax_a_scr, cols)
            scores(2 * t + 2, sa_scr, smax_a_scr, cols)
            pv(jnp.maximum(2 * t - 1, 0), pb_scr, alpha_b_scr[:, cols], cols)
        for j, cols in enumerate(halves):
            alpha_b_scr[:, cols] = probs(sb_scr, pb_scr, smax_b_scr, cols)
            scores(2 * t + 3, sb_scr, smax_b_scr, cols)
            pv(2 * t, pa_scr, alpha_a[j], cols)

    def init():
        m_scr[...] = jnp.full(m_scr.shape, NEG, F32)
        acc_scr[...] = jnp.zeros(acc_scr.shape, F32)

    def finish():
        pv(2 * i + 1, pb_scr, alpha_b_scr[:, HI], HI)
        l = acc_scr[HEAD_DIM:HEAD_DIM + 1, :]
        o_ref[:, pl.ds(q0, TQ)] = (acc_scr[:HEAD_DIM, :] / l).astype(BF16)

    def next_tile_scores():
        q_next = pl.multiple_of(jnp.minimum(i + 1, SEQ // TQ - 1) * TQ, TQ)
        scores(0, sa_scr, smax_a_scr, ALL, False, q_next)
        scores(1, sb_scr, smax_b_scr, ALL, False, q_next)

    if is_first:
        init()
        score_pair(0, True)
        pair(0, None, first=True, diag=True)
        next_tile_scores()
        finish()
    else:
        init()
        pb_scr[...] = jnp.zeros(pb_scr.shape, BF16)
        alpha_b_scr[...] = jnp.ones(alpha_b_scr.shape, F32)

        def body(t, carry):
            plain_pair(t)
            return carry

        lax.fori_loop(0, i - 1, body, 0)
        pair(i - 1, True)
        pair(i, None, diag=True)
        next_tile_scores()
        finish()


def _attention(qT, k, vT):
    return pl.pallas_call(
        _attn_kernel,
        grid=(HEADS,),
        in_specs=[
            pl.BlockSpec((None, SLOT, SEQ), lambda h: (h, 0, 0)),
            pl.BlockSpec((None, SEQ, SLOT), lambda h: (h, 0, 0)),
            pl.BlockSpec((HEAD_DIM, SEQ), lambda h: (h, 0)),
        ],
        out_specs=pl.BlockSpec((HEAD_DIM, SEQ), lambda h: (h, 0)),
        out_shape=jax.ShapeDtypeStruct((HEADS * HEAD_DIM, SEQ), BF16),
        scratch_shapes=[
            pltpu.VMEM((CH, TQ), F32),
            pltpu.VMEM((CH, TQ), F32),
            pltpu.VMEM((CH, TQ), BF16),
            pltpu.VMEM((CH, TQ), BF16),
            pltpu.VMEM((8, TQ), F32),
            pltpu.VMEM((8, TQ), F32),
            pltpu.VMEM((1, TQ), F32),
            pltpu.VMEM((1, TQ), F32),
            pltpu.VMEM((ACC_ROWS, TQ), F32),
        ],
        compiler_params=_params(("arbitrary",)),
        name="attention",
    )(qT, k, vT)


def _out_ffn_kernel(oT_ref, h_ref, wo_ref, g_ref, wg_ref, wu_ref, wd_ref, gfin_ref,
                    out_ref, *, final):
    o = oT_ref[...].T
    h1 = h_ref[...] + _dot(o, wo_ref[...])
    xn = _rms(h1, g_ref[...]).astype(BF16)
    acc = h1
    for c in range(D_FF // FF_CHUNK):
        sl = slice(c * FF_CHUNK, (c + 1) * FF_CHUNK)
        gate = _dot(xn, wg_ref[:, sl])
        up = _dot(xn, wu_ref[:, sl])
        act = (gate * (1.0 / (1.0 + jnp.exp(-gate))) * up).astype(BF16)
        acc = acc + _dot(act, wd_ref[sl, :])
    if final:
        acc = _rms(acc, gfin_ref[...])
    out_ref[...] = acc


def _out_ffn(oT, h, wo, g, wg, wu, wd, gfin, final):
    n = SEQ // TM
    return pl.pallas_call(
        functools.partial(_out_ffn_kernel, final=final),
        grid=(n,),
        in_specs=[
            pl.BlockSpec((D_MODEL, TM), lambda i: (0, i)),
            pl.BlockSpec((TM, D_MODEL), lambda i: (i, 0)),
            _const_spec((D_MODEL, D_MODEL)),
            _const_spec((1, D_MODEL)),
            _const_spec((D_MODEL, D_FF)),
            _const_spec((D_MODEL, D_FF)),
            _const_spec((D_FF, D_MODEL)),
            _const_spec((1, D_MODEL)),
        ],
        out_specs=pl.BlockSpec((TM, D_MODEL), lambda i: (i, 0)),
        out_shape=jax.ShapeDtypeStruct((SEQ, D_MODEL), F32),
        compiler_params=_params(("arbitrary",)),
        name="out_ffn",
    )(oT, h, wo, g, wg, wu, wd, gfin)


def _rope_table_kernel(pos_ref, invf_ref, cos_ref, sin_ref, cosT_ref, sinT_ref):
    ang = pos_ref[...] * invf_ref[...]
    cos = jnp.cos(ang)
    sin = jnp.sin(ang)
    cos_ref[...] = cos
    sin_ref[...] = sin
    cosT_ref[...] = cos.T
    sinT_ref[...] = sin.T


def _rope_tables(pos_col, invf):
    n = SEQ // TM
    row = pl.BlockSpec((TM, LANES), lambda i: (i, 0))
    colT = pl.BlockSpec((LANES, TM), lambda i: (0, i))
    return pl.pallas_call(
        _rope_table_kernel,
        grid=(n,),
        in_specs=[pl.BlockSpec((TM, 1), lambda i: (i, 0)), _const_spec((1, LANES))],
        out_specs=[row, row, colT, colT],
        out_shape=[jax.ShapeDtypeStruct((SEQ, LANES), F32)] * 2
        + [jax.ShapeDtypeStruct((LANES, SEQ), F32)] * 2,
        compiler_params=_params(("arbitrary",)),
        name="rope_tables",
    )(pos_col, invf)


def _mla_kv_kernel(h_ref, g_ref, wa_ref, gckv_ref, cos_ref, sin_ref, wuk_ref,
                   p1_ref, p2_ref, wuvT_ref, k_ref, vT_ref):
    hn = _rms(h_ref[...], g_ref[...]).astype(BF16)
    a = _dot(hn, wa_ref[...])
    c = _rms(a[:, :KV_RANK], gckv_ref[...]).astype(BF16)
    x1 = a[:, KV_RANK:KV_RANK + LANES]
    x2 = a[:, KV_RANK + LANES:]
    cos = cos_ref[...]
    sin = sin_ref[...]
    r1 = (x1 * cos - x2 * sin).astype(BF16)
    r2 = (x1 * sin + x2 * cos).astype(BF16)
    k = _dot(c, wuk_ref[...]) + _dot(r1, p1_ref[...]) + _dot(r2, p2_ref[...])
    _store_heads(k_ref, k.astype(BF16))
    vT_ref[...] = _dot(wuvT_ref[...], c.T).astype(BF16)


def _mla_kv(h, g, wa, gckv, cos, sin, wuk, p1, p2, wuvT):
    n = SEQ // TM
    return pl.pallas_call(
        _mla_kv_kernel,
        grid=(n,),
        in_specs=[
            pl.BlockSpec((TM, D_MODEL), lambda i: (i, 0)),
            _const_spec((1, D_MODEL)),
            _const_spec((D_MODEL, 4 * LANES)),
            _const_spec((1, KV_RANK)),
            pl.BlockSpec((TM, LANES), lambda i: (i, 0)),
            pl.BlockSpec((TM, LANES), lambda i: (i, 0)),
            _const_spec((KV_RANK, AUG)),
            _const_spec((LANES, AUG)),
            _const_spec((LANES, AUG)),
            _const_spec((D_MODEL, KV_RANK)),
        ],
        out_specs=[
            pl.BlockSpec((HEADS, TM, SLOT), lambda i: (0, i, 0)),
            pl.BlockSpec((D_MODEL, TM), lambda i: (0, i)),
        ],
        out_shape=[
            jax.ShapeDtypeStruct((HEADS, SEQ, SLOT), BF16),
            jax.ShapeDtypeStruct((D_MODEL, SEQ), BF16),
        ],
        compiler_params=_params(("arbitrary",)),
        name="mla_kv",
    )(h, g, wa, gckv, cos, sin, wuk, p1, p2, wuvT)


def _mla_q_kernel(h_ref, g_ref, wdq_ref, gcq_ref, wnT_ref, wx1T_ref, wx2T_ref,
                  cosT_ref, sinT_ref, qT_ref):
    scale = LOG2E / math.sqrt(MLA_NOPE + MLA_ROPE)
    xn = _rms(h_ref[...], g_ref[...]).astype(BF16)
    cq = _rms(_dot(xn, wdq_ref[...]), gcq_ref[...]).astype(BF16)
    cqT = cq.T
    nT = _dot(wnT_ref[...], cqT) * scale
    x1T = _dot(wx1T_ref[...], cqT)
    x2T = _dot(wx2T_ref[...], cqT)
    cosT = jnp.concatenate([cosT_ref[...]] * HEADS, axis=0)
    sinT = jnp.concatenate([sinT_ref[...]] * HEADS, axis=0)
    r1 = (x1T * cosT - x2T * sinT) * scale
    r2 = (x1T * sinT + x2T * cosT) * scale
    zeros = jnp.zeros((SLOT - MLA_NOPE - MLA_ROPE, TM), BF16)
    for h in range(HEADS):
        qT_ref[h, 0:MLA_NOPE, :] = nT[h * MLA_NOPE:(h + 1) * MLA_NOPE].astype(BF16)
        qT_ref[h, MLA_NOPE:MLA_NOPE + HALF_ROPE, :] = (
            r1[h * HALF_ROPE:(h + 1) * HALF_ROPE].astype(BF16))
        qT_ref[h, MLA_NOPE + HALF_ROPE:MLA_NOPE + MLA_ROPE, :] = (
            r2[h * HALF_ROPE:(h + 1) * HALF_ROPE].astype(BF16))
        qT_ref[h, MLA_NOPE + MLA_ROPE:, :] = zeros


def _mla_q(h, g, wdq, gcq, wnT, wx1T, wx2T, cosT, sinT):
    n = SEQ // TM
    return pl.pallas_call(
        _mla_q_kernel,
        grid=(n,),
        in_specs=[
            pl.BlockSpec((TM, D_MODEL), lambda i: (i, 0)),
            _const_spec((1, D_MODEL)),
            _const_spec((D_MODEL, Q_RANK)),
            _const_spec((1, Q_RANK)),
            _const_spec((HEADS * MLA_NOPE, Q_RANK)),
            _const_spec((HEADS * HALF_ROPE, Q_RANK)),
            _const_spec((HEADS * HALF_ROPE, Q_RANK)),
            pl.BlockSpec((HALF_ROPE, TM), lambda i: (0, i)),
            pl.BlockSpec((HALF_ROPE, TM), lambda i: (0, i)),
        ],
        out_specs=pl.BlockSpec((HEADS, SLOT, TM), lambda i: (0, 0, i)),
        out_shape=jax.ShapeDtypeStruct((HEADS, SLOT, SEQ), BF16),
        compiler_params=_params(("arbitrary",)),
        name="mla_q",
    )(h, g, wdq, gcq, wnT, wx1T, wx2T, cosT, sinT)


def _pad_heads_cols(w, width):
    k = w.shape[0]
    w = w.reshape(k, HEADS, width)
    w = jnp.pad(w, ((0, 0), (0, 0), (0, SLOT - width)))
    return w.reshape(k, AUG)


def _fox_bias_placement():
    w = HEADS * HEAD_DIM
    pk = np.zeros((3 * LANES, w), np.float32)
    for h in range(HEADS):
        base = (h // 2) * LANES + (HEAD_DIM if h % 2 == 0 else 0)
        for j in range(3):
            pk[HEADS, base + j] = 1.0
            pk[j * LANES + h, base + 3 + j] = -1.0
    pq = np.zeros((w, 3 * LANES), np.float32)
    for h in range(HEADS):
        for j in range(3):
            pq[h * HEAD_DIM + j, j * LANES + h] = 1.0
            pq[h * HEAD_DIM + 3 + j, HEADS] = 1.0
    return jnp.asarray(pk, BF16), jnp.asarray(pq, BF16)


def _rope_placement():
    p1 = np.zeros((LANES, AUG), np.float32)
    p2 = np.zeros((LANES, AUG), np.float32)
    for h in range(HEADS):
        for j in range(HALF_ROPE):
            p1[j, h * SLOT + MLA_NOPE + j] = 1.0
            p2[j, h * SLOT + MLA_NOPE + HALF_ROPE + j] = 1.0
    return jnp.asarray(p1, BF16), jnp.asarray(p2, BF16)


def kernel(x, positions, attn_norm, ffn_norm, w_gate, w_up, w_down, fox_w_in, fox_b_f,
           fox_w_o, kv_norm, w_kv_a, ckv_norm, w_uk, w_uv, mla_w_dq, cq_norm, mla_w_uq,
           mla_w_o, final_norm):
    assert x.shape == (1, SEQ, D_MODEL)
    h = x.reshape(SEQ, D_MODEL)
    W = HEADS * HEAD_DIM
    pk, pqT = _fox_bias_placement()
    p1, p2 = _rope_placement()
    tri = jnp.asarray(np.tril(np.ones((TM, TM), np.float32)), BF16)
    row = lambda v: v.reshape(1, -1)

    def ffn_args(l):
        return (row(ffn_norm[l]), w_gate[l].astype(BF16), w_up[l].astype(BF16),
                w_down[l].astype(BF16), row(final_norm))

    for l in range(N_A):
        w_in = fox_w_in[l]
        wqT = w_in[:, :W].T.astype(BF16)
        wk = w_in[:, W:2 * W].astype(BF16)
        wvT = w_in[:, 2 * W:3 * W].T.astype(BF16)
        wf = jnp.pad(w_in[:, 3 * W:], ((0, 0), (0, LANES - HEADS))).astype(BF16)
        bf = jnp.pad(fox_b_f[l], (0, LANES - HEADS)).reshape(1, LANES)
        qT, k, vT = _fox_proj(h, row(attn_norm[l]), wqT, wk, wvT, wf, bf, pk, pqT, tri)
        oT = _attention(qT.reshape(HEADS, SLOT, SEQ), k, vT)
        h = _out_ffn(oT, h, fox_w_o[l].astype(BF16), *ffn_args(l), final=False)

    invf = ROPE_THETA ** (-jnp.arange(0, HALF_ROPE, dtype=F32) * 2.0 / MLA_ROPE)
    invf = jnp.pad(invf, (0, LANES - HALF_ROPE)).reshape(1, LANES)
    pos_col = positions.reshape(SEQ, 1).astype(F32)
    cos, sin, cosT, sinT = _rope_tables(pos_col, invf)
    wa = jnp.concatenate([
        w_kv_a[:, :KV_RANK],
        jnp.pad(w_kv_a[:, KV_RANK:KV_RANK + HALF_ROPE], ((0, 0), (0, LANES - HALF_ROPE))),
        jnp.pad(w_kv_a[:, KV_RANK + HALF_ROPE:], ((0, 0), (0, LANES - HALF_ROPE))),
    ], axis=1).astype(BF16)
    wuk = _pad_heads_cols(w_uk.reshape(KV_RANK, HEADS * MLA_NOPE), MLA_NOPE).astype(BF16)
    wuvT = w_uv.reshape(KV_RANK, HEADS * HEAD_DIM).T.astype(BF16)
    k_sh, vT_sh = _mla_kv(h, row(kv_norm), wa, row(ckv_norm), cos, sin, wuk, p1, p2, wuvT)

    for l in range(N_A, DEPTH):
        j = l - N_A
        wuq = mla_w_uq[j].reshape(Q_RANK, HEADS, MLA_NOPE + MLA_ROPE)
        wnT = wuq[:, :, :MLA_NOPE].reshape(Q_RANK, -1).T.astype(BF16)
        wx1T = wuq[:, :, MLA_NOPE:MLA_NOPE + HALF_ROPE].reshape(Q_RANK, -1).T.astype(BF16)
        wx2T = wuq[:, :, MLA_NOPE + HALF_ROPE:].reshape(Q_RANK, -1).T.astype(BF16)
        qT = _mla_q(h, row(attn_norm[l]), mla_w_dq[j].astype(BF16), row(cq_norm[j]),
                    wnT, wx1T, wx2T, cosT, sinT)
        oT = _attention(qT, k_sh, vT_sh)
        h = _out_ffn(oT, h, mla_w_o[j].astype(BF16), *ffn_args(l), final=(l == DEPTH - 1))

    return h.reshape(1, SEQ, D_MODEL)
```

```python
import functools
import math

import numpy as np
import jax
import jax.numpy as jnp
from jax import lax
from jax.experimental import pallas as pl
from jax.experimental.pallas import tpu as pltpu

F32 = jnp.float32
BF16 = jnp.bfloat16

D_MODEL = 1024
SEQ = 16384
DEPTH = 4
N_A = DEPTH // 2
RMS_EPS = 1e-6
HEADS = 16
HEAD_DIM = 64
MLA_NOPE = 64
MLA_ROPE = 32
HALF_ROPE = MLA_ROPE // 2
KV_RANK = 256
Q_RANK = 768
ROPE_THETA = 10000.0
D_FF = 2816

LANES = 128
SLOT = 128
AUG = HEADS * SLOT
LOG2E = math.log2(math.e)
NEG = -1e30

TM = 512
TQ = 1024
CH = TQ // 2
WQ = 256
ACC_ROWS = HEAD_DIM + 16
FF_CHUNK = 256
VMEM_LIMIT = 56 * 1024 * 1024


def _dot(a, b):
    return jnp.dot(a, b, preferred_element_type=F32)


def _rms(x, g):
    inv = lax.rsqrt(jnp.mean(x * x, axis=-1, keepdims=True) + RMS_EPS)
    return (x * inv) * g


def _split3(x):
    hi = x.astype(BF16)
    r = x - hi.astype(F32)
    mid = r.astype(BF16)
    lo = (r - mid.astype(F32)).astype(BF16)
    return hi, mid, lo


def _store_heads(k_ref, k):
    for h in range(HEADS):
        k_ref[h] = k[:, h * SLOT:(h + 1) * SLOT]


def _const_spec(shape):
    nd = len(shape)
    return pl.BlockSpec(shape, lambda *_: (0,) * nd, pipeline_mode=pl.Buffered(1))


def _params(sem):
    return pltpu.CompilerParams(dimension_semantics=sem, vmem_limit_bytes=VMEM_LIMIT)


def _fox_proj_kernel(h_ref, g_ref, wqT_ref, wk_ref, wvT_ref, wf_ref, bf_ref,
                     pk_ref, pqT_ref, tri_ref, qT_ref, k_ref, vT_ref, carry_ref):
    @pl.when(pl.program_id(0) == 0)
    def _():
        carry_ref[...] = jnp.zeros_like(carry_ref)

    xn = _rms(h_ref[...], g_ref[...]).astype(BF16)
    xnT = xn.T

    f = _dot(xn, wf_ref[...]) + bf_ref[...]
    logf = (jnp.minimum(f, 0.0) - jnp.log1p(jnp.exp(-jnp.abs(f)))) * LOG2E
    hi, mid, lo = _split3(logf)
    tri = tri_ref[...]
    cum = _dot(tri, hi) + _dot(tri, mid) + _dot(tri, lo) + carry_ref[...]
    carry_ref[...] = cum[TM - 1:TM, :]

    lane = lax.broadcasted_iota(jnp.int32, cum.shape, 1)
    c = jnp.where(lane < HEADS, cum, 1.0)
    c3 = jnp.concatenate(_split3(c), axis=1)
    c3T = jnp.concatenate(_split3(c.T), axis=0)

    kd = _dot(xn, wk_ref[...])
    kb = _dot(c3, pk_ref[...])
    low = lane < HEAD_DIM
    for g in range(HEADS // 2):
        kg = kd[:, g * LANES:(g + 1) * LANES]
        bg = kb[:, g * LANES:(g + 1) * LANES]
        k_ref[2 * g] = jnp.where(low, kg, bg).astype(BF16)
        k_ref[2 * g + 1] = jnp.where(low, bg, kg).astype(BF16)
    qd = (_dot(wqT_ref[...], xnT) * (LOG2E / math.sqrt(HEAD_DIM))).astype(BF16)
    qb = _dot(pqT_ref[...], c3T).astype(BF16)
    for h in range(HEADS):
        val = qd[h * HEAD_DIM:(h + 1) * HEAD_DIM]
        bias = qb[h * HEAD_DIM:(h + 1) * HEAD_DIM]
        first, second = (val, bias) if h % 2 == 0 else (bias, val)
        qT_ref[h * SLOT:h * SLOT + HEAD_DIM, :] = first
        qT_ref[h * SLOT + HEAD_DIM:(h + 1) * SLOT, :] = second
    vT_ref[...] = _dot(wvT_ref[...], xnT).astype(BF16)


def _fox_proj(h, g, wqT, wk, wvT, wf, bf, pk, pqT, tri):
    n = SEQ // TM
    return pl.pallas_call(
        _fox_proj_kernel,
        grid=(n,),
        in_specs=[
            pl.BlockSpec((TM, D_MODEL), lambda i: (i, 0)),
            _const_spec((1, D_MODEL)),
            _const_spec((D_MODEL, D_MODEL)),
            _const_spec((D_MODEL, D_MODEL)),
            _const_spec((D_MODEL, D_MODEL)),
            _const_spec((D_MODEL, LANES)),
            _const_spec((1, LANES)),
            _const_spec((3 * LANES, D_MODEL)),
            _const_spec((D_MODEL, 3 * LANES)),
            _const_spec((TM, TM)),
        ],
        out_specs=[
            pl.BlockSpec((AUG, TM), lambda i: (0, i)),
            pl.BlockSpec((HEADS, TM, SLOT), lambda i: (0, i, 0)),
            pl.BlockSpec((D_MODEL, TM), lambda i: (0, i)),
        ],
        out_shape=[
            jax.ShapeDtypeStruct((AUG, SEQ), BF16),
            jax.ShapeDtypeStruct((HEADS, SEQ, SLOT), BF16),
            jax.ShapeDtypeStruct((D_MODEL, SEQ), BF16),
        ],
        scratch_shapes=[pltpu.VMEM((1, LANES), F32)],
        compiler_params=_params(("arbitrary",)),
        name="fox_proj",
    )(h, g, wqT, wk, wvT, wf, bf, pk, pqT, tri)


def _attn_kernel(qT_ref, k_ref, vT_ref, o_ref, sa_scr, sb_scr, pa_scr, pb_scr,
                 smax_a_scr, smax_b_scr, alpha_b_scr, m_scr, acc_scr):
    _attn_tile(jnp.int32(0), True, qT_ref, k_ref, vT_ref, o_ref, sa_scr, sb_scr, pa_scr, pb_scr,
               smax_a_scr, smax_b_scr, alpha_b_scr, m_scr, acc_scr)

    def tile(i, carry):
        _attn_tile(i, False, qT_ref, k_ref, vT_ref, o_ref, sa_scr, sb_scr, pa_scr, pb_scr,
                   smax_a_scr, smax_b_scr, alpha_b_scr, m_scr, acc_scr)
        return carry

    lax.fori_loop(1, SEQ // TQ, tile, 0)


def _attn_tile(i, is_first, qT_ref, k_ref, vT_ref, o_ref, sa_scr, sb_scr, pa_scr, pb_scr,
               smax_a_scr, smax_b_scr, alpha_b_scr, m_scr, acc_scr):
    groups = CH // 8
    ALL = slice(0, TQ)
    HI = slice(CH, TQ)
    q0 = pl.multiple_of(i * TQ, TQ)

    def scores(c, s_ref, smax_ref, cols=ALL, diag=False, qbase=q0):
        n = cols.stop - cols.start
        off = pl.multiple_of(c * CH, CH)
        q_cols = pl.ds(pl.multiple_of(qbase + cols.start, CH), n)
        s = _dot(k_ref[pl.ds(off, CH), :], qT_ref[:, q_cols])
        if diag:
            row = lax.broadcasted_iota(jnp.int32, (CH, n), 0)
            col = lax.broadcasted_iota(jnp.int32, (CH, n), 1)
            s = jnp.where(row <= col, s, NEG)
        s_ref[:, cols] = s
        smax_ref[:, cols] = jnp.max(s.reshape(groups, 8, n), axis=0)

    def probs(s_ref, p_ref, smax_ref, cols=ALL):
        m_old = m_scr[:, cols]
        m_new = jnp.maximum(m_old, jnp.max(smax_ref[:, cols], axis=0, keepdims=True))
        alpha = jnp.exp2(m_old - m_new)
        m_scr[:, cols] = m_new
        p_ref[:, cols] = jnp.exp2(s_ref[:, cols] - m_new).astype(BF16)
        return alpha

    ones_rows = jnp.ones((ACC_ROWS - HEAD_DIM, CH), BF16)

    def pv(c, p_ref, alpha, cols=ALL):
        off = pl.multiple_of(c * CH, CH)
        lhs = jnp.concatenate([vT_ref[:, pl.ds(off, CH)], ones_rows], axis=0)
        acc_scr[:, cols] = alpha * acc_scr[:, cols] + _dot(lhs, p_ref[:, cols])

    def score_pair(t, diag):
        scores(2 * t, sa_scr, smax_a_scr, ALL, diag)
        scores(2 * t + 1, sb_scr, smax_b_scr, HI if diag else ALL, diag)

    def pair(t, next_diag, first=False, diag=False):
        cols_b = HI if diag else ALL
        if not first:
            pv(jnp.maximum(2 * t - 1, 0), pb_scr, alpha_b_scr[...])
        alpha_a = probs(sa_scr, pa_scr, smax_a_scr)
        if next_diag is not None:
            scores(2 * t + 2, sa_scr, smax_a_scr, ALL, next_diag)
        pv(2 * t, pa_scr, alpha_a)
        alpha_b_scr[:, cols_b] = probs(sb_scr, pb_scr, smax_b_scr, cols_b)
        if next_diag is not None:
            scores(2 * t + 3, sb_scr, smax_b_scr, HI if next_diag else ALL, next_diag)

    def plain_pair(t):
        halves = tuple(slice(j * WQ, (j + 1) * WQ) for j in range(TQ // WQ))
        alpha_a = [None] * len(halves)
        for j, cols in enumerate(halves):
            alpha_a[j] = probs(sa_scr, pa_scr, smax_a_scr, cols)
            scores(2 * t + 2, sa_scr, smax_a_scr, cols)
            pv(jnp.maximum(2 * t - 1, 0), pb_scr, alpha_b_scr[:, cols], cols)
        for j, cols in enumerate(halves):
            alpha_b_scr[:, cols] = probs(sb_scr, pb_scr, smax_b_scr, cols)
            scores(2 * t + 3, sb_scr, smax_b_scr, cols)
            pv(2 * t, pa_scr, alpha_a[j], cols)

    def init():
        m_scr[...] = jnp.full(m_scr.shape, NEG, F32)
        acc_scr[...] = jnp.zeros(acc_scr.shape, F32)

    def finish():
        pv(2 * i + 1, pb_scr, alpha_b_scr[:, HI], HI)
        l = acc_scr[HEAD_DIM:HEAD_DIM + 1, :]
        o_ref[:, pl.ds(q0, TQ)] = (acc_scr[:HEAD_DIM, :] / l).astype(BF16)

    def next_tile_scores():
        q_next = pl.multiple_of(jnp.minimum(i + 1, SEQ // TQ - 1) * TQ, TQ)
        scores(0, sa_scr, smax_a_scr, ALL, False, q_next)
        scores(1, sb_scr, smax_b_scr, ALL, False, q_next)

    if is_first:
        init()
        score_pair(0, True)
        pair(0, None, first=True, diag=True)
        next_tile_scores()
        finish()
    else:
        init()
        pb_scr[...] = jnp.zeros(pb_scr.shape, BF16)
        alpha_b_scr[...] = jnp.ones(alpha_b_scr.shape, F32)

        def body(t, carry):
            plain_pair(t)
            return carry

        lax.fori_loop(0, i - 1, body, 0)
        pair(i - 1, True)
        pair(i, None, diag=True)
        next_tile_scores()
        finish()


def _attention(qT, k, vT):
    return pl.pallas_call(
        _attn_kernel,
        grid=(HEADS,),
        in_specs=[
            pl.BlockSpec((None, SLOT, SEQ), lambda h: (h, 0, 0)),
            pl.BlockSpec((None, SEQ, SLOT), lambda h: (h, 0, 0)),
            pl.BlockSpec((HEAD_DIM, SEQ), lambda h: (h, 0)),
        ],
        out_specs=pl.BlockSpec((HEAD_DIM, SEQ), lambda h: (h, 0)),
        out_shape=jax.ShapeDtypeStruct((HEADS * HEAD_DIM, SEQ), BF16),
        scratch_shapes=[
            pltpu.VMEM((CH, TQ), F32),
            pltpu.VMEM((CH, TQ), F32),
            pltpu.VMEM((CH, TQ), BF16),
            pltpu.VMEM((CH, TQ), BF16),
            pltpu.VMEM((8, TQ), F32),
            pltpu.VMEM((8, TQ), F32),
            pltpu.VMEM((1, TQ), F32),
            pltpu.VMEM((1, TQ), F32),
            pltpu.VMEM((ACC_ROWS, TQ), F32),
        ],
        compiler_params=_params(("arbitrary",)),
        name="attention",
    )(qT, k, vT)


def _out_ffn_kernel(oT_ref, h_ref, wo_ref, g_ref, wg_ref, wu_ref, wd_ref, gfin_ref,
                    out_ref, *, final):
    o = oT_ref[...].T
    h1 = h_ref[...] + _dot(o, wo_ref[...])
    xn = _rms(h1, g_ref[...]).astype(BF16)
    acc = h1
    for c in range(D_FF // FF_CHUNK):
        sl = slice(c * FF_CHUNK, (c + 1) * FF_CHUNK)
        gate = _dot(xn, wg_ref[:, sl])
        up = _dot(xn, wu_ref[:, sl])
        act = (gate * (1.0 / (1.0 + jnp.exp(-gate))) * up).astype(BF16)
        acc = acc + _dot(act, wd_ref[sl, :])
    if final:
        acc = _rms(acc, gfin_ref[...])
    out_ref[...] = acc


def _out_ffn(oT, h, wo, g, wg, wu, wd, gfin, final):
    n = SEQ // TM
    return pl.pallas_call(
        functools.partial(_out_ffn_kernel, final=final),
        grid=(n,),
        in_specs=[
            pl.BlockSpec((D_MODEL, TM), lambda i: (0, i)),
            pl.BlockSpec((TM, D_MODEL), lambda i: (i, 0)),
            _const_spec((D_MODEL, D_MODEL)),
            _const_spec((1, D_MODEL)),
            _const_spec((D_MODEL, D_FF)),
            _const_spec((D_MODEL, D_FF)),
            _const_spec((D_FF, D_MODEL)),
            _const_spec((1, D_MODEL)),
        ],
        out_specs=pl.BlockSpec((TM, D_MODEL), lambda i: (i, 0)),
        out_shape=jax.ShapeDtypeStruct((SEQ, D_MODEL), F32),
        compiler_params=_params(("arbitrary",)),
        name="out_ffn",
    )(oT, h, wo, g, wg, wu, wd, gfin)


def _rope_table_kernel(pos_ref, invf_ref, cos_ref, sin_ref, cosT_ref, sinT_ref):
    ang = pos_ref[...] * invf_ref[...]
    cos = jnp.cos(ang)
    sin = jnp.sin(ang)
    cos_ref[...] = cos
    sin_ref[...] = sin
    cosT_ref[...] = cos.T
    sinT_ref[...] = sin.T


def _rope_tables(pos_col, invf):
    n = SEQ // TM
    row = pl.BlockSpec((TM, LANES), lambda i: (i, 0))
    colT = pl.BlockSpec((LANES, TM), lambda i: (0, i))
    return pl.pallas_call(
        _rope_table_kernel,
        grid=(n,),
        in_specs=[pl.BlockSpec((TM, 1), lambda i: (i, 0)), _const_spec((1, LANES))],
        out_specs=[row, row, colT, colT],
        out_shape=[jax.ShapeDtypeStruct((SEQ, LANES), F32)] * 2
        + [jax.ShapeDtypeStruct((LANES, SEQ), F32)] * 2,
        compiler_params=_params(("arbitrary",)),
        name="rope_tables",
    )(pos_col, invf)


def _mla_kv_kernel(h_ref, g_ref, wa_ref, gckv_ref, cos_ref, sin_ref, wuk_ref,
                   p1_ref, p2_ref, wuvT_ref, k_ref, vT_ref):
    hn = _rms(h_ref[...], g_ref[...]).astype(BF16)
    a = _dot(hn, wa_ref[...])
    c = _rms(a[:, :KV_RANK], gckv_ref[...]).astype(BF16)
    x1 = a[:, KV_RANK:KV_RANK + LANES]
    x2 = a[:, KV_RANK + LANES:]
    cos = cos_ref[...]
    sin = sin_ref[...]
    r1 = (x1 * cos - x2 * sin).astype(BF16)
    r2 = (x1 * sin + x2 * cos).astype(BF16)
    k = _dot(c, wuk_ref[...]) + _dot(r1, p1_ref[...]) + _dot(r2, p2_ref[...])
    _store_heads(k_ref, k.astype(BF16))
    vT_ref[...] = _dot(wuvT_ref[...], c.T).astype(BF16)


def _mla_kv(h, g, wa, gckv, cos, sin, wuk, p1, p2, wuvT):
    n = SEQ // TM
    return pl.pallas_call(
        _mla_kv_kernel,
        grid=(n,),
        in_specs=[
            pl.BlockSpec((TM, D_MODEL), lambda i: (i, 0)),
            _const_spec((1, D_MODEL)),
            _const_spec((D_MODEL, 4 * LANES)),
            _const_spec((1, KV_RANK)),
            pl.BlockSpec((TM, LANES), lambda i: (i, 0)),
            pl.BlockSpec((TM, LANES), lambda i: (i, 0)),
            _const_spec((KV_RANK, AUG)),
            _const_spec((LANES, AUG)),
            _const_spec((LANES, AUG)),
            _const_spec((D_MODEL, KV_RANK)),
        ],
        out_specs=[
            pl.BlockSpec((HEADS, TM, SLOT), lambda i: (0, i, 0)),
            pl.BlockSpec((D_MODEL, TM), lambda i: (0, i)),
        ],
        out_shape=[
            jax.ShapeDtypeStruct((HEADS, SEQ, SLOT), BF16),
            jax.ShapeDtypeStruct((D_MODEL, SEQ), BF16),
        ],
        compiler_params=_params(("arbitrary",)),
        name="mla_kv",
    )(h, g, wa, gckv, cos, sin, wuk, p1, p2, wuvT)


def _mla_q_kernel(h_ref, g_ref, wdq_ref, gcq_ref, wnT_ref, wx1T_ref, wx2T_ref,
                  cosT_ref, sinT_ref, qT_ref):
    scale = LOG2E / math.sqrt(MLA_NOPE + MLA_ROPE)
    xn = _rms(h_ref[...], g_ref[...]).astype(BF16)
    cq = _rms(_dot(xn, wdq_ref[...]), gcq_ref[...]).astype(BF16)
    cqT = cq.T
    nT = _dot(wnT_ref[...], cqT) * scale
    x1T = _dot(wx1T_ref[...], cqT)
    x2T = _dot(wx2T_ref[...], cqT)
    cosT = jnp.concatenate([cosT_ref[...]] * HEADS, axis=0)
    sinT = jnp.concatenate([sinT_ref[...]] * HEADS, axis=0)
    r1 = (x1T * cosT - x2T * sinT) * scale
    r2 = (x1T * sinT + x2T * cosT) * scale
    zeros = jnp.zeros((SLOT - MLA_NOPE - MLA_ROPE, TM), BF16)
    for h in range(HEADS):
        qT_ref[h, 0:MLA_NOPE, :] = nT[h * MLA_NOPE:(h + 1) * MLA_NOPE].astype(BF16)
        qT_ref[h, MLA_NOPE:MLA_NOPE + HALF_ROPE, :] = (
            r1[h * HALF_ROPE:(h + 1) * HALF_ROPE].astype(BF16))
        qT_ref[h, MLA_NOPE + HALF_ROPE:MLA_NOPE + MLA_ROPE, :] = (
            r2[h * HALF_ROPE:(h + 1) * HALF_ROPE].astype(BF16))
        qT_ref[h, MLA_NOPE + MLA_ROPE:, :] = zeros


def _mla_q(h, g, wdq, gcq, wnT, wx1T, wx2T, cosT, sinT):
    n = SEQ // TM
    return pl.pallas_call(
        _mla_q_kernel,
        grid=(n,),
        in_specs=[
            pl.BlockSpec((TM, D_MODEL), lambda i: (i, 0)),
            _const_spec((1, D_MODEL)),
            _const_spec((D_MODEL, Q_RANK)),
            _const_spec((1, Q_RANK)),
            _const_spec((HEADS * MLA_NOPE, Q_RANK)),
            _const_spec((HEADS * HALF_ROPE, Q_RANK)),
            _const_spec((HEADS * HALF_ROPE, Q_RANK)),
            pl.BlockSpec((HALF_ROPE, TM), lambda i: (0, i)),
            pl.BlockSpec((HALF_ROPE, TM), lambda i: (0, i)),
        ],
        out_specs=pl.BlockSpec((HEADS, SLOT, TM), lambda i: (0, 0, i)),
        out_shape=jax.ShapeDtypeStruct((HEADS, SLOT, SEQ), BF16),
        compiler_params=_params(("arbitrary",)),
        name="mla_q",
    )(h, g, wdq, gcq, wnT, wx1T, wx2T, cosT, sinT)


def _pad_heads_cols(w, width):
    k = w.shape[0]
    w = w.reshape(k, HEADS, width)
    w = jnp.pad(w, ((0, 0), (0, 0), (0, SLOT - width)))
    return w.reshape(k, AUG)


def _fox_bias_placement():
    w = HEADS * HEAD_DIM
    pk = np.zeros((3 * LANES, w), np.float32)
    for h in range(HEADS):
        base = (h // 2) * LANES + (HEAD_DIM if h % 2 == 0 else 0)
        for j in range(3):
            pk[HEADS, base + j] = 1.0
            pk[j * LANES + h, base + 3 + j] = -1.0
    pq = np.zeros((w, 3 * LANES), np.float32)
    for h in range(HEADS):
        for j in range(3):
            pq[h * HEAD_DIM + j, j * LANES + h] = 1.0
            pq[h * HEAD_DIM + 3 + j, HEADS] = 1.0
    return jnp.asarray(pk, BF16), jnp.asarray(pq, BF16)


def _rope_placement():
    p1 = np.zeros((LANES, AUG), np.float32)
    p2 = np.zeros((LANES, AUG), np.float32)
    for h in range(HEADS):
        for j in range(HALF_ROPE):
            p1[j, h * SLOT + MLA_NOPE + j] = 1.0
            p2[j, h * SLOT + MLA_NOPE + HALF_ROPE + j] = 1.0
    return jnp.asarray(p1, BF16), jnp.asarray(p2, BF16)


def kernel(x, positions, attn_norm, ffn_norm, w_gate, w_up, w_down, fox_w_in, fox_b_f,
           fox_w_o, kv_norm, w_kv_a, ckv_norm, w_uk, w_uv, mla_w_dq, cq_norm, mla_w_uq,
           mla_w_o, final_norm):
    assert x.shape == (1, SEQ, D_MODEL)
    h = x.reshape(SEQ, D_MODEL)
    W = HEADS * HEAD_DIM
    pk, pqT = _fox_bias_placement()
    p1, p2 = _rope_placement()
    tri = jnp.asarray(np.tril(np.ones((TM, TM), np.float32)), BF16)
    row = lambda v: v.reshape(1, -1)

    def ffn_args(l):
        return (row(ffn_norm[l]), w_gate[l].astype(BF16), w_up[l].astype(BF16),
                w_down[l].astype(BF16), row(final_norm))

    for l in range(N_A):
        w_in = fox_w_in[l]
        wqT = w_in[:, :W].T.astype(BF16)
        wk = w_in[:, W:2 * W].astype(BF16)
        wvT = w_in[:, 2 * W:3 * W].T.astype(BF16)
        wf = jnp.pad(w_in[:, 3 * W:], ((0, 0), (0, LANES - HEADS))).astype(BF16)
        bf = jnp.pad(fox_b_f[l], (0, LANES - HEADS)).reshape(1, LANES)
        qT, k, vT = _fox_proj(h, row(attn_norm[l]), wqT, wk, wvT, wf, bf, pk, pqT, tri)
        oT = _attention(qT.reshape(HEADS, SLOT, SEQ), k, vT)
        h = _out_ffn(oT, h, fox_w_o[l].astype(BF16), *ffn_args(l), final=False)

    invf = ROPE_THETA ** (-jnp.arange(0, HALF_ROPE, dtype=F32) * 2.0 / MLA_ROPE)
    invf = jnp.pad(invf, (0, LANES - HALF_ROPE)).reshape(1, LANES)
    pos_col = positions.reshape(SEQ, 1).astype(F32)
    cos, sin, cosT, sinT = _rope_tables(pos_col, invf)
    wa = jnp.concatenate([
        w_kv_a[:, :KV_RANK],
        jnp.pad(w_kv_a[:, KV_RANK:KV_RANK + HALF_ROPE], ((0, 0), (0, LANES - HALF_ROPE))),
        jnp.pad(w_kv_a[:, KV_RANK + HALF_ROPE:], ((0, 0), (0, LANES - HALF_ROPE))),
    ], axis=1).astype(BF16)
    wuk = _pad_heads_cols(w_uk.reshape(KV_RANK, HEADS * MLA_NOPE), MLA_NOPE).astype(BF16)
    wuvT = w_uv.reshape(KV_RANK, HEADS * HEAD_DIM).T.astype(BF16)
    k_sh, vT_sh = _mla_kv(h, row(kv_norm), wa, row(ckv_norm), cos, sin, wuk, p1, p2, wuvT)

    for l in range(N_A, DEPTH):
        j = l - N_A
        wuq = mla_w_uq[j].reshape(Q_RANK, HEADS, MLA_NOPE + MLA_ROPE)
        wnT = wuq[:, :, :MLA_NOPE].reshape(Q_RANK, -1).T.astype(BF16)
        wx1T = wuq[:, :, MLA_NOPE:MLA_NOPE + HALF_ROPE].reshape(Q_RANK, -1).T.astype(BF16)
        wx2T = wuq[:, :, MLA_NOPE + HALF_ROPE:].reshape(Q_RANK, -1).T.astype(BF16)
        qT = _mla_q(h, row(attn_norm[l]), mla_w_dq[j].astype(BF16), row(cq_norm[j]),
                    wnT, wx1T, wx2T, cosT, sinT)
        oT = _attention(qT, k_sh, vT_sh)
        h = _out_ffn(oT, h, mla_w_o[j].astype(BF16), *ffn_args(l), final=(l == DEPTH - 1))

    return h.reshape(1, SEQ, D_MODEL)
```

```python
import functools
import math

import numpy as np
import jax
import jax.numpy as jnp
from jax import lax
from jax.experimental import pallas as pl
from jax.experimental.pallas import tpu as pltpu

F32 = jnp.float32
BF16 = jnp.bfloat16

D_MODEL = 1024
SEQ = 16384
DEPTH = 4
N_A = DEPTH // 2
RMS_EPS = 1e-6
HEADS = 16
HEAD_DIM = 64
MLA_NOPE = 64
MLA_ROPE = 32
HALF_ROPE = MLA_ROPE // 2
KV_RANK = 256
Q_RANK = 768
ROPE_THETA = 10000.0
D_FF = 2816

LANES = 128
SLOT = 128
AUG = HEADS * SLOT
LOG2E = math.log2(math.e)
NEG = -1e30

TM = 512
TQ = 1024
CH = TQ // 2
WQ = 256
ACC_ROWS = HEAD_DIM + 16
FF_CHUNK = 256
VMEM_LIMIT = 56 * 1024 * 1024


def _dot(a, b):
    return jnp.dot(a, b, preferred_element_type=F32)


def _rms(x, g):
    inv = lax.rsqrt(jnp.mean(x * x, axis=-1, keepdims=True) + RMS_EPS)
    return (x * inv) * g


def _split3(x):
    hi = x.astype(BF16)
    r = x - hi.astype(F32)
    mid = r.astype(BF16)
    lo = (r - mid.astype(F32)).astype(BF16)
    return hi, mid, lo


def _store_heads(k_ref, k):
    for h in range(HEADS):
        k_ref[h] = k[:, h * SLOT:(h + 1) * SLOT]


def _const_spec(shape):
    nd = len(shape)
    return pl.BlockSpec(shape, lambda *_: (0,) * nd, pipeline_mode=pl.Buffered(1))


def _params(sem):
    return pltpu.CompilerParams(dimension_semantics=sem, vmem_limit_bytes=VMEM_LIMIT)


def _fox_proj_kernel(h_ref, g_ref, wqT_ref, wk_ref, wvT_ref, wf_ref, bf_ref,
                     pk_ref, pqT_ref, tri_ref, qT_ref, k_ref, vT_ref, carry_ref):
    @pl.when(pl.program_id(0) == 0)
    def _():
        carry_ref[...] = jnp.zeros_like(carry_ref)

    xn = _rms(h_ref[...], g_ref[...]).astype(BF16)
    xnT = xn.T

    f = _dot(xn, wf_ref[...]) + bf_ref[...]
    logf = (jnp.minimum(f, 0.0) - jnp.log1p(jnp.exp(-jnp.abs(f)))) * LOG2E
    hi, mid, lo = _split3(logf)
    tri = tri_ref[...]
    cum = _dot(tri, hi) + _dot(tri, mid) + _dot(tri, lo) + carry_ref[...]
    carry_ref[...] = cum[TM - 1:TM, :]

    lane = lax.broadcasted_iota(jnp.int32, cum.shape, 1)
    c = jnp.where(lane < HEADS, cum, 1.0)
    c3 = jnp.concatenate(_split3(c), axis=1)
    c3T = jnp.concatenate(_split3(c.T), axis=0)

    kd = _dot(xn, wk_ref[...])
    kb = _dot(c3, pk_ref[...])
    low = lane < HEAD_DIM
    for g in range(HEADS // 2):
        kg = kd[:, g * LANES:(g + 1) * LANES]
        bg = kb[:, g * LANES:(g + 1) * LANES]
        k_ref[2 * g] = jnp.where(low, kg, bg).astype(BF16)
        k_ref[2 * g + 1] = jnp.where(low, bg, kg).astype(BF16)
    qd = (_dot(wqT_ref[...], xnT) * (LOG2E / math.sqrt(HEAD_DIM))).astype(BF16)
    qb = _dot(pqT_ref[...], c3T).astype(BF16)
    for h in range(HEADS):
        val = qd[h * HEAD_DIM:(h + 1) * HEAD_DIM]
        bias = qb[h * HEAD_DIM:(h + 1) * HEAD_DIM]
        first, second = (val, bias) if h % 2 == 0 else (bias, val)
        qT_ref[h * SLOT:h * SLOT + HEAD_DIM, :] = first
        qT_ref[h * SLOT + HEAD_DIM:(h + 1) * SLOT, :] = second
    vT_ref[...] = _dot(wvT_ref[...], xnT).astype(BF16)


def _fox_proj(h, g, wqT, wk, wvT, wf, bf, pk, pqT, tri):
    n = SEQ // TM
    return pl.pallas_call(
        _fox_proj_kernel,
        grid=(n,),
        in_specs=[
            pl.BlockSpec((TM, D_MODEL), lambda i: (i, 0)),
            _const_spec((1, D_MODEL)),
            _const_spec((D_MODEL, D_MODEL)),
            _const_spec((D_MODEL, D_MODEL)),
            _const_spec((D_MODEL, D_MODEL)),
            _const_spec((D_MODEL, LANES)),
            _const_spec((1, LANES)),
            _const_spec((3 * LANES, D_MODEL)),
            _const_spec((D_MODEL, 3 * LANES)),
            _const_spec((TM, TM)),
        ],
        out_specs=[
            pl.BlockSpec((AUG, TM), lambda i: (0, i)),
            pl.BlockSpec((HEADS, TM, SLOT), lambda i: (0, i, 0)),
            pl.BlockSpec((D_MODEL, TM), lambda i: (0, i)),
        ],
        out_shape=[
            jax.ShapeDtypeStruct((AUG, SEQ), BF16),
            jax.ShapeDtypeStruct((HEADS, SEQ, SLOT), BF16),
            jax.ShapeDtypeStruct((D_MODEL, SEQ), BF16),
        ],
        scratch_shapes=[pltpu.VMEM((1, LANES), F32)],
        compiler_params=_params(("arbitrary",)),
        name="fox_proj",
    )(h, g, wqT, wk, wvT, wf, bf, pk, pqT, tri)


def _attn_kernel(qT_ref, k_ref, vT_ref, o_ref, sa_scr, sb_scr, pa_scr, pb_scr,
                 smax_a_scr, smax_b_scr, alpha_b_scr, m_scr, acc_scr):
    _attn_tile(jnp.int32(0), True, qT_ref, k_ref, vT_ref, o_ref, sa_scr, sb_scr, pa_scr, pb_scr,
               smax_a_scr, smax_b_scr, alpha_b_scr, m_scr, acc_scr)

    def tile(i, carry):
        _attn_tile(i, False, qT_ref, k_ref, vT_ref, o_ref, sa_scr, sb_scr, pa_scr, pb_scr,
                   smax_a_scr, smax_b_scr, alpha_b_scr, m_scr, acc_scr)
        return carry

    lax.fori_loop(1, SEQ // TQ, tile, 0)


def _attn_tile(i, is_first, qT_ref, k_ref, vT_ref, o_ref, sa_scr, sb_scr, pa_scr, pb_scr,
               smax_a_scr, smax_b_scr, alpha_b_scr, m_scr, acc_scr):
    groups = CH // 8
    windows = tuple(slice(j * WQ, (j + 1) * WQ) for j in range(TQ // WQ))
    half = CH // WQ
    q0 = pl.multiple_of(i * TQ, TQ)
    q_next = pl.multiple_of(jnp.minimum(i + 1, SEQ // TQ - 1) * TQ, TQ)

    def scores(c, s_ref, smax_ref, cols, mask_off=None, qbase=q0):
        off = pl.multiple_of(c * CH, CH)
        q_cols = pl.ds(pl.multiple_of(qbase + cols.start, WQ), WQ)
        s = _dot(k_ref[pl.ds(off, CH), :], qT_ref[:, q_cols])
        if mask_off is not None:
            row = lax.broadcasted_iota(jnp.int32, (CH, WQ), 0)
            col = lax.broadcasted_iota(jnp.int32, (CH, WQ), 1)
            s = jnp.where(row <= col + mask_off, s, NEG)
        s_ref[:, cols] = s
        smax_ref[:, cols] = jnp.max(s.reshape(groups, 8, WQ), axis=0)

    def probs(s_ref, p_ref, smax_ref, cols):
        m_old = m_scr[:, cols]
        m_new = jnp.maximum(m_old, jnp.max(smax_ref[:, cols], axis=0, keepdims=True))
        alpha = jnp.exp2(m_old - m_new)
        m_scr[:, cols] = m_new
        p_ref[:, cols] = jnp.exp2(s_ref[:, cols] - m_new).astype(BF16)
        return alpha

    ones_rows = jnp.ones((ACC_ROWS - HEAD_DIM, CH), BF16)

    def pv(c, p_ref, alpha, cols):
        off = pl.multiple_of(c * CH, CH)
        lhs = jnp.concatenate([vT_ref[:, pl.ds(off, CH)], ones_rows], axis=0)
        acc_scr[:, cols] = alpha * acc_scr[:, cols] + _dot(lhs, p_ref[:, cols])

    def next_scores(t, nxt, chunk_b, j, cols):
        s_ref, smax_ref = (sb_scr, smax_b_scr) if chunk_b else (sa_scr, smax_a_scr)
        if nxt == 'plain':
            scores(2 * t + 2 + chunk_b, s_ref, smax_ref, cols)
        elif nxt == 'tile':
            scores(chunk_b, s_ref, smax_ref, cols, None, q_next)
        elif not chunk_b:
            scores(2 * t + 2, s_ref, smax_ref, cols, j * WQ if j < half else None)
        elif j >= half:
            scores(2 * t + 3, s_ref, smax_ref, cols, (j - half) * WQ)

    def pair(t, nxt, first=False, diag=False):
        alpha_a = [None] * len(windows)
        for j, cols in enumerate(windows):
            alpha_a[j] = probs(sa_scr, pa_scr, smax_a_scr, cols)
            next_scores(t, nxt, 0, j, cols)
            if not first:
                pv(jnp.maximum(2 * t - 1, 0), pb_scr, alpha_b_scr[:, cols], cols)
        for j, cols in enumerate(windows):
            if not (diag and j < half):
                alpha_b_scr[:, cols] = probs(sb_scr, pb_scr, smax_b_scr, cols)
            next_scores(t, nxt, 1, j, cols)
            pv(2 * t, pa_scr, alpha_a[j], cols)

    m_scr[...] = jnp.full(m_scr.shape, NEG, F32)
    acc_scr[...] = jnp.zeros(acc_scr.shape, F32)
    if is_first:
        for j, cols in enumerate(windows):
            scores(0, sa_scr, smax_a_scr, cols, j * WQ if j < half else None)
            if j >= half:
                scores(1, sb_scr, smax_b_scr, cols, (j - half) * WQ)
    else:
        pb_scr[...] = jnp.zeros(pb_scr.shape, BF16)
        alpha_b_scr[...] = jnp.ones(alpha_b_scr.shape, F32)

        def body(t, carry):
            pair(t, 'plain')
            return carry

        lax.fori_loop(0, i - 1, body, 0)
        pair(i - 1, 'diag')
    pair(i, 'tile', first=is_first, diag=True)
    for cols in windows[half:]:
        pv(2 * i + 1, pb_scr, alpha_b_scr[:, cols], cols)
    l = acc_scr[HEAD_DIM:HEAD_DIM + 1, :]
    o_ref[:, pl.ds(q0, TQ)] = (acc_scr[:HEAD_DIM, :] / l).astype(BF16)


def _attention(qT, k, vT):
    return pl.pallas_call(
        _attn_kernel,
        grid=(HEADS,),
        in_specs=[
            pl.BlockSpec((None, SLOT, SEQ), lambda h: (h, 0, 0)),
            pl.BlockSpec((None, SEQ, SLOT), lambda h: (h, 0, 0)),
            pl.BlockSpec((HEAD_DIM, SEQ), lambda h: (h, 0)),
        ],
        out_specs=pl.BlockSpec((HEAD_DIM, SEQ), lambda h: (h, 0)),
        out_shape=jax.ShapeDtypeStruct((HEADS * HEAD_DIM, SEQ), BF16),
        scratch_shapes=[
            pltpu.VMEM((CH, TQ), F32),
            pltpu.VMEM((CH, TQ), F32),
            pltpu.VMEM((CH, TQ), BF16),
            pltpu.VMEM((CH, TQ), BF16),
            pltpu.VMEM((8, TQ), F32),
            pltpu.VMEM((8, TQ), F32),
            pltpu.VMEM((1, TQ), F32),
            pltpu.VMEM((1, TQ), F32),
            pltpu.VMEM((ACC_ROWS, TQ), F32),
        ],
        compiler_params=_params(("arbitrary",)),
        name="attention",
    )(qT, k, vT)


def _out_ffn_kernel(oT_ref, h_ref, wo_ref, g_ref, wg_ref, wu_ref, wd_ref, gfin_ref,
                    out_ref, *, final):
    o = oT_ref[...].T
    h1 = h_ref[...] + _dot(o, wo_ref[...])
    xn = _rms(h1, g_ref[...]).astype(BF16)
    acc = h1
    for c in range(D_FF // FF_CHUNK):
        sl = slice(c * FF_CHUNK, (c + 1) * FF_CHUNK)
        gate = _dot(xn, wg_ref[:, sl])
        up = _dot(xn, wu_ref[:, sl])
        act = (gate * (1.0 / (1.0 + jnp.exp(-gate))) * up).astype(BF16)
        acc = acc + _dot(act, wd_ref[sl, :])
    if final:
        acc = _rms(acc, gfin_ref[...])
    out_ref[...] = acc


def _out_ffn(oT, h, wo, g, wg, wu, wd, gfin, final):
    n = SEQ // TM
    return pl.pallas_call(
        functools.partial(_out_ffn_kernel, final=final),
        grid=(n,),
        in_specs=[
            pl.BlockSpec((D_MODEL, TM), lambda i: (0, i)),
            pl.BlockSpec((TM, D_MODEL), lambda i: (i, 0)),
            _const_spec((D_MODEL, D_MODEL)),
            _const_spec((1, D_MODEL)),
            _const_spec((D_MODEL, D_FF)),
            _const_spec((D_MODEL, D_FF)),
            _const_spec((D_FF, D_MODEL)),
            _const_spec((1, D_MODEL)),
        ],
        out_specs=pl.BlockSpec((TM, D_MODEL), lambda i: (i, 0)),
        out_shape=jax.ShapeDtypeStruct((SEQ, D_MODEL), F32),
        compiler_params=_params(("arbitrary",)),
        name="out_ffn",
    )(oT, h, wo, g, wg, wu, wd, gfin)


def _rope_table_kernel(pos_ref, invf_ref, cos_ref, sin_ref, cosT_ref, sinT_ref):
    ang = pos_ref[...] * invf_ref[...]
    cos = jnp.cos(ang)
    sin = jnp.sin(ang)
    cos_ref[...] = cos
    sin_ref[...] = sin
    cosT_ref[...] = cos.T
    sinT_ref[...] = sin.T


def _rope_tables(pos_col, invf):
    n = SEQ // TM
    row = pl.BlockSpec((TM, LANES), lambda i: (i, 0))
    colT = pl.BlockSpec((LANES, TM), lambda i: (0, i))
    return pl.pallas_call(
        _rope_table_kernel,
        grid=(n,),
        in_specs=[pl.BlockSpec((TM, 1), lambda i: (i, 0)), _const_spec((1, LANES))],
        out_specs=[row, row, colT, colT],
        out_shape=[jax.ShapeDtypeStruct((SEQ, LANES), F32)] * 2
        + [jax.ShapeDtypeStruct((LANES, SEQ), F32)] * 2,
        compiler_params=_params(("arbitrary",)),
        name="rope_tables",
    )(pos_col, invf)


def _mla_kv_kernel(h_ref, g_ref, wa_ref, gckv_ref, cos_ref, sin_ref, wuk_ref,
                   p1_ref, p2_ref, wuvT_ref, k_ref, vT_ref):
    hn = _rms(h_ref[...], g_ref[...]).astype(BF16)
    a = _dot(hn, wa_ref[...])
    c = _rms(a[:, :KV_RANK], gckv_ref[...]).astype(BF16)
    x1 = a[:, KV_RANK:KV_RANK + LANES]
    x2 = a[:, KV_RANK + LANES:]
    cos = cos_ref[...]
    sin = sin_ref[...]
    r1 = (x1 * cos - x2 * sin).astype(BF16)
    r2 = (x1 * sin + x2 * cos).astype(BF16)
    k = _dot(c, wuk_ref[...]) + _dot(r1, p1_ref[...]) + _dot(r2, p2_ref[...])
    _store_heads(k_ref, k.astype(BF16))
    vT_ref[...] = _dot(wuvT_ref[...], c.T).astype(BF16)


def _mla_kv(h, g, wa, gckv, cos, sin, wuk, p1, p2, wuvT):
    n = SEQ // TM
    return pl.pallas_call(
        _mla_kv_kernel,
        grid=(n,),
        in_specs=[
            pl.BlockSpec((TM, D_MODEL), lambda i: (i, 0)),
            _const_spec((1, D_MODEL)),
            _const_spec((D_MODEL, 4 * LANES)),
            _const_spec((1, KV_RANK)),
            pl.BlockSpec((TM, LANES), lambda i: (i, 0)),
            pl.BlockSpec((TM, LANES), lambda i: (i, 0)),
            _const_spec((KV_RANK, AUG)),
            _const_spec((LANES, AUG)),
            _const_spec((LANES, AUG)),
            _const_spec((D_MODEL, KV_RANK)),
        ],
        out_specs=[
            pl.BlockSpec((HEADS, TM, SLOT), lambda i: (0, i, 0)),
            pl.BlockSpec((D_MODEL, TM), lambda i: (0, i)),
        ],
        out_shape=[
            jax.ShapeDtypeStruct((HEADS, SEQ, SLOT), BF16),
            jax.ShapeDtypeStruct((D_MODEL, SEQ), BF16),
        ],
        compiler_params=_params(("arbitrary",)),
        name="mla_kv",
    )(h, g, wa, gckv, cos, sin, wuk, p1, p2, wuvT)


def _mla_q_kernel(h_ref, g_ref, wdq_ref, gcq_ref, wnT_ref, wx1T_ref, wx2T_ref,
                  cosT_ref, sinT_ref, qT_ref):
    scale = LOG2E / math.sqrt(MLA_NOPE + MLA_ROPE)
    xn = _rms(h_ref[...], g_ref[...]).astype(BF16)
    cq = _rms(_dot(xn, wdq_ref[...]), gcq_ref[...]).astype(BF16)
    cqT = cq.T
    nT = _dot(wnT_ref[...], cqT) * scale
    x1T = _dot(wx1T_ref[...], cqT)
    x2T = _dot(wx2T_ref[...], cqT)
    cosT = jnp.concatenate([cosT_ref[...]] * HEADS, axis=0)
    sinT = jnp.concatenate([sinT_ref[...]] * HEADS, axis=0)
    r1 = (x1T * cosT - x2T * sinT) * scale
    r2 = (x1T * sinT + x2T * cosT) * scale
    zeros = jnp.zeros((SLOT - MLA_NOPE - MLA_ROPE, TM), BF16)
    for h in range(HEADS):
        qT_ref[h, 0:MLA_NOPE, :] = nT[h * MLA_NOPE:(h + 1) * MLA_NOPE].astype(BF16)
        qT_ref[h, MLA_NOPE:MLA_NOPE + HALF_ROPE, :] = (
            r1[h * HALF_ROPE:(h + 1) * HALF_ROPE].astype(BF16))
        qT_ref[h, MLA_NOPE + HALF_ROPE:MLA_NOPE + MLA_ROPE, :] = (
            r2[h * HALF_ROPE:(h + 1) * HALF_ROPE].astype(BF16))
        qT_ref[h, MLA_NOPE + MLA_ROPE:, :] = zeros


def _mla_q(h, g, wdq, gcq, wnT, wx1T, wx2T, cosT, sinT):
    n = SEQ // TM
    return pl.pallas_call(
        _mla_q_kernel,
        grid=(n,),
        in_specs=[
            pl.BlockSpec((TM, D_MODEL), lambda i: (i, 0)),
            _const_spec((1, D_MODEL)),
            _const_spec((D_MODEL, Q_RANK)),
            _const_spec((1, Q_RANK)),
            _const_spec((HEADS * MLA_NOPE, Q_RANK)),
            _const_spec((HEADS * HALF_ROPE, Q_RANK)),
            _const_spec((HEADS * HALF_ROPE, Q_RANK)),
            pl.BlockSpec((HALF_ROPE, TM), lambda i: (0, i)),
            pl.BlockSpec((HALF_ROPE, TM), lambda i: (0, i)),
        ],
        out_specs=pl.BlockSpec((HEADS, SLOT, TM), lambda i: (0, 0, i)),
        out_shape=jax.ShapeDtypeStruct((HEADS, SLOT, SEQ), BF16),
        compiler_params=_params(("arbitrary",)),
        name="mla_q",
    )(h, g, wdq, gcq, wnT, wx1T, wx2T, cosT, sinT)


def _pad_heads_cols(w, width):
    k = w.shape[0]
    w = w.reshape(k, HEADS, width)
    w = jnp.pad(w, ((0, 0), (0, 0), (0, SLOT - width)))
    return w.reshape(k, AUG)


def _fox_bias_placement():
    w = HEADS * HEAD_DIM
    pk = np.zeros((3 * LANES, w), np.float32)
    for h in range(HEADS):
        base = (h // 2) * LANES + (HEAD_DIM if h % 2 == 0 else 0)
        for j in range(3):
            pk[HEADS, base + j] = 1.0
            pk[j * LANES + h, base + 3 + j] = -1.0
    pq = np.zeros((w, 3 * LANES), np.float32)
    for h in range(HEADS):
        for j in range(3):
            pq[h * HEAD_DIM + j, j * LANES + h] = 1.0
            pq[h * HEAD_DIM + 3 + j, HEADS] = 1.0
    return jnp.asarray(pk, BF16), jnp.asarray(pq, BF16)


def _rope_placement():
    p1 = np.zeros((LANES, AUG), np.float32)
    p2 = np.zeros((LANES, AUG), np.float32)
    for h in range(HEADS):
        for j in range(HALF_ROPE):
            p1[j, h * SLOT + MLA_NOPE + j] = 1.0
            p2[j, h * SLOT + MLA_NOPE + HALF_ROPE + j] = 1.0
    return jnp.asarray(p1, BF16), jnp.asarray(p2, BF16)


def kernel(x, positions, attn_norm, ffn_norm, w_gate, w_up, w_down, fox_w_in, fox_b_f,
           fox_w_o, kv_norm, w_kv_a, ckv_norm, w_uk, w_uv, mla_w_dq, cq_norm, mla_w_uq,
           mla_w_o, final_norm):
    assert x.shape == (1, SEQ, D_MODEL)
    h = x.reshape(SEQ, D_MODEL)
    W = HEADS * HEAD_DIM
    pk, pqT = _fox_bias_placement()
    p1, p2 = _rope_placement()
    tri = jnp.asarray(np.tril(np.ones((TM, TM), np.float32)), BF16)
    row = lambda v: v.reshape(1, -1)

    def ffn_args(l):
        return (row(ffn_norm[l]), w_gate[l].astype(BF16), w_up[l].astype(BF16),
                w_down[l].astype(BF16), row(final_norm))

    for l in range(N_A):
        w_in = fox_w_in[l]
        wqT = w_in[:, :W].T.astype(BF16)
        wk = w_in[:, W:2 * W].astype(BF16)
        wvT = w_in[:, 2 * W:3 * W].T.astype(BF16)
        wf = jnp.pad(w_in[:, 3 * W:], ((0, 0), (0, LANES - HEADS))).astype(BF16)
        bf = jnp.pad(fox_b_f[l], (0, LANES - HEADS)).reshape(1, LANES)
        qT, k, vT = _fox_proj(h, row(attn_norm[l]), wqT, wk, wvT, wf, bf, pk, pqT, tri)
        oT = _attention(qT.reshape(HEADS, SLOT, SEQ), k, vT)
        h = _out_ffn(oT, h, fox_w_o[l].astype(BF16), *ffn_args(l), final=False)

    invf = ROPE_THETA ** (-jnp.arange(0, HALF_ROPE, dtype=F32) * 2.0 / MLA_ROPE)
    invf = jnp.pad(invf, (0, LANES - HALF_ROPE)).reshape(1, LANES)
    pos_col = positions.reshape(SEQ, 1).astype(F32)
    cos, sin, cosT, sinT = _rope_tables(pos_col, invf)
    wa = jnp.concatenate([
        w_kv_a[:, :KV_RANK],
        jnp.pad(w_kv_a[:, KV_RANK:KV_RANK + HALF_ROPE], ((0, 0), (0, LANES - HALF_ROPE))),
        jnp.pad(w_kv_a[:, KV_RANK + HALF_ROPE:], ((0, 0), (0, LANES - HALF_ROPE))),
    ], axis=1).astype(BF16)
    wuk = _pad_heads_cols(w_uk.reshape(KV_RANK, HEADS * MLA_NOPE), MLA_NOPE).astype(BF16)
    wuvT = w_uv.reshape(KV_RANK, HEADS * HEAD_DIM).T.astype(BF16)
    k_sh, vT_sh = _mla_kv(h, row(kv_norm), wa, row(ckv_norm), cos, sin, wuk, p1, p2, wuvT)

    for l in range(N_A, DEPTH):
        j = l - N_A
        wuq = mla_w_uq[j].reshape(Q_RANK, HEADS, MLA_NOPE + MLA_ROPE)
        wnT = wuq[:, :, :MLA_NOPE].reshape(Q_RANK, -1).T.astype(BF16)
        wx1T = wuq[:, :, MLA_NOPE:MLA_NOPE + HALF_ROPE].reshape(Q_RANK, -1).T.astype(BF16)
        wx2T = wuq[:, :, MLA_NOPE + HALF_ROPE:].reshape(Q_RANK, -1).T.astype(BF16)
        qT = _mla_q(h, row(attn_norm[l]), mla_w_dq[j].astype(BF16), row(cq_norm[j]),
                    wnT, wx1T, wx2T, cosT, sinT)
        oT = _attention(qT, k_sh, vT_sh)
        h = _out_ffn(oT, h, mla_w_o[j].astype(BF16), *ffn_args(l), final=(l == DEPTH - 1))

    return h.reshape(1, SEQ, D_MODEL)
```

```python
import functools
import math

import numpy as np
import jax
import jax.numpy as jnp
from jax import lax
from jax.experimental import pallas as pl
from jax.experimental.pallas import tpu as pltpu

F32 = jnp.float32
BF16 = jnp.bfloat16

D_MODEL = 1024
SEQ = 16384
DEPTH = 4
N_A = DEPTH // 2
RMS_EPS = 1e-6
HEADS = 16
HEAD_DIM = 64
MLA_NOPE = 64
MLA_ROPE = 32
HALF_ROPE = MLA_ROPE // 2
KV_RANK = 256
Q_RANK = 768
ROPE_THETA = 10000.0
D_FF = 2816

LANES = 128
SLOT = 128
AUG = HEADS * SLOT
LOG2E = math.log2(math.e)
NEG = -1e30

TM = 512
TQ = 1024
CH = TQ // 2
WQ = 256
ACC_ROWS = HEAD_DIM + 16
FF_CHUNK = 256
VMEM_LIMIT = 56 * 1024 * 1024


def _dot(a, b):
    return jnp.dot(a, b, preferred_element_type=F32)


def _rms(x, g):
    inv = lax.rsqrt(jnp.mean(x * x, axis=-1, keepdims=True) + RMS_EPS)
    return (x * inv) * g


def _split3(x):
    hi = x.astype(BF16)
    r = x - hi.astype(F32)
    mid = r.astype(BF16)
    lo = (r - mid.astype(F32)).astype(BF16)
    return hi, mid, lo


def _store_heads(k_ref, k):
    for h in range(HEADS):
        k_ref[h] = k[:, h * SLOT:(h + 1) * SLOT]


def _const_spec(shape):
    nd = len(shape)
    return pl.BlockSpec(shape, lambda *_: (0,) * nd, pipeline_mode=pl.Buffered(1))


def _params(sem):
    return pltpu.CompilerParams(dimension_semantics=sem, vmem_limit_bytes=VMEM_LIMIT)


def _fox_proj_kernel(h_ref, g_ref, wqT_ref, wk_ref, wvT_ref, wf_ref, bf_ref,
                     pk_ref, pqT_ref, tri_ref, qT_ref, k_ref, vT_ref, carry_ref):
    @pl.when(pl.program_id(0) == 0)
    def _():
        carry_ref[...] = jnp.zeros_like(carry_ref)

    xn = _rms(h_ref[...], g_ref[...]).astype(BF16)
    xnT = xn.T

    f = _dot(xn, wf_ref[...]) + bf_ref[...]
    logf = (jnp.minimum(f, 0.0) - jnp.log1p(jnp.exp(-jnp.abs(f)))) * LOG2E
    hi, mid, lo = _split3(logf)
    tri = tri_ref[...]
    cum = _dot(tri, hi) + _dot(tri, mid) + _dot(tri, lo) + carry_ref[...]
    carry_ref[...] = cum[TM - 1:TM, :]

    lane = lax.broadcasted_iota(jnp.int32, cum.shape, 1)
    c = jnp.where(lane < HEADS, cum, 1.0)
    c3 = jnp.concatenate(_split3(c), axis=1)
    c3T = jnp.concatenate(_split3(c.T), axis=0)

    kd = _dot(xn, wk_ref[...])
    kb = _dot(c3, pk_ref[...])
    low = lane < HEAD_DIM
    for g in range(HEADS // 2):
        kg = kd[:, g * LANES:(g + 1) * LANES]
        bg = kb[:, g * LANES:(g + 1) * LANES]
        k_ref[2 * g] = jnp.where(low, kg, bg).astype(BF16)
        k_ref[2 * g + 1] = jnp.where(low, bg, kg).astype(BF16)
    qd = (_dot(wqT_ref[...], xnT) * (LOG2E / math.sqrt(HEAD_DIM))).astype(BF16)
    qb = _dot(pqT_ref[...], c3T).astype(BF16)
    for h in range(HEADS):
        val = qd[h * HEAD_DIM:(h + 1) * HEAD_DIM]
        bias = qb[h * HEAD_DIM:(h + 1) * HEAD_DIM]
        first, second = (val, bias) if h % 2 == 0 else (bias, val)
        qT_ref[h * SLOT:h * SLOT + HEAD_DIM, :] = first
        qT_ref[h * SLOT + HEAD_DIM:(h + 1) * SLOT, :] = second
    vT_ref[...] = _dot(wvT_ref[...], xnT).astype(BF16)


def _fox_proj(h, g, wqT, wk, wvT, wf, bf, pk, pqT, tri):
    n = SEQ // TM
    return pl.pallas_call(
        _fox_proj_kernel,
        grid=(n,),
        in_specs=[
            pl.BlockSpec((TM, D_MODEL), lambda i: (i, 0)),
            _const_spec((1, D_MODEL)),
            _const_spec((D_MODEL, D_MODEL)),
            _const_spec((D_MODEL, D_MODEL)),
            _const_spec((D_MODEL, D_MODEL)),
            _const_spec((D_MODEL, LANES)),
            _const_spec((1, LANES)),
            _const_spec((3 * LANES, D_MODEL)),
            _const_spec((D_MODEL, 3 * LANES)),
            _const_spec((TM, TM)),
        ],
        out_specs=[
            pl.BlockSpec((AUG, TM), lambda i: (0, i)),
            pl.BlockSpec((HEADS, TM, SLOT), lambda i: (0, i, 0)),
            pl.BlockSpec((D_MODEL, TM), lambda i: (0, i)),
        ],
        out_shape=[
            jax.ShapeDtypeStruct((AUG, SEQ), BF16),
            jax.ShapeDtypeStruct((HEADS, SEQ, SLOT), BF16),
            jax.ShapeDtypeStruct((D_MODEL, SEQ), BF16),
        ],
        scratch_shapes=[pltpu.VMEM((1, LANES), F32)],
        compiler_params=_params(("arbitrary",)),
        name="fox_proj",
    )(h, g, wqT, wk, wvT, wf, bf, pk, pqT, tri)


def _attn_kernel(qT_ref, k_ref, vT_ref, o_ref, sa_scr, sb_scr, pa_scr, pb_scr,
                 smax_a_scr, smax_b_scr, alpha_b_scr, m_scr, acc_scr):
    _attn_tile(jnp.int32(0), True, qT_ref, k_ref, vT_ref, o_ref, sa_scr, sb_scr, pa_scr, pb_scr,
               smax_a_scr, smax_b_scr, alpha_b_scr, m_scr, acc_scr)

    def tile(i, carry):
        _attn_tile(i, False, qT_ref, k_ref, vT_ref, o_ref, sa_scr, sb_scr, pa_scr, pb_scr,
                   smax_a_scr, smax_b_scr, alpha_b_scr, m_scr, acc_scr)
        return carry

    lax.fori_loop(1, SEQ // TQ, tile, 0)


def _attn_tile(i, is_first, qT_ref, k_ref, vT_ref, o_ref, sa_scr, sb_scr, pa_scr, pb_scr,
               smax_a_scr, smax_b_scr, alpha_b_scr, m_scr, acc_scr):
    groups = CH // 8
    windows = tuple(slice(j * WQ, (j + 1) * WQ) for j in range(TQ // WQ))
    half = CH // WQ
    q0 = pl.multiple_of(i * TQ, TQ)
    q_next = pl.multiple_of(jnp.minimum(i + 1, SEQ // TQ - 1) * TQ, TQ)

    def scores(c, s_ref, smax_ref, cols, mask_off=None, qbase=q0):
        off = pl.multiple_of(c * CH, CH)
        q_cols = pl.ds(pl.multiple_of(qbase + cols.start, WQ), WQ)
        s = _dot(k_ref[pl.ds(off, CH), :], qT_ref[:, q_cols])
        if mask_off is not None:
            row = lax.broadcasted_iota(jnp.int32, (CH, WQ), 0)
            col = lax.broadcasted_iota(jnp.int32, (CH, WQ), 1)
            s = jnp.where(row <= col + mask_off, s, NEG)
        s_ref[:, cols] = s
        smax_ref[:, cols] = jnp.max(s.reshape(groups, 8, WQ), axis=0)

    def probs(s_ref, p_ref, smax_ref, cols):
        m_old = m_scr[:, cols]
        m_new = jnp.maximum(m_old, jnp.max(smax_ref[:, cols], axis=0, keepdims=True))
        alpha = jnp.exp2(m_old - m_new)
        m_scr[:, cols] = m_new
        p_ref[:, cols] = jnp.exp2(s_ref[:, cols] - m_new).astype(BF16)
        return alpha

    ones_rows = jnp.ones((ACC_ROWS - HEAD_DIM, CH), BF16)

    def pv(c, p_ref, alpha, cols):
        off = pl.multiple_of(c * CH, CH)
        lhs = jnp.concatenate([vT_ref[:, pl.ds(off, CH)], ones_rows], axis=0)
        acc_scr[:, cols] = alpha * acc_scr[:, cols] + _dot(lhs, p_ref[:, cols])

    def next_scores(t, nxt, chunk_b, j, cols):
        s_ref, smax_ref = (sb_scr, smax_b_scr) if chunk_b else (sa_scr, smax_a_scr)
        if nxt == 'plain':
            scores(2 * t + 2 + chunk_b, s_ref, smax_ref, cols)
        elif nxt == 'tile':
            scores(chunk_b, s_ref, smax_ref, cols, None, q_next)
        elif not chunk_b:
            scores(2 * t + 2, s_ref, smax_ref, cols, j * WQ if j < half else None)
        elif j >= half:
            scores(2 * t + 3, s_ref, smax_ref, cols, (j - half) * WQ)

    def pair(t, nxt, first=False, diag=False):
        for j, cols in enumerate(windows):
            alpha_a = probs(sa_scr, pa_scr, smax_a_scr, cols)
            next_scores(t, nxt, 0, j, cols)
            if not first:
                pv(jnp.maximum(2 * t - 1, 0), pb_scr, alpha_b_scr[:, cols], cols)
            if not (diag and j < half):
                alpha_b_scr[:, cols] = probs(sb_scr, pb_scr, smax_b_scr, cols)
            next_scores(t, nxt, 1, j, cols)
            pv(2 * t, pa_scr, alpha_a, cols)

    m_scr[...] = jnp.full(m_scr.shape, NEG, F32)
    acc_scr[...] = jnp.zeros(acc_scr.shape, F32)
    if is_first:
        for j, cols in enumerate(windows):
            scores(0, sa_scr, smax_a_scr, cols, j * WQ if j < half else None)
            if j >= half:
                scores(1, sb_scr, smax_b_scr, cols, (j - half) * WQ)
    else:
        pb_scr[...] = jnp.zeros(pb_scr.shape, BF16)
        alpha_b_scr[...] = jnp.ones(alpha_b_scr.shape, F32)

        def body(t, carry):
            pair(t, 'plain')
            return carry

        lax.fori_loop(0, i - 1, body, 0)
        pair(i - 1, 'diag')
    pair(i, 'tile', first=is_first, diag=True)
    for cols in windows[half:]:
        pv(2 * i + 1, pb_scr, alpha_b_scr[:, cols], cols)
    l = acc_scr[HEAD_DIM:HEAD_DIM + 1, :]
    o_ref[:, pl.ds(q0, TQ)] = (acc_scr[:HEAD_DIM, :] / l).astype(BF16)


def _attention(qT, k, vT):
    return pl.pallas_call(
        _attn_kernel,
        grid=(HEADS,),
        in_specs=[
            pl.BlockSpec((None, SLOT, SEQ), lambda h: (h, 0, 0)),
            pl.BlockSpec((None, SEQ, SLOT), lambda h: (h, 0, 0)),
            pl.BlockSpec((HEAD_DIM, SEQ), lambda h: (h, 0)),
        ],
        out_specs=pl.BlockSpec((HEAD_DIM, SEQ), lambda h: (h, 0)),
        out_shape=jax.ShapeDtypeStruct((HEADS * HEAD_DIM, SEQ), BF16),
        scratch_shapes=[
            pltpu.VMEM((CH, TQ), F32),
            pltpu.VMEM((CH, TQ), F32),
            pltpu.VMEM((CH, TQ), BF16),
            pltpu.VMEM((CH, TQ), BF16),
            pltpu.VMEM((8, TQ), F32),
            pltpu.VMEM((8, TQ), F32),
            pltpu.VMEM((1, TQ), F32),
            pltpu.VMEM((1, TQ), F32),
            pltpu.VMEM((ACC_ROWS, TQ), F32),
        ],
        compiler_params=_params(("arbitrary",)),
        name="attention",
    )(qT, k, vT)


def _out_ffn_kernel(oT_ref, h_ref, wo_ref, g_ref, wg_ref, wu_ref, wd_ref, gfin_ref,
                    out_ref, *, final):
    o = oT_ref[...].T
    h1 = h_ref[...] + _dot(o, wo_ref[...])
    xn = _rms(h1, g_ref[...]).astype(BF16)
    acc = h1
    for c in range(D_FF // FF_CHUNK):
        sl = slice(c * FF_CHUNK, (c + 1) * FF_CHUNK)
        gate = _dot(xn, wg_ref[:, sl])
        up = _dot(xn, wu_ref[:, sl])
        act = (gate * (1.0 / (1.0 + jnp.exp(-gate))) * up).astype(BF16)
        acc = acc + _dot(act, wd_ref[sl, :])
    if final:
        acc = _rms(acc, gfin_ref[...])
    out_ref[...] = acc


def _out_ffn(oT, h, wo, g, wg, wu, wd, gfin, final):
    n = SEQ // TM
    return pl.pallas_call(
        functools.partial(_out_ffn_kernel, final=final),
        grid=(n,),
        in_specs=[
            pl.BlockSpec((D_MODEL, TM), lambda i: (0, i)),
            pl.BlockSpec((TM, D_MODEL), lambda i: (i, 0)),
            _const_spec((D_MODEL, D_MODEL)),
            _const_spec((1, D_MODEL)),
            _const_spec((D_MODEL, D_FF)),
            _const_spec((D_MODEL, D_FF)),
            _const_spec((D_FF, D_MODEL)),
            _const_spec((1, D_MODEL)),
        ],
        out_specs=pl.BlockSpec((TM, D_MODEL), lambda i: (i, 0)),
        out_shape=jax.ShapeDtypeStruct((SEQ, D_MODEL), F32),
        compiler_params=_params(("arbitrary",)),
        name="out_ffn",
    )(oT, h, wo, g, wg, wu, wd, gfin)


def _rope_table_kernel(pos_ref, invf_ref, cos_ref, sin_ref, cosT_ref, sinT_ref):
    ang = pos_ref[...] * invf_ref[...]
    cos = jnp.cos(ang)
    sin = jnp.sin(ang)
    cos_ref[...] = cos
    sin_ref[...] = sin
    cosT_ref[...] = cos.T
    sinT_ref[...] = sin.T


def _rope_tables(pos_col, invf):
    n = SEQ // TM
    row = pl.BlockSpec((TM, LANES), lambda i: (i, 0))
    colT = pl.BlockSpec((LANES, TM), lambda i: (0, i))
    return pl.pallas_call(
        _rope_table_kernel,
        grid=(n,),
        in_specs=[pl.BlockSpec((TM, 1), lambda i: (i, 0)), _const_spec((1, LANES))],
        out_specs=[row, row, colT, colT],
        out_shape=[jax.ShapeDtypeStruct((SEQ, LANES), F32)] * 2
        + [jax.ShapeDtypeStruct((LANES, SEQ), F32)] * 2,
        compiler_params=_params(("arbitrary",)),
        name="rope_tables",
    )(pos_col, invf)


def _mla_kv_kernel(h_ref, g_ref, wa_ref, gckv_ref, cos_ref, sin_ref, wuk_ref,
                   p1_ref, p2_ref, wuvT_ref, k_ref, vT_ref):
    hn = _rms(h_ref[...], g_ref[...]).astype(BF16)
    a = _dot(hn, wa_ref[...])
    c = _rms(a[:, :KV_RANK], gckv_ref[...]).astype(BF16)
    x1 = a[:, KV_RANK:KV_RANK + LANES]
    x2 = a[:, KV_RANK + LANES:]
    cos = cos_ref[...]
    sin = sin_ref[...]
    r1 = (x1 * cos - x2 * sin).astype(BF16)
    r2 = (x1 * sin + x2 * cos).astype(BF16)
    k = _dot(c, wuk_ref[...]) + _dot(r1, p1_ref[...]) + _dot(r2, p2_ref[...])
    _store_heads(k_ref, k.astype(BF16))
    vT_ref[...] = _dot(wuvT_ref[...], c.T).astype(BF16)


def _mla_kv(h, g, wa, gckv, cos, sin, wuk, p1, p2, wuvT):
    n = SEQ // TM
    return pl.pallas_call(
        _mla_kv_kernel,
        grid=(n,),
        in_specs=[
            pl.BlockSpec((TM, D_MODEL), lambda i: (i, 0)),
            _const_spec((1, D_MODEL)),
            _const_spec((D_MODEL, 4 * LANES)),
            _const_spec((1, KV_RANK)),
            pl.BlockSpec((TM, LANES), lambda i: (i, 0)),
            pl.BlockSpec((TM, LANES), lambda i: (i, 0)),
            _const_spec((KV_RANK, AUG)),
            _const_spec((LANES, AUG)),
            _const_spec((LANES, AUG)),
            _const_spec((D_MODEL, KV_RANK)),
        ],
        out_specs=[
            pl.BlockSpec((HEADS, TM, SLOT), lambda i: (0, i, 0)),
            pl.BlockSpec((D_MODEL, TM), lambda i: (0, i)),
        ],
        out_shape=[
            jax.ShapeDtypeStruct((HEADS, SEQ, SLOT), BF16),
            jax.ShapeDtypeStruct((D_MODEL, SEQ), BF16),
        ],
        compiler_params=_params(("arbitrary",)),
        name="mla_kv",
    )(h, g, wa, gckv, cos, sin, wuk, p1, p2, wuvT)


def _mla_q_kernel(h_ref, g_ref, wdq_ref, gcq_ref, wnT_ref, wx1T_ref, wx2T_ref,
                  cosT_ref, sinT_ref, qT_ref):
    scale = LOG2E / math.sqrt(MLA_NOPE + MLA_ROPE)
    xn = _rms(h_ref[...], g_ref[...]).astype(BF16)
    cq = _rms(_dot(xn, wdq_ref[...]), gcq_ref[...]).astype(BF16)
    cqT = cq.T
    nT = _dot(wnT_ref[...], cqT) * scale
    x1T = _dot(wx1T_ref[...], cqT)
    x2T = _dot(wx2T_ref[...], cqT)
    cosT = jnp.concatenate([cosT_ref[...]] * HEADS, axis=0)
    sinT = jnp.concatenate([sinT_ref[...]] * HEADS, axis=0)
    r1 = (x1T * cosT - x2T * sinT) * scale
    r2 = (x1T * sinT + x2T * cosT) * scale
    zeros = jnp.zeros((SLOT - MLA_NOPE - MLA_ROPE, TM), BF16)
    for h in range(HEADS):
        qT_ref[h, 0:MLA_NOPE, :] = nT[h * MLA_NOPE:(h + 1) * MLA_NOPE].astype(BF16)
        qT_ref[h, MLA_NOPE:MLA_NOPE + HALF_ROPE, :] = (
            r1[h * HALF_ROPE:(h + 1) * HALF_ROPE].astype(BF16))
        qT_ref[h, MLA_NOPE + HALF_ROPE:MLA_NOPE + MLA_ROPE, :] = (
            r2[h * HALF_ROPE:(h + 1) * HALF_ROPE].astype(BF16))
        qT_ref[h, MLA_NOPE + MLA_ROPE:, :] = zeros


def _mla_q(h, g, wdq, gcq, wnT, wx1T, wx2T, cosT, sinT):
    n = SEQ // TM
    return pl.pallas_call(
        _mla_q_kernel,
        grid=(n,),
        in_specs=[
            pl.BlockSpec((TM, D_MODEL), lambda i: (i, 0)),
            _const_spec((1, D_MODEL)),
            _const_spec((D_MODEL, Q_RANK)),
            _const_spec((1, Q_RANK)),
            _const_spec((HEADS * MLA_NOPE, Q_RANK)),
            _const_spec((HEADS * HALF_ROPE, Q_RANK)),
            _const_spec((HEADS * HALF_ROPE, Q_RANK)),
            pl.BlockSpec((HALF_ROPE, TM), lambda i: (0, i)),
            pl.BlockSpec((HALF_ROPE, TM), lambda i: (0, i)),
        ],
        out_specs=pl.BlockSpec((HEADS, SLOT, TM), lambda i: (0, 0, i)),
        out_shape=jax.ShapeDtypeStruct((HEADS, SLOT, SEQ), BF16),
        compiler_params=_params(("arbitrary",)),
        name="mla_q",
    )(h, g, wdq, gcq, wnT, wx1T, wx2T, cosT, sinT)


def _pad_heads_cols(w, width):
    k = w.shape[0]
    w = w.reshape(k, HEADS, width)
    w = jnp.pad(w, ((0, 0), (0, 0), (0, SLOT - width)))
    return w.reshape(k, AUG)


def _fox_bias_placement():
    w = HEADS * HEAD_DIM
    pk = np.zeros((3 * LANES, w), np.float32)
    for h in range(HEADS):
        base = (h // 2) * LANES + (HEAD_DIM if h % 2 == 0 else 0)
        for j in range(3):
            pk[HEADS, base + j] = 1.0
            pk[j * LANES + h, base + 3 + j] = -1.0
    pq = np.zeros((w, 3 * LANES), np.float32)
    for h in range(HEADS):
        for j in range(3):
            pq[h * HEAD_DIM + j, j * LANES + h] = 1.0
            pq[h * HEAD_DIM + 3 + j, HEADS] = 1.0
    return jnp.asarray(pk, BF16), jnp.asarray(pq, BF16)


def _rope_placement():
    p1 = np.zeros((LANES, AUG), np.float32)
    p2 = np.zeros((LANES, AUG), np.float32)
    for h in range(HEADS):
        for j in range(HALF_ROPE):
            p1[j, h * SLOT + MLA_NOPE + j] = 1.0
            p2[j, h * SLOT + MLA_NOPE + HALF_ROPE + j] = 1.0
    return jnp.asarray(p1, BF16), jnp.asarray(p2, BF16)


def kernel(x, positions, attn_norm, ffn_norm, w_gate, w_up, w_down, fox_w_in, fox_b_f,
           fox_w_o, kv_norm, w_kv_a, ckv_norm, w_uk, w_uv, mla_w_dq, cq_norm, mla_w_uq,
           mla_w_o, final_norm):
    assert x.shape == (1, SEQ, D_MODEL)
    h = x.reshape(SEQ, D_MODEL)
    W = HEADS * HEAD_DIM
    pk, pqT = _fox_bias_placement()
    p1, p2 = _rope_placement()
    tri = jnp.asarray(np.tril(np.ones((TM, TM), np.float32)), BF16)
    row = lambda v: v.reshape(1, -1)

    def ffn_args(l):
        return (row(ffn_norm[l]), w_gate[l].astype(BF16), w_up[l].astype(BF16),
                w_down[l].astype(BF16), row(final_norm))

    for l in range(N_A):
        w_in = fox_w_in[l]
        wqT = w_in[:, :W].T.astype(BF16)
        wk = w_in[:, W:2 * W].astype(BF16)
        wvT = w_in[:, 2 * W:3 * W].T.astype(BF16)
        wf = jnp.pad(w_in[:, 3 * W:], ((0, 0), (0, LANES - HEADS))).astype(BF16)
        bf = jnp.pad(fox_b_f[l], (0, LANES - HEADS)).reshape(1, LANES)
        qT, k, vT = _fox_proj(h, row(attn_norm[l]), wqT, wk, wvT, wf, bf, pk, pqT, tri)
        oT = _attention(qT.reshape(HEADS, SLOT, SEQ), k, vT)
        h = _out_ffn(oT, h, fox_w_o[l].astype(BF16), *ffn_args(l), final=False)

    invf = ROPE_THETA ** (-jnp.arange(0, HALF_ROPE, dtype=F32) * 2.0 / MLA_ROPE)
    invf = jnp.pad(invf, (0, LANES - HALF_ROPE)).reshape(1, LANES)
    pos_col = positions.reshape(SEQ, 1).astype(F32)
    cos, sin, cosT, sinT = _rope_tables(pos_col, invf)
    wa = jnp.concatenate([
        w_kv_a[:, :KV_RANK],
        jnp.pad(w_kv_a[:, KV_RANK:KV_RANK + HALF_ROPE], ((0, 0), (0, LANES - HALF_ROPE))),
        jnp.pad(w_kv_a[:, KV_RANK + HALF_ROPE:], ((0, 0), (0, LANES - HALF_ROPE))),
    ], axis=1).astype(BF16)
    wuk = _pad_heads_cols(w_uk.reshape(KV_RANK, HEADS * MLA_NOPE), MLA_NOPE).astype(BF16)
    wuvT = w_uv.reshape(KV_RANK, HEADS * HEAD_DIM).T.astype(BF16)
    k_sh, vT_sh = _mla_kv(h, row(kv_norm), wa, row(ckv_norm), cos, sin, wuk, p1, p2, wuvT)

    for l in range(N_A, DEPTH):
        j = l - N_A
        wuq = mla_w_uq[j].reshape(Q_RANK, HEADS, MLA_NOPE + MLA_ROPE)
        wnT = wuq[:, :, :MLA_NOPE].reshape(Q_RANK, -1).T.astype(BF16)
        wx1T = wuq[:, :, MLA_NOPE:MLA_NOPE + HALF_ROPE].reshape(Q_RANK, -1).T.astype(BF16)
        wx2T = wuq[:, :, MLA_NOPE + HALF_ROPE:].reshape(Q_RANK, -1).T.astype(BF16)
        qT = _mla_q(h, row(attn_norm[l]), mla_w_dq[j].astype(BF16), row(cq_norm[j]),
                    wnT, wx1T, wx2T, cosT, sinT)
        oT = _attention(qT, k_sh, vT_sh)
        h = _out_ffn(oT, h, mla_w_o[j].astype(BF16), *ffn_args(l), final=(l == DEPTH - 1))

    return h.reshape(1, SEQ, D_MODEL)
```

```python
import functools
import math

import numpy as np
import jax
import jax.numpy as jnp
from jax import lax
from jax.experimental import pallas as pl
from jax.experimental.pallas import tpu as pltpu

F32 = jnp.float32
BF16 = jnp.bfloat16

D_MODEL = 1024
SEQ = 16384
DEPTH = 4
N_A = DEPTH // 2
RMS_EPS = 1e-6
HEADS = 16
HEAD_DIM = 64
MLA_NOPE = 64
MLA_ROPE = 32
HALF_ROPE = MLA_ROPE // 2
KV_RANK = 256
Q_RANK = 768
ROPE_THETA = 10000.0
D_FF = 2816

LANES = 128
SLOT = 128
AUG = HEADS * SLOT
LOG2E = math.log2(math.e)
NEG = -1e30

TM = 512
TQ = 1024
CH = TQ // 2
WQ = 256
ACC_ROWS = HEAD_DIM + 16
FF_CHUNK = 256
VMEM_LIMIT = 56 * 1024 * 1024


def _dot(a, b):
    return jnp.dot(a, b, preferred_element_type=F32)


def _rms(x, g):
    inv = lax.rsqrt(jnp.mean(x * x, axis=-1, keepdims=True) + RMS_EPS)
    return (x * inv) * g


def _split3(x):
    hi = x.astype(BF16)
    r = x - hi.astype(F32)
    mid = r.astype(BF16)
    lo = (r - mid.astype(F32)).astype(BF16)
    return hi, mid, lo


def _store_heads(k_ref, k):
    for h in range(HEADS):
        k_ref[h] = k[:, h * SLOT:(h + 1) * SLOT]


def _const_spec(shape):
    nd = len(shape)
    return pl.BlockSpec(shape, lambda *_: (0,) * nd, pipeline_mode=pl.Buffered(1))


def _params(sem):
    return pltpu.CompilerParams(dimension_semantics=sem, vmem_limit_bytes=VMEM_LIMIT)


def _fox_proj_kernel(h_ref, g_ref, wqT_ref, wk_ref, wvT_ref, wf_ref, bf_ref,
                     pk_ref, pqT_ref, tri_ref, qT_ref, k_ref, vT_ref, carry_ref):
    @pl.when(pl.program_id(0) == 0)
    def _():
        carry_ref[...] = jnp.zeros_like(carry_ref)

    xn = _rms(h_ref[...], g_ref[...]).astype(BF16)
    xnT = xn.T

    f = _dot(xn, wf_ref[...]) + bf_ref[...]
    logf = (jnp.minimum(f, 0.0) - jnp.log1p(jnp.exp(-jnp.abs(f)))) * LOG2E
    hi, mid, lo = _split3(logf)
    tri = tri_ref[...]
    cum = _dot(tri, hi) + _dot(tri, mid) + _dot(tri, lo) + carry_ref[...]
    carry_ref[...] = cum[TM - 1:TM, :]

    lane = lax.broadcasted_iota(jnp.int32, cum.shape, 1)
    c = jnp.where(lane < HEADS, cum, 1.0)
    c3 = jnp.concatenate(_split3(c), axis=1)
    c3T = jnp.concatenate(_split3(c.T), axis=0)

    kd = _dot(xn, wk_ref[...])
    kb = _dot(c3, pk_ref[...])
    low = lane < HEAD_DIM
    for g in range(HEADS // 2):
        kg = kd[:, g * LANES:(g + 1) * LANES]
        bg = kb[:, g * LANES:(g + 1) * LANES]
        k_ref[2 * g] = jnp.where(low, kg, bg).astype(BF16)
        k_ref[2 * g + 1] = jnp.where(low, bg, kg).astype(BF16)
    qd = (_dot(wqT_ref[...], xnT) * (LOG2E / math.sqrt(HEAD_DIM))).astype(BF16)
    qb = _dot(pqT_ref[...], c3T).astype(BF16)
    for h in range(HEADS):
        val = qd[h * HEAD_DIM:(h + 1) * HEAD_DIM]
        bias = qb[h * HEAD_DIM:(h + 1) * HEAD_DIM]
        first, second = (val, bias) if h % 2 == 0 else (bias, val)
        qT_ref[h * SLOT:h * SLOT + HEAD_DIM, :] = first
        qT_ref[h * SLOT + HEAD_DIM:(h + 1) * SLOT, :] = second
    vT_ref[...] = _dot(wvT_ref[...], xnT).astype(BF16)


def _fox_proj(h, g, wqT, wk, wvT, wf, bf, pk, pqT, tri):
    n = SEQ // TM
    return pl.pallas_call(
        _fox_proj_kernel,
        grid=(n,),
        in_specs=[
            pl.BlockSpec((TM, D_MODEL), lambda i: (i, 0)),
            _const_spec((1, D_MODEL)),
            _const_spec((D_MODEL, D_MODEL)),
            _const_spec((D_MODEL, D_MODEL)),
            _const_spec((D_MODEL, D_MODEL)),
            _const_spec((D_MODEL, LANES)),
            _const_spec((1, LANES)),
            _const_spec((3 * LANES, D_MODEL)),
            _const_spec((D_MODEL, 3 * LANES)),
            _const_spec((TM, TM)),
        ],
        out_specs=[
            pl.BlockSpec((AUG, TM), lambda i: (0, i)),
            pl.BlockSpec((HEADS, TM, SLOT), lambda i: (0, i, 0)),
            pl.BlockSpec((D_MODEL, TM), lambda i: (0, i)),
        ],
        out_shape=[
            jax.ShapeDtypeStruct((AUG, SEQ), BF16),
            jax.ShapeDtypeStruct((HEADS, SEQ, SLOT), BF16),
            jax.ShapeDtypeStruct((D_MODEL, SEQ), BF16),
        ],
        scratch_shapes=[pltpu.VMEM((1, LANES), F32)],
        compiler_params=_params(("arbitrary",)),
        name="fox_proj",
    )(h, g, wqT, wk, wvT, wf, bf, pk, pqT, tri)


def _attn_kernel(qT_ref, k_ref, vT_ref, o_ref, sa_scr, sb_scr, pa_scr, pb_scr,
                 smax_a_scr, smax_b_scr, alpha_b_scr, m_scr, acc_scr):
    def tile(i, carry):
        _attn_tile(i, qT_ref, k_ref, vT_ref, o_ref, sa_scr, sb_scr, pa_scr, pb_scr,
                   smax_a_scr, smax_b_scr, alpha_b_scr, m_scr, acc_scr)
        return carry

    lax.fori_loop(0, SEQ // TQ, tile, 0)


def _attn_tile(i, qT_ref, k_ref, vT_ref, o_ref, sa_scr, sb_scr, pa_scr, pb_scr,
               smax_a_scr, smax_b_scr, alpha_b_scr, m_scr, acc_scr):
    groups = CH // 8
    windows = tuple(slice(j * WQ, (j + 1) * WQ) for j in range(TQ // WQ))
    half = CH // WQ
    q0 = pl.multiple_of(i * TQ, TQ)
    q_next = pl.multiple_of(jnp.minimum(i + 1, SEQ // TQ - 1) * TQ, TQ)

    def scores(c, s_ref, smax_ref, cols, mask_off=None, qbase=q0):
        off = pl.multiple_of(c * CH, CH)
        q_cols = pl.ds(pl.multiple_of(qbase + cols.start, WQ), WQ)
        s = _dot(k_ref[pl.ds(off, CH), :], qT_ref[:, q_cols])
        if mask_off is not None:
            row = lax.broadcasted_iota(jnp.int32, (CH, WQ), 0)
            col = lax.broadcasted_iota(jnp.int32, (CH, WQ), 1)
            s = jnp.where(row <= col + mask_off, s, NEG)
        s_ref[:, cols] = s
        smax_ref[:, cols] = jnp.max(s.reshape(groups, 8, WQ), axis=0)

    def probs(s_ref, p_ref, smax_ref, cols):
        m_old = m_scr[:, cols]
        m_new = jnp.maximum(m_old, jnp.max(smax_ref[:, cols], axis=0, keepdims=True))
        alpha = jnp.exp2(m_old - m_new)
        m_scr[:, cols] = m_new
        p_ref[:, cols] = jnp.exp2(s_ref[:, cols] - m_new).astype(BF16)
        return alpha

    ones_rows = jnp.ones((ACC_ROWS - HEAD_DIM, CH), BF16)

    def pv(c, p_ref, alpha, cols):
        off = pl.multiple_of(c * CH, CH)
        lhs = jnp.concatenate([vT_ref[:, pl.ds(off, CH)], ones_rows], axis=0)
        acc_scr[:, cols] = alpha * acc_scr[:, cols] + _dot(lhs, p_ref[:, cols])

    def next_scores(t, nxt, chunk_b, j, cols):
        s_ref, smax_ref = (sb_scr, smax_b_scr) if chunk_b else (sa_scr, smax_a_scr)
        if nxt == 'plain':
            scores(2 * t + 2 + chunk_b, s_ref, smax_ref, cols)
        elif nxt == 'tile':
            scores(chunk_b, s_ref, smax_ref, cols, None, q_next)
        elif not chunk_b:
            scores(2 * t + 2, s_ref, smax_ref, cols, j * WQ if j < half else None)
        elif j >= half:
            scores(2 * t + 3, s_ref, smax_ref, cols, (j - half) * WQ)

    def pair(t, nxt, first=False, diag=False):
        for j, cols in enumerate(windows):
            alpha_a = probs(sa_scr, pa_scr, smax_a_scr, cols)
            next_scores(t, nxt, 0, j, cols)
            if not first:
                pv(jnp.maximum(2 * t - 1, 0), pb_scr, alpha_b_scr[:, cols], cols)
            if not (diag and j < half):
                alpha_b_scr[:, cols] = probs(sb_scr, pb_scr, smax_b_scr, cols)
            next_scores(t, nxt, 1, j, cols)
            pv(2 * t, pa_scr, alpha_a, cols)

    m_scr[...] = jnp.full(m_scr.shape, NEG, F32)
    acc_scr[...] = jnp.zeros(acc_scr.shape, F32)
    pb_scr[...] = jnp.zeros(pb_scr.shape, BF16)
    alpha_b_scr[...] = jnp.ones(alpha_b_scr.shape, F32)

    @pl.when(i == 0)
    def _():
        for j, cols in enumerate(windows):
            scores(0, sa_scr, smax_a_scr, cols, j * WQ if j < half else None)
            if j >= half:
                scores(1, sb_scr, smax_b_scr, cols, (j - half) * WQ)

    def body(t, carry):
        pair(t, 'plain')
        return carry

    lax.fori_loop(0, i - 1, body, 0)

    @pl.when(i > 0)
    def _():
        pair(i - 1, 'diag')

    pair(i, 'tile', diag=True)
    for cols in windows[half:]:
        pv(2 * i + 1, pb_scr, alpha_b_scr[:, cols], cols)
    l = acc_scr[HEAD_DIM:HEAD_DIM + 1, :]
    o_ref[:, pl.ds(q0, TQ)] = (acc_scr[:HEAD_DIM, :] / l).astype(BF16)


def _attention(qT, k, vT):
    return pl.pallas_call(
        _attn_kernel,
        grid=(HEADS,),
        in_specs=[
            pl.BlockSpec((None, SLOT, SEQ), lambda h: (h, 0, 0)),
            pl.BlockSpec((None, SEQ, SLOT), lambda h: (h, 0, 0)),
            pl.BlockSpec((HEAD_DIM, SEQ), lambda h: (h, 0)),
        ],
        out_specs=pl.BlockSpec((HEAD_DIM, SEQ), lambda h: (h, 0)),
        out_shape=jax.ShapeDtypeStruct((HEADS * HEAD_DIM, SEQ), BF16),
        scratch_shapes=[
            pltpu.VMEM((CH, TQ), F32),
            pltpu.VMEM((CH, TQ), F32),
            pltpu.VMEM((CH, TQ), BF16),
            pltpu.VMEM((CH, TQ), BF16),
            pltpu.VMEM((8, TQ), F32),
            pltpu.VMEM((8, TQ), F32),
            pltpu.VMEM((1, TQ), F32),
            pltpu.VMEM((1, TQ), F32),
            pltpu.VMEM((ACC_ROWS, TQ), F32),
        ],
        compiler_params=_params(("arbitrary",)),
        name="attention",
    )(qT, k, vT)


def _out_ffn_kernel(oT_ref, h_ref, wo_ref, g_ref, wg_ref, wu_ref, wd_ref, gfin_ref,
                    out_ref, *, final):
    o = oT_ref[...].T
    h1 = h_ref[...] + _dot(o, wo_ref[...])
    xn = _rms(h1, g_ref[...]).astype(BF16)
    acc = h1
    for c in range(D_FF // FF_CHUNK):
        sl = slice(c * FF_CHUNK, (c + 1) * FF_CHUNK)
        gate = _dot(xn, wg_ref[:, sl])
        up = _dot(xn, wu_ref[:, sl])
        act = (gate * (1.0 / (1.0 + jnp.exp(-gate))) * up).astype(BF16)
        acc = acc + _dot(act, wd_ref[sl, :])
    if final:
        acc = _rms(acc, gfin_ref[...])
    out_ref[...] = acc


def _out_ffn(oT, h, wo, g, wg, wu, wd, gfin, final):
    n = SEQ // TM
    return pl.pallas_call(
        functools.partial(_out_ffn_kernel, final=final),
        grid=(n,),
        in_specs=[
            pl.BlockSpec((D_MODEL, TM), lambda i: (0, i)),
            pl.BlockSpec((TM, D_MODEL), lambda i: (i, 0)),
            _const_spec((D_MODEL, D_MODEL)),
            _const_spec((1, D_MODEL)),
            _const_spec((D_MODEL, D_FF)),
            _const_spec((D_MODEL, D_FF)),
            _const_spec((D_FF, D_MODEL)),
            _const_spec((1, D_MODEL)),
        ],
        out_specs=pl.BlockSpec((TM, D_MODEL), lambda i: (i, 0)),
        out_shape=jax.ShapeDtypeStruct((SEQ, D_MODEL), F32),
        compiler_params=_params(("arbitrary",)),
        name="out_ffn",
    )(oT, h, wo, g, wg, wu, wd, gfin)


def _rope_table_kernel(pos_ref, invf_ref, cos_ref, sin_ref, cosT_ref, sinT_ref):
    ang = pos_ref[...] * invf_ref[...]
    cos = jnp.cos(ang)
    sin = jnp.sin(ang)
    cos_ref[...] = cos
    sin_ref[...] = sin
    cosT_ref[...] = cos.T
    sinT_ref[...] = sin.T


def _rope_tables(pos_col, invf):
    n = SEQ // TM
    row = pl.BlockSpec((TM, LANES), lambda i: (i, 0))
    colT = pl.BlockSpec((LANES, TM), lambda i: (0, i))
    return pl.pallas_call(
        _rope_table_kernel,
        grid=(n,),
        in_specs=[pl.BlockSpec((TM, 1), lambda i: (i, 0)), _const_spec((1, LANES))],
        out_specs=[row, row, colT, colT],
        out_shape=[jax.ShapeDtypeStruct((SEQ, LANES), F32)] * 2
        + [jax.ShapeDtypeStruct((LANES, SEQ), F32)] * 2,
        compiler_params=_params(("arbitrary",)),
        name="rope_tables",
    )(pos_col, invf)


def _mla_kv_kernel(h_ref, g_ref, wa_ref, gckv_ref, cos_ref, sin_ref, wuk_ref,
                   p1_ref, p2_ref, wuvT_ref, k_ref, vT_ref):
    hn = _rms(h_ref[...], g_ref[...]).astype(BF16)
    a = _dot(hn, wa_ref[...])
    c = _rms(a[:, :KV_RANK], gckv_ref[...]).astype(BF16)
    x1 = a[:, KV_RANK:KV_RANK + LANES]
    x2 = a[:, KV_RANK + LANES:]
    cos = cos_ref[...]
    sin = sin_ref[...]
    r1 = (x1 * cos - x2 * sin).astype(BF16)
    r2 = (x1 * sin + x2 * cos).astype(BF16)
    k = _dot(c, wuk_ref[...]) + _dot(r1, p1_ref[...]) + _dot(r2, p2_ref[...])
    _store_heads(k_ref, k.astype(BF16))
    vT_ref[...] = _dot(wuvT_ref[...], c.T).astype(BF16)


def _mla_kv(h, g, wa, gckv, cos, sin, wuk, p1, p2, wuvT):
    n = SEQ // TM
    return pl.pallas_call(
        _mla_kv_kernel,
        grid=(n,),
        in_specs=[
            pl.BlockSpec((TM, D_MODEL), lambda i: (i, 0)),
            _const_spec((1, D_MODEL)),
            _const_spec((D_MODEL, 4 * LANES)),
            _const_spec((1, KV_RANK)),
            pl.BlockSpec((TM, LANES), lambda i: (i, 0)),
            pl.BlockSpec((TM, LANES), lambda i: (i, 0)),
            _const_spec((KV_RANK, AUG)),
            _const_spec((LANES, AUG)),
            _const_spec((LANES, AUG)),
            _const_spec((D_MODEL, KV_RANK)),
        ],
        out_specs=[
            pl.BlockSpec((HEADS, TM, SLOT), lambda i: (0, i, 0)),
            pl.BlockSpec((D_MODEL, TM), lambda i: (0, i)),
        ],
        out_shape=[
            jax.ShapeDtypeStruct((HEADS, SEQ, SLOT), BF16),
            jax.ShapeDtypeStruct((D_MODEL, SEQ), BF16),
        ],
        compiler_params=_params(("arbitrary",)),
        name="mla_kv",
    )(h, g, wa, gckv, cos, sin, wuk, p1, p2, wuvT)


def _mla_q_kernel(h_ref, g_ref, wdq_ref, gcq_ref, wnT_ref, wx1T_ref, wx2T_ref,
                  cosT_ref, sinT_ref, qT_ref):
    scale = LOG2E / math.sqrt(MLA_NOPE + MLA_ROPE)
    xn = _rms(h_ref[...], g_ref[...]).astype(BF16)
    cq = _rms(_dot(xn, wdq_ref[...]), gcq_ref[...]).astype(BF16)
    cqT = cq.T
    nT = _dot(wnT_ref[...], cqT) * scale
    x1T = _dot(wx1T_ref[...], cqT)
    x2T = _dot(wx2T_ref[...], cqT)
    cosT = jnp.concatenate([cosT_ref[...]] * HEADS, axis=0)
    sinT = jnp.concatenate([sinT_ref[...]] * HEADS, axis=0)
    r1 = (x1T * cosT - x2T * sinT) * scale
    r2 = (x1T * sinT + x2T * cosT) * scale
    zeros = jnp.zeros((SLOT - MLA_NOPE - MLA_ROPE, TM), BF16)
    for h in range(HEADS):
        qT_ref[h, 0:MLA_NOPE, :] = nT[h * MLA_NOPE:(h + 1) * MLA_NOPE].astype(BF16)
        qT_ref[h, MLA_NOPE:MLA_NOPE + HALF_ROPE, :] = (
            r1[h * HALF_ROPE:(h + 1) * HALF_ROPE].astype(BF16))
        qT_ref[h, MLA_NOPE + HALF_ROPE:MLA_NOPE + MLA_ROPE, :] = (
            r2[h * HALF_ROPE:(h + 1) * HALF_ROPE].astype(BF16))
        qT_ref[h, MLA_NOPE + MLA_ROPE:, :] = zeros


def _mla_q(h, g, wdq, gcq, wnT, wx1T, wx2T, cosT, sinT):
    n = SEQ // TM
    return pl.pallas_call(
        _mla_q_kernel,
        grid=(n,),
        in_specs=[
            pl.BlockSpec((TM, D_MODEL), lambda i: (i, 0)),
            _const_spec((1, D_MODEL)),
            _const_spec((D_MODEL, Q_RANK)),
            _const_spec((1, Q_RANK)),
            _const_spec((HEADS * MLA_NOPE, Q_RANK)),
            _const_spec((HEADS * HALF_ROPE, Q_RANK)),
            _const_spec((HEADS * HALF_ROPE, Q_RANK)),
            pl.BlockSpec((HALF_ROPE, TM), lambda i: (0, i)),
            pl.BlockSpec((HALF_ROPE, TM), lambda i: (0, i)),
        ],
        out_specs=pl.BlockSpec((HEADS, SLOT, TM), lambda i: (0, 0, i)),
        out_shape=jax.ShapeDtypeStruct((HEADS, SLOT, SEQ), BF16),
        compiler_params=_params(("arbitrary",)),
        name="mla_q",
    )(h, g, wdq, gcq, wnT, wx1T, wx2T, cosT, sinT)


def _pad_heads_cols(w, width):
    k = w.shape[0]
    w = w.reshape(k, HEADS, width)
    w = jnp.pad(w, ((0, 0), (0, 0), (0, SLOT - width)))
    return w.reshape(k, AUG)


def _fox_bias_placement():
    w = HEADS * HEAD_DIM
    pk = np.zeros((3 * LANES, w), np.float32)
    for h in range(HEADS):
        base = (h // 2) * LANES + (HEAD_DIM if h % 2 == 0 else 0)
        for j in range(3):
            pk[HEADS, base + j] = 1.0
            pk[j * LANES + h, base + 3 + j] = -1.0
    pq = np.zeros((w, 3 * LANES), np.float32)
    for h in range(HEADS):
        for j in range(3):
            pq[h * HEAD_DIM + j, j * LANES + h] = 1.0
            pq[h * HEAD_DIM + 3 + j, HEADS] = 1.0
    return jnp.asarray(pk, BF16), jnp.asarray(pq, BF16)


def _rope_placement():
    p1 = np.zeros((LANES, AUG), np.float32)
    p2 = np.zeros((LANES, AUG), np.float32)
    for h in range(HEADS):
        for j in range(HALF_ROPE):
            p1[j, h * SLOT + MLA_NOPE + j] = 1.0
            p2[j, h * SLOT + MLA_NOPE + HALF_ROPE + j] = 1.0
    return jnp.asarray(p1, BF16), jnp.asarray(p2, BF16)


def kernel(x, positions, attn_norm, ffn_norm, w_gate, w_up, w_down, fox_w_in, fox_b_f,
           fox_w_o, kv_norm, w_kv_a, ckv_norm, w_uk, w_uv, mla_w_dq, cq_norm, mla_w_uq,
           mla_w_o, final_norm):
    assert x.shape == (1, SEQ, D_MODEL)
    h = x.reshape(SEQ, D_MODEL)
    W = HEADS * HEAD_DIM
    pk, pqT = _fox_bias_placement()
    p1, p2 = _rope_placement()
    tri = jnp.asarray(np.tril(np.ones((TM, TM), np.float32)), BF16)
    row = lambda v: v.reshape(1, -1)

    def ffn_args(l):
        return (row(ffn_norm[l]), w_gate[l].astype(BF16), w_up[l].astype(BF16),
                w_down[l].astype(BF16), row(final_norm))

    for l in range(N_A):
        w_in = fox_w_in[l]
        wqT = w_in[:, :W].T.astype(BF16)
        wk = w_in[:, W:2 * W].astype(BF16)
        wvT = w_in[:, 2 * W:3 * W].T.astype(BF16)
        wf = jnp.pad(w_in[:, 3 * W:], ((0, 0), (0, LANES - HEADS))).astype(BF16)
        bf = jnp.pad(fox_b_f[l], (0, LANES - HEADS)).reshape(1, LANES)
        qT, k, vT = _fox_proj(h, row(attn_norm[l]), wqT, wk, wvT, wf, bf, pk, pqT, tri)
        oT = _attention(qT.reshape(HEADS, SLOT, SEQ), k, vT)
        h = _out_ffn(oT, h, fox_w_o[l].astype(BF16), *ffn_args(l), final=False)

    invf = ROPE_THETA ** (-jnp.arange(0, HALF_ROPE, dtype=F32) * 2.0 / MLA_ROPE)
    invf = jnp.pad(invf, (0, LANES - HALF_ROPE)).reshape(1, LANES)
    pos_col = positions.reshape(SEQ, 1).astype(F32)
    cos, sin, cosT, sinT = _rope_tables(pos_col, invf)
    wa = jnp.concatenate([
        w_kv_a[:, :KV_RANK],
        jnp.pad(w_kv_a[:, KV_RANK:KV_RANK + HALF_ROPE], ((0, 0), (0, LANES - HALF_ROPE))),
        jnp.pad(w_kv_a[:, KV_RANK + HALF_ROPE:], ((0, 0), (0, LANES - HALF_ROPE))),
    ], axis=1).astype(BF16)
    wuk = _pad_heads_cols(w_uk.reshape(KV_RANK, HEADS * MLA_NOPE), MLA_NOPE).astype(BF16)
    wuvT = w_uv.reshape(KV_RANK, HEADS * HEAD_DIM).T.astype(BF16)
    k_sh, vT_sh = _mla_kv(h, row(kv_norm), wa, row(ckv_norm), cos, sin, wuk, p1, p2, wuvT)

    for l in range(N_A, DEPTH):
        j = l - N_A
        wuq = mla_w_uq[j].reshape(Q_RANK, HEADS, MLA_NOPE + MLA_ROPE)
        wnT = wuq[:, :, :MLA_NOPE].reshape(Q_RANK, -1).T.astype(BF16)
        wx1T = wuq[:, :, MLA_NOPE:MLA_NOPE + HALF_ROPE].reshape(Q_RANK, -1).T.astype(BF16)
        wx2T = wuq[:, :, MLA_NOPE + HALF_ROPE:].reshape(Q_RANK, -1).T.astype(BF16)
        qT = _mla_q(h, row(attn_norm[l]), mla_w_dq[j].astype(BF16), row(cq_norm[j]),
                    wnT, wx1T, wx2T, cosT, sinT)
        oT = _attention(qT, k_sh, vT_sh)
        h = _out_ffn(oT, h, mla_w_o[j].astype(BF16), *ffn_args(l), final=(l == DEPTH - 1))

    return h.reshape(1, SEQ, D_MODEL)
```

```python
import functools
import math

import numpy as np
import jax
import jax.numpy as jnp
from jax import lax
from jax.experimental import pallas as pl
from jax.experimental.pallas import tpu as pltpu

F32 = jnp.float32
BF16 = jnp.bfloat16

D_MODEL = 1024
SEQ = 16384
DEPTH = 4
N_A = DEPTH // 2
RMS_EPS = 1e-6
HEADS = 16
HEAD_DIM = 64
MLA_NOPE = 64
MLA_ROPE = 32
HALF_ROPE = MLA_ROPE // 2
KV_RANK = 256
Q_RANK = 768
ROPE_THETA = 10000.0
D_FF = 2816

LANES = 128
SLOT = 128
AUG = HEADS * SLOT
LOG2E = math.log2(math.e)
NEG = -1e30

TM = 512
TQ = 1024
CH = TQ // 2
WQ = 256
ACC_ROWS = HEAD_DIM + 16
FF_CHUNK = 256
VMEM_LIMIT = 56 * 1024 * 1024


def _dot(a, b):
    return jnp.dot(a, b, preferred_element_type=F32)


def _rms(x, g):
    inv = lax.rsqrt(jnp.mean(x * x, axis=-1, keepdims=True) + RMS_EPS)
    return (x * inv) * g


def _split3(x):
    hi = x.astype(BF16)
    r = x - hi.astype(F32)
    mid = r.astype(BF16)
    lo = (r - mid.astype(F32)).astype(BF16)
    return hi, mid, lo


def _store_heads(k_ref, k):
    for h in range(HEADS):
        k_ref[h] = k[:, h * SLOT:(h + 1) * SLOT]


def _const_spec(shape):
    nd = len(shape)
    return pl.BlockSpec(shape, lambda *_: (0,) * nd, pipeline_mode=pl.Buffered(1))


def _params(sem):
    return pltpu.CompilerParams(dimension_semantics=sem, vmem_limit_bytes=VMEM_LIMIT)


def _fox_proj_kernel(h_ref, g_ref, wqT_ref, wk_ref, wvT_ref, wf_ref, bf_ref,
                     pk_ref, pqT_ref, tri_ref, qT_ref, k_ref, vT_ref, carry_ref):
    @pl.when(pl.program_id(0) == 0)
    def _():
        carry_ref[...] = jnp.zeros_like(carry_ref)

    xn = _rms(h_ref[...], g_ref[...]).astype(BF16)
    xnT = xn.T

    f = _dot(xn, wf_ref[...]) + bf_ref[...]
    logf = (jnp.minimum(f, 0.0) - jnp.log1p(jnp.exp(-jnp.abs(f)))) * LOG2E
    hi, mid, lo = _split3(logf)
    tri = tri_ref[...]
    cum = _dot(tri, hi) + _dot(tri, mid) + _dot(tri, lo) + carry_ref[...]
    carry_ref[...] = cum[TM - 1:TM, :]

    lane = lax.broadcasted_iota(jnp.int32, cum.shape, 1)
    c = jnp.where(lane < HEADS, cum, 1.0)
    c3 = jnp.concatenate(_split3(c), axis=1)
    c3T = jnp.concatenate(_split3(c.T), axis=0)

    kd = _dot(xn, wk_ref[...])
    kb = _dot(c3, pk_ref[...])
    low = lane < HEAD_DIM
    for g in range(HEADS // 2):
        kg = kd[:, g * LANES:(g + 1) * LANES]
        bg = kb[:, g * LANES:(g + 1) * LANES]
        k_ref[2 * g] = jnp.where(low, kg, bg).astype(BF16)
        k_ref[2 * g + 1] = jnp.where(low, bg, kg).astype(BF16)
    qd = (_dot(wqT_ref[...], xnT) * (LOG2E / math.sqrt(HEAD_DIM))).astype(BF16)
    qb = _dot(pqT_ref[...], c3T).astype(BF16)
    for h in range(HEADS):
        val = qd[h * HEAD_DIM:(h + 1) * HEAD_DIM]
        bias = qb[h * HEAD_DIM:(h + 1) * HEAD_DIM]
        first, second = (val, bias) if h % 2 == 0 else (bias, val)
        qT_ref[h * SLOT:h * SLOT + HEAD_DIM, :] = first
        qT_ref[h * SLOT + HEAD_DIM:(h + 1) * SLOT, :] = second
    vT_ref[...] = _dot(wvT_ref[...], xnT).astype(BF16)


def _fox_proj(h, g, wqT, wk, wvT, wf, bf, pk, pqT, tri):
    n = SEQ // TM
    return pl.pallas_call(
        _fox_proj_kernel,
        grid=(n,),
        in_specs=[
            pl.BlockSpec((TM, D_MODEL), lambda i: (i, 0)),
            _const_spec((1, D_MODEL)),
            _const_spec((D_MODEL, D_MODEL)),
            _const_spec((D_MODEL, D_MODEL)),
            _const_spec((D_MODEL, D_MODEL)),
            _const_spec((D_MODEL, LANES)),
            _const_spec((1, LANES)),
            _const_spec((3 * LANES, D_MODEL)),
            _const_spec((D_MODEL, 3 * LANES)),
            _const_spec((TM, TM)),
        ],
        out_specs=[
            pl.BlockSpec((AUG, TM), lambda i: (0, i)),
            pl.BlockSpec((HEADS, TM, SLOT), lambda i: (0, i, 0)),
            pl.BlockSpec((D_MODEL, TM), lambda i: (0, i)),
        ],
        out_shape=[
            jax.ShapeDtypeStruct((AUG, SEQ), BF16),
            jax.ShapeDtypeStruct((HEADS, SEQ, SLOT), BF16),
            jax.ShapeDtypeStruct((D_MODEL, SEQ), BF16),
        ],
        scratch_shapes=[pltpu.VMEM((1, LANES), F32)],
        compiler_params=_params(("arbitrary",)),
        name="fox_proj",
    )(h, g, wqT, wk, wvT, wf, bf, pk, pqT, tri)


def _attn_kernel(qT_ref, k_ref, vT_ref, o_ref, sa_scr, sb_scr, pa_scr, pb_scr,
                 smax_a_scr, smax_b_scr, alpha_b_scr, m_scr, acc_scr):
    _attn_tile(jnp.int32(0), True, qT_ref, k_ref, vT_ref, o_ref, sa_scr, sb_scr, pa_scr, pb_scr,
               smax_a_scr, smax_b_scr, alpha_b_scr, m_scr, acc_scr)

    def tile(i, carry):
        _attn_tile(i, False, qT_ref, k_ref, vT_ref, o_ref, sa_scr, sb_scr, pa_scr, pb_scr,
                   smax_a_scr, smax_b_scr, alpha_b_scr, m_scr, acc_scr)
        return carry

    lax.fori_loop(1, SEQ // TQ, tile, 0)


def _attn_tile(i, is_first, qT_ref, k_ref, vT_ref, o_ref, sa_scr, sb_scr, pa_scr, pb_scr,
               smax_a_scr, smax_b_scr, alpha_b_scr, m_scr, acc_scr):
    groups = CH // 8
    windows = tuple(slice(j * WQ, (j + 1) * WQ) for j in range(TQ // WQ))
    half = CH // WQ
    q0 = pl.multiple_of(i * TQ, TQ)
    q_next = pl.multiple_of(jnp.minimum(i + 1, SEQ // TQ - 1) * TQ, TQ)

    def scores(c, s_ref, smax_ref, cols, mask_off=None, qbase=q0):
        rows = live_rows(mask_off)
        off = pl.multiple_of(c * CH, CH)
        q_cols = pl.ds(pl.multiple_of(qbase + cols.start, WQ), WQ)
        s = _dot(k_ref[pl.ds(off, rows), :], qT_ref[:, q_cols])
        if mask_off is not None:
            row = lax.broadcasted_iota(jnp.int32, (rows, WQ), 0)
            col = lax.broadcasted_iota(jnp.int32, (rows, WQ), 1)
            s = jnp.where(row <= col + mask_off, s, NEG)
        s_ref[:rows, cols] = s
        smax_ref[:, cols] = jnp.max(s.reshape(rows // 8, 8, WQ), axis=0)

    def live_rows(mask_off):
        return CH if mask_off is None else min(CH, mask_off + WQ)

    def probs(s_ref, p_ref, smax_ref, cols, rows=CH):
        m_old = m_scr[:, cols]
        m_new = jnp.maximum(m_old, jnp.max(smax_ref[:, cols], axis=0, keepdims=True))
        alpha = jnp.exp2(m_old - m_new)
        m_scr[:, cols] = m_new
        p_ref[:rows, cols] = jnp.exp2(s_ref[:rows, cols] - m_new).astype(BF16)
        return alpha

    ones_rows = jnp.ones((ACC_ROWS - HEAD_DIM, CH), BF16)

    def pv(c, p_ref, alpha, cols, rows=CH):
        off = pl.multiple_of(c * CH, CH)
        lhs = jnp.concatenate([vT_ref[:, pl.ds(off, rows)], ones_rows[:, :rows]], axis=0)
        acc_scr[:, cols] = alpha * acc_scr[:, cols] + _dot(lhs, p_ref[:rows, cols])

    def next_scores(t, nxt, chunk_b, j, cols):
        s_ref, smax_ref = (sb_scr, smax_b_scr) if chunk_b else (sa_scr, smax_a_scr)
        if nxt == 'plain':
            scores(2 * t + 2 + chunk_b, s_ref, smax_ref, cols)
        elif nxt == 'tile':
            scores(chunk_b, s_ref, smax_ref, cols, None, q_next)
        elif not chunk_b:
            scores(2 * t + 2, s_ref, smax_ref, cols, j * WQ if j < half else None)
        elif j >= half:
            scores(2 * t + 3, s_ref, smax_ref, cols, (j - half) * WQ)

    def pair(t, nxt, first=False, diag=False):
        for j, cols in enumerate(windows):
            rows_a = live_rows(j * WQ if j < half else None) if diag else CH
            rows_b = live_rows((j - half) * WQ) if diag and j >= half else CH
            alpha_a = probs(sa_scr, pa_scr, smax_a_scr, cols, rows_a)
            next_scores(t, nxt, 0, j, cols)
            if not first:
                pv(jnp.maximum(2 * t - 1, 0), pb_scr, alpha_b_scr[:, cols], cols)
            if not (diag and j < half):
                alpha_b_scr[:, cols] = probs(sb_scr, pb_scr, smax_b_scr, cols, rows_b)
            next_scores(t, nxt, 1, j, cols)
            pv(2 * t, pa_scr, alpha_a, cols, rows_a)

    m_scr[...] = jnp.full(m_scr.shape, NEG, F32)
    acc_scr[...] = jnp.zeros(acc_scr.shape, F32)
    if is_first:
        for j, cols in enumerate(windows):
            scores(0, sa_scr, smax_a_scr, cols, j * WQ if j < half else None)
            if j >= half:
                scores(1, sb_scr, smax_b_scr, cols, (j - half) * WQ)
    else:
        pb_scr[...] = jnp.zeros(pb_scr.shape, BF16)
        alpha_b_scr[...] = jnp.ones(alpha_b_scr.shape, F32)

        def body(t, carry):
            pair(t, 'plain')
            return carry

        lax.fori_loop(0, i - 1, body, 0)
        pair(i - 1, 'diag')
    pair(i, 'tile', first=is_first, diag=True)
    for j, cols in enumerate(windows[half:]):
        pv(2 * i + 1, pb_scr, alpha_b_scr[:, cols], cols, live_rows(j * WQ))
    l = acc_scr[HEAD_DIM:HEAD_DIM + 1, :]
    o_ref[:, pl.ds(q0, TQ)] = (acc_scr[:HEAD_DIM, :] / l).astype(BF16)


def _attention(qT, k, vT):
    return pl.pallas_call(
        _attn_kernel,
        grid=(HEADS,),
        in_specs=[
            pl.BlockSpec((None, SLOT, SEQ), lambda h: (h, 0, 0)),
            pl.BlockSpec((None, SEQ, SLOT), lambda h: (h, 0, 0)),
            pl.BlockSpec((HEAD_DIM, SEQ), lambda h: (h, 0)),
        ],
        out_specs=pl.BlockSpec((HEAD_DIM, SEQ), lambda h: (h, 0)),
        out_shape=jax.ShapeDtypeStruct((HEADS * HEAD_DIM, SEQ), BF16),
        scratch_shapes=[
            pltpu.VMEM((CH, TQ), F32),
            pltpu.VMEM((CH, TQ), F32),
            pltpu.VMEM((CH, TQ), BF16),
            pltpu.VMEM((CH, TQ), BF16),
            pltpu.VMEM((8, TQ), F32),
            pltpu.VMEM((8, TQ), F32),
            pltpu.VMEM((1, TQ), F32),
            pltpu.VMEM((1, TQ), F32),
            pltpu.VMEM((ACC_ROWS, TQ), F32),
        ],
        compiler_params=_params(("arbitrary",)),
        name="attention",
    )(qT, k, vT)


def _out_ffn_kernel(oT_ref, h_ref, wo_ref, g_ref, wg_ref, wu_ref, wd_ref, gfin_ref,
                    out_ref, *, final):
    o = oT_ref[...].T
    h1 = h_ref[...] + _dot(o, wo_ref[...])
    xn = _rms(h1, g_ref[...]).astype(BF16)
    acc = h1
    for c in range(D_FF // FF_CHUNK):
        sl = slice(c * FF_CHUNK, (c + 1) * FF_CHUNK)
        gate = _dot(xn, wg_ref[:, sl])
        up = _dot(xn, wu_ref[:, sl])
        act = (gate * (1.0 / (1.0 + jnp.exp(-gate))) * up).astype(BF16)
        acc = acc + _dot(act, wd_ref[sl, :])
    if final:
        acc = _rms(acc, gfin_ref[...])
    out_ref[...] = acc


def _out_ffn(oT, h, wo, g, wg, wu, wd, gfin, final):
    n = SEQ // TM
    return pl.pallas_call(
        functools.partial(_out_ffn_kernel, final=final),
        grid=(n,),
        in_specs=[
            pl.BlockSpec((D_MODEL, TM), lambda i: (0, i)),
            pl.BlockSpec((TM, D_MODEL), lambda i: (i, 0)),
            _const_spec((D_MODEL, D_MODEL)),
            _const_spec((1, D_MODEL)),
            _const_spec((D_MODEL, D_FF)),
            _const_spec((D_MODEL, D_FF)),
            _const_spec((D_FF, D_MODEL)),
            _const_spec((1, D_MODEL)),
        ],
        out_specs=pl.BlockSpec((TM, D_MODEL), lambda i: (i, 0)),
        out_shape=jax.ShapeDtypeStruct((SEQ, D_MODEL), F32),
        compiler_params=_params(("arbitrary",)),
        name="out_ffn",
    )(oT, h, wo, g, wg, wu, wd, gfin)


def _rope_table_kernel(pos_ref, invf_ref, cos_ref, sin_ref, cosT_ref, sinT_ref):
    ang = pos_ref[...] * invf_ref[...]
    cos = jnp.cos(ang)
    sin = jnp.sin(ang)
    cos_ref[...] = cos
    sin_ref[...] = sin
    cosT_ref[...] = cos.T
    sinT_ref[...] = sin.T


def _rope_tables(pos_col, invf):
    n = SEQ // TM
    row = pl.BlockSpec((TM, LANES), lambda i: (i, 0))
    colT = pl.BlockSpec((LANES, TM), lambda i: (0, i))
    return pl.pallas_call(
        _rope_table_kernel,
        grid=(n,),
        in_specs=[pl.BlockSpec((TM, 1), lambda i: (i, 0)), _const_spec((1, LANES))],
        out_specs=[row, row, colT, colT],
        out_shape=[jax.ShapeDtypeStruct((SEQ, LANES), F32)] * 2
        + [jax.ShapeDtypeStruct((LANES, SEQ), F32)] * 2,
        compiler_params=_params(("arbitrary",)),
        name="rope_tables",
    )(pos_col, invf)


def _mla_kv_kernel(h_ref, g_ref, wa_ref, gckv_ref, cos_ref, sin_ref, wuk_ref,
                   p1_ref, p2_ref, wuvT_ref, k_ref, vT_ref):
    hn = _rms(h_ref[...], g_ref[...]).astype(BF16)
    a = _dot(hn, wa_ref[...])
    c = _rms(a[:, :KV_RANK], gckv_ref[...]).astype(BF16)
    x1 = a[:, KV_RANK:KV_RANK + LANES]
    x2 = a[:, KV_RANK + LANES:]
    cos = cos_ref[...]
    sin = sin_ref[...]
    r1 = (x1 * cos - x2 * sin).astype(BF16)
    r2 = (x1 * sin + x2 * cos).astype(BF16)
    k = _dot(c, wuk_ref[...]) + _dot(r1, p1_ref[...]) + _dot(r2, p2_ref[...])
    _store_heads(k_ref, k.astype(BF16))
    vT_ref[...] = _dot(wuvT_ref[...], c.T).astype(BF16)


def _mla_kv(h, g, wa, gckv, cos, sin, wuk, p1, p2, wuvT):
    n = SEQ // TM
    return pl.pallas_call(
        _mla_kv_kernel,
        grid=(n,),
        in_specs=[
            pl.BlockSpec((TM, D_MODEL), lambda i: (i, 0)),
            _const_spec((1, D_MODEL)),
            _const_spec((D_MODEL, 4 * LANES)),
            _const_spec((1, KV_RANK)),
            pl.BlockSpec((TM, LANES), lambda i: (i, 0)),
            pl.BlockSpec((TM, LANES), lambda i: (i, 0)),
            _const_spec((KV_RANK, AUG)),
            _const_spec((LANES, AUG)),
            _const_spec((LANES, AUG)),
            _const_spec((D_MODEL, KV_RANK)),
        ],
        out_specs=[
            pl.BlockSpec((HEADS, TM, SLOT), lambda i: (0, i, 0)),
            pl.BlockSpec((D_MODEL, TM), lambda i: (0, i)),
        ],
        out_shape=[
            jax.ShapeDtypeStruct((HEADS, SEQ, SLOT), BF16),
            jax.ShapeDtypeStruct((D_MODEL, SEQ), BF16),
        ],
        compiler_params=_params(("arbitrary",)),
        name="mla_kv",
    )(h, g, wa, gckv, cos, sin, wuk, p1, p2, wuvT)


def _mla_q_kernel(h_ref, g_ref, wdq_ref, gcq_ref, wnT_ref, wx1T_ref, wx2T_ref,
                  cosT_ref, sinT_ref, qT_ref):
    scale = LOG2E / math.sqrt(MLA_NOPE + MLA_ROPE)
    xn = _rms(h_ref[...], g_ref[...]).astype(BF16)
    cq = _rms(_dot(xn, wdq_ref[...]), gcq_ref[...]).astype(BF16)
    cqT = cq.T
    nT = _dot(wnT_ref[...], cqT) * scale
    x1T = _dot(wx1T_ref[...], cqT)
    x2T = _dot(wx2T_ref[...], cqT)
    cosT = jnp.concatenate([cosT_ref[...]] * HEADS, axis=0)
    sinT = jnp.concatenate([sinT_ref[...]] * HEADS, axis=0)
    r1 = (x1T * cosT - x2T * sinT) * scale
    r2 = (x1T * sinT + x2T * cosT) * scale
    zeros = jnp.zeros((SLOT - MLA_NOPE - MLA_ROPE, TM), BF16)
    for h in range(HEADS):
        qT_ref[h, 0:MLA_NOPE, :] = nT[h * MLA_NOPE:(h + 1) * MLA_NOPE].astype(BF16)
        qT_ref[h, MLA_NOPE:MLA_NOPE + HALF_ROPE, :] = (
            r1[h * HALF_ROPE:(h + 1) * HALF_ROPE].astype(BF16))
        qT_ref[h, MLA_NOPE + HALF_ROPE:MLA_NOPE + MLA_ROPE, :] = (
            r2[h * HALF_ROPE:(h + 1) * HALF_ROPE].astype(BF16))
        qT_ref[h, MLA_NOPE + MLA_ROPE:, :] = zeros


def _mla_q(h, g, wdq, gcq, wnT, wx1T, wx2T, cosT, sinT):
    n = SEQ // TM
    return pl.pallas_call(
        _mla_q_kernel,
        grid=(n,),
        in_specs=[
            pl.BlockSpec((TM, D_MODEL), lambda i: (i, 0)),
            _const_spec((1, D_MODEL)),
            _const_spec((D_MODEL, Q_RANK)),
            _const_spec((1, Q_RANK)),
            _const_spec((HEADS * MLA_NOPE, Q_RANK)),
            _const_spec((HEADS * HALF_ROPE, Q_RANK)),
            _const_spec((HEADS * HALF_ROPE, Q_RANK)),
            pl.BlockSpec((HALF_ROPE, TM), lambda i: (0, i)),
            pl.BlockSpec((HALF_ROPE, TM), lambda i: (0, i)),
        ],
        out_specs=pl.BlockSpec((HEADS, SLOT, TM), lambda i: (0, 0, i)),
        out_shape=jax.ShapeDtypeStruct((HEADS, SLOT, SEQ), BF16),
        compiler_params=_params(("arbitrary",)),
        name="mla_q",
    )(h, g, wdq, gcq, wnT, wx1T, wx2T, cosT, sinT)


def _pad_heads_cols(w, width):
    k = w.shape[0]
    w = w.reshape(k, HEADS, width)
    w = jnp.pad(w, ((0, 0), (0, 0), (0, SLOT - width)))
    return w.reshape(k, AUG)


def _fox_bias_placement():
    w = HEADS * HEAD_DIM
    pk = np.zeros((3 * LANES, w), np.float32)
    for h in range(HEADS):
        base = (h // 2) * LANES + (HEAD_DIM if h % 2 == 0 else 0)
        for j in range(3):
            pk[HEADS, base + j] = 1.0
            pk[j * LANES + h, base + 3 + j] = -1.0
    pq = np.zeros((w, 3 * LANES), np.float32)
    for h in range(HEADS):
        for j in range(3):
            pq[h * HEAD_DIM + j, j * LANES + h] = 1.0
            pq[h * HEAD_DIM + 3 + j, HEADS] = 1.0
    return jnp.asarray(pk, BF16), jnp.asarray(pq, BF16)


def _rope_placement():
    p1 = np.zeros((LANES, AUG), np.float32)
    p2 = np.zeros((LANES, AUG), np.float32)
    for h in range(HEADS):
        for j in range(HALF_ROPE):
            p1[j, h * SLOT + MLA_NOPE + j] = 1.0
            p2[j, h * SLOT + MLA_NOPE + HALF_ROPE + j] = 1.0
    return jnp.asarray(p1, BF16), jnp.asarray(p2, BF16)


def kernel(x, positions, attn_norm, ffn_norm, w_gate, w_up, w_down, fox_w_in, fox_b_f,
           fox_w_o, kv_norm, w_kv_a, ckv_norm, w_uk, w_uv, mla_w_dq, cq_norm, mla_w_uq,
           mla_w_o, final_norm):
    assert x.shape == (1, SEQ, D_MODEL)
    h = x.reshape(SEQ, D_MODEL)
    W = HEADS * HEAD_DIM
    pk, pqT = _fox_bias_placement()
    p1, p2 = _rope_placement()
    tri = jnp.asarray(np.tril(np.ones((TM, TM), np.float32)), BF16)
    row = lambda v: v.reshape(1, -1)

    def ffn_args(l):
        return (row(ffn_norm[l]), w_gate[l].astype(BF16), w_up[l].astype(BF16),
                w_down[l].astype(BF16), row(final_norm))

    for l in range(N_A):
        w_in = fox_w_in[l]
        wqT = w_in[:, :W].T.astype(BF16)
        wk = w_in[:, W:2 * W].astype(BF16)
        wvT = w_in[:, 2 * W:3 * W].T.astype(BF16)
        wf = jnp.pad(w_in[:, 3 * W:], ((0, 0), (0, LANES - HEADS))).astype(BF16)
        bf = jnp.pad(fox_b_f[l], (0, LANES - HEADS)).reshape(1, LANES)
        qT, k, vT = _fox_proj(h, row(attn_norm[l]), wqT, wk, wvT, wf, bf, pk, pqT, tri)
        oT = _attention(qT.reshape(HEADS, SLOT, SEQ), k, vT)
        h = _out_ffn(oT, h, fox_w_o[l].astype(BF16), *ffn_args(l), final=False)

    invf = ROPE_THETA ** (-jnp.arange(0, HALF_ROPE, dtype=F32) * 2.0 / MLA_ROPE)
    invf = jnp.pad(invf, (0, LANES - HALF_ROPE)).reshape(1, LANES)
    pos_col = positions.reshape(SEQ, 1).astype(F32)
    cos, sin, cosT, sinT = _rope_tables(pos_col, invf)
    wa = jnp.concatenate([
        w_kv_a[:, :KV_RANK],
        jnp.pad(w_kv_a[:, KV_RANK:KV_RANK + HALF_ROPE], ((0, 0), (0, LANES - HALF_ROPE))),
        jnp.pad(w_kv_a[:, KV_RANK + HALF_ROPE:], ((0, 0), (0, LANES - HALF_ROPE))),
    ], axis=1).astype(BF16)
    wuk = _pad_heads_cols(w_uk.reshape(KV_RANK, HEADS * MLA_NOPE), MLA_NOPE).astype(BF16)
    wuvT = w_uv.reshape(KV_RANK, HEADS * HEAD_DIM).T.astype(BF16)
    k_sh, vT_sh = _mla_kv(h, row(kv_norm), wa, row(ckv_norm), cos, sin, wuk, p1, p2, wuvT)

    for l in range(N_A, DEPTH):
        j = l - N_A
        wuq = mla_w_uq[j].reshape(Q_RANK, HEADS, MLA_NOPE + MLA_ROPE)
        wnT = wuq[:, :, :MLA_NOPE].reshape(Q_RANK, -1).T.astype(BF16)
        wx1T = wuq[:, :, MLA_NOPE:MLA_NOPE + HALF_ROPE].reshape(Q_RANK, -1).T.astype(BF16)
        wx2T = wuq[:, :, MLA_NOPE + HALF_ROPE:].reshape(Q_RANK, -1).T.astype(BF16)
        qT = _mla_q(h, row(attn_norm[l]), mla_w_dq[j].astype(BF16), row(cq_norm[j]),
                    wnT, wx1T, wx2T, cosT, sinT)
        oT = _attention(qT, k_sh, vT_sh)
        h = _out_ffn(oT, h, mla_w_o[j].astype(BF16), *ffn_args(l), final=(l == DEPTH - 1))

    return h.reshape(1, SEQ, D_MODEL)
```

```python
import functools
import math

import numpy as np
import jax
import jax.numpy as jnp
from jax import lax
from jax.experimental import pallas as pl
from jax.experimental.pallas import tpu as pltpu

F32 = jnp.float32
BF16 = jnp.bfloat16

D_MODEL = 1024
SEQ = 16384
DEPTH = 4
N_A = DEPTH // 2
RMS_EPS = 1e-6
HEADS = 16
HEAD_DIM = 64
MLA_NOPE = 64
MLA_ROPE = 32
HALF_ROPE = MLA_ROPE // 2
KV_RANK = 256
Q_RANK = 768
ROPE_THETA = 10000.0
D_FF = 2816

LANES = 128
SLOT = 128
AUG = HEADS * SLOT
LOG2E = math.log2(math.e)
NEG = -1e30

TM = 512
TQ = 1024
CH = TQ // 2
WQ = 256
ACC_ROWS = HEAD_DIM + 16
FF_CHUNK = 256
VMEM_LIMIT = 56 * 1024 * 1024


def _dot(a, b):
    return jnp.dot(a, b, preferred_element_type=F32)


def _rms(x, g):
    inv = lax.rsqrt(jnp.mean(x * x, axis=-1, keepdims=True) + RMS_EPS)
    return (x * inv) * g


def _split3(x):
    hi = x.astype(BF16)
    r = x - hi.astype(F32)
    mid = r.astype(BF16)
    lo = (r - mid.astype(F32)).astype(BF16)
    return hi, mid, lo


def _const_spec(shape):
    nd = len(shape)
    return pl.BlockSpec(shape, lambda *_: (0,) * nd, pipeline_mode=pl.Buffered(1))


def _params(sem):
    return pltpu.CompilerParams(dimension_semantics=sem, vmem_limit_bytes=VMEM_LIMIT)


def _fox_proj_kernel(h_ref, g_ref, wqT_ref, wk_ref, wvT_ref, wf_ref, bf_ref,
                     pk_ref, pqT_ref, tri_ref, qT_ref, k_ref, vT_ref, carry_ref):
    @pl.when(pl.program_id(0) == 0)
    def _():
        carry_ref[...] = jnp.zeros_like(carry_ref)

    xn = _rms(h_ref[...], g_ref[...]).astype(BF16)
    xnT = xn.T

    f = _dot(xn, wf_ref[...]) + bf_ref[...]
    logf = (jnp.minimum(f, 0.0) - jnp.log1p(jnp.exp(-jnp.abs(f)))) * LOG2E
    hi, mid, lo = _split3(logf)
    tri = tri_ref[...]
    cum = _dot(tri, hi) + _dot(tri, mid) + _dot(tri, lo) + carry_ref[...]
    carry_ref[...] = cum[TM - 1:TM, :]

    lane = lax.broadcasted_iota(jnp.int32, cum.shape, 1)
    c = jnp.where(lane < HEADS, cum, 1.0)
    c3 = jnp.concatenate(_split3(c), axis=1)
    c3T = jnp.concatenate(_split3(c.T), axis=0)

    kd = _dot(xn, wk_ref[...])
    kb = _dot(c3, pk_ref[...])
    low = lane < HEAD_DIM
    for g in range(HEADS // 2):
        kg = kd[:, g * LANES:(g + 1) * LANES]
        bg = kb[:, g * LANES:(g + 1) * LANES]
        k_ref[2 * g] = jnp.where(low, kg, bg).astype(BF16)
        k_ref[2 * g + 1] = jnp.where(low, bg, kg).astype(BF16)
    qd = (_dot(wqT_ref[...], xnT) * (LOG2E / math.sqrt(HEAD_DIM))).astype(BF16)
    qb = _dot(pqT_ref[...], c3T).astype(BF16)
    for h in range(HEADS):
        val = qd[h * HEAD_DIM:(h + 1) * HEAD_DIM]
        bias = qb[h * HEAD_DIM:(h + 1) * HEAD_DIM]
        first, second = (val, bias) if h % 2 == 0 else (bias, val)
        qT_ref[h * SLOT:h * SLOT + HEAD_DIM, :] = first
        qT_ref[h * SLOT + HEAD_DIM:(h + 1) * SLOT, :] = second
    vT_ref[...] = _dot(wvT_ref[...], xnT).astype(BF16)


def _fox_proj(h, g, wqT, wk, wvT, wf, bf, pk, pqT, tri):
    n = SEQ // TM
    return pl.pallas_call(
        _fox_proj_kernel,
        grid=(n,),
        in_specs=[
            pl.BlockSpec((TM, D_MODEL), lambda i: (i, 0)),
            _const_spec((1, D_MODEL)),
            _const_spec((D_MODEL, D_MODEL)),
            _const_spec((D_MODEL, D_MODEL)),
            _const_spec((D_MODEL, D_MODEL)),
            _const_spec((D_MODEL, LANES)),
            _const_spec((1, LANES)),
            _const_spec((3 * LANES, D_MODEL)),
            _const_spec((D_MODEL, 3 * LANES)),
            _const_spec((TM, TM)),
        ],
        out_specs=[
            pl.BlockSpec((AUG, TM), lambda i: (0, i)),
            pl.BlockSpec((HEADS, TM, SLOT), lambda i: (0, i, 0)),
            pl.BlockSpec((D_MODEL, TM), lambda i: (0, i)),
        ],
        out_shape=[
            jax.ShapeDtypeStruct((AUG, SEQ), BF16),
            jax.ShapeDtypeStruct((HEADS, SEQ, SLOT), BF16),
            jax.ShapeDtypeStruct((D_MODEL, SEQ), BF16),
        ],
        scratch_shapes=[pltpu.VMEM((1, LANES), F32)],
        compiler_params=_params(("arbitrary",)),
        name="fox_proj",
    )(h, g, wqT, wk, wvT, wf, bf, pk, pqT, tri)


def _attn_kernel(qT_ref, k_ref, vT_ref, o_ref, sa_scr, sb_scr, pa_scr, pb_scr,
                 smax_a_scr, smax_b_scr, alpha_b_scr, m_scr, acc_scr):
    _attn_tile(jnp.int32(0), True, qT_ref, k_ref, vT_ref, o_ref, sa_scr, sb_scr, pa_scr, pb_scr,
               smax_a_scr, smax_b_scr, alpha_b_scr, m_scr, acc_scr)

    def tile(i, carry):
        _attn_tile(i, False, qT_ref, k_ref, vT_ref, o_ref, sa_scr, sb_scr, pa_scr, pb_scr,
                   smax_a_scr, smax_b_scr, alpha_b_scr, m_scr, acc_scr)
        return carry

    lax.fori_loop(1, SEQ // TQ, tile, 0)


def _attn_tile(i, is_first, qT_ref, k_ref, vT_ref, o_ref, sa_scr, sb_scr, pa_scr, pb_scr,
               smax_a_scr, smax_b_scr, alpha_b_scr, m_scr, acc_scr):
    groups = CH // 8
    windows = tuple(slice(j * WQ, (j + 1) * WQ) for j in range(TQ // WQ))
    half = CH // WQ
    q0 = pl.multiple_of(i * TQ, TQ)
    q_next = pl.multiple_of(jnp.minimum(i + 1, SEQ // TQ - 1) * TQ, TQ)

    def scores(c, s_ref, smax_ref, cols, mask_off=None, qbase=q0):
        rows = live_rows(mask_off)
        off = pl.multiple_of(c * CH, CH)
        q_cols = pl.ds(pl.multiple_of(qbase + cols.start, WQ), WQ)
        s = _dot(k_ref[pl.ds(off, rows), :], qT_ref[:, q_cols])
        if mask_off is not None:
            row = lax.broadcasted_iota(jnp.int32, (rows, WQ), 0)
            col = lax.broadcasted_iota(jnp.int32, (rows, WQ), 1)
            s = jnp.where(row <= col + mask_off, s, NEG)
        s_ref[:rows, cols] = s
        smax_ref[:, cols] = jnp.max(s.reshape(rows // 8, 8, WQ), axis=0)

    def live_rows(mask_off):
        return CH if mask_off is None else min(CH, mask_off + WQ)

    def probs(s_ref, p_ref, smax_ref, cols, rows=CH):
        m_old = m_scr[:, cols]
        m_new = jnp.maximum(m_old, jnp.max(smax_ref[:, cols], axis=0, keepdims=True))
        alpha = jnp.exp2(m_old - m_new)
        m_scr[:, cols] = m_new
        p_ref[:rows, cols] = jnp.exp2(s_ref[:rows, cols] - m_new).astype(BF16)
        return alpha

    ones_rows = jnp.ones((ACC_ROWS - HEAD_DIM, CH), BF16)

    def pv(c, p_ref, alpha, cols, rows=CH):
        off = pl.multiple_of(c * CH, CH)
        lhs = jnp.concatenate([vT_ref[:, pl.ds(off, rows)], ones_rows[:, :rows]], axis=0)
        acc_scr[:, cols] = alpha * acc_scr[:, cols] + _dot(lhs, p_ref[:rows, cols])

    def next_scores(t, nxt, chunk_b, j, cols):
        s_ref, smax_ref = (sb_scr, smax_b_scr) if chunk_b else (sa_scr, smax_a_scr)
        if nxt == 'plain':
            scores(2 * t + 2 + chunk_b, s_ref, smax_ref, cols)
        elif nxt == 'tile':
            scores(chunk_b, s_ref, smax_ref, cols, None, q_next)
        elif not chunk_b:
            scores(2 * t + 2, s_ref, smax_ref, cols, j * WQ if j < half else None)
        elif j >= half:
            scores(2 * t + 3, s_ref, smax_ref, cols, (j - half) * WQ)

    def pair(t, nxt, first=False, diag=False):
        for j, cols in enumerate(windows):
            rows_a = live_rows(j * WQ if j < half else None) if diag else CH
            rows_b = live_rows((j - half) * WQ) if diag and j >= half else CH
            alpha_a = probs(sa_scr, pa_scr, smax_a_scr, cols, rows_a)
            next_scores(t, nxt, 0, j, cols)
            if not first:
                pv(jnp.maximum(2 * t - 1, 0), pb_scr, alpha_b_scr[:, cols], cols)
            if not (diag and j < half):
                alpha_b_scr[:, cols] = probs(sb_scr, pb_scr, smax_b_scr, cols, rows_b)
            next_scores(t, nxt, 1, j, cols)
            pv(2 * t, pa_scr, alpha_a, cols, rows_a)

    m_scr[...] = jnp.full(m_scr.shape, NEG, F32)
    acc_scr[...] = jnp.zeros(acc_scr.shape, F32)
    if is_first:
        for j, cols in enumerate(windows):
            scores(0, sa_scr, smax_a_scr, cols, j * WQ if j < half else None)
            if j >= half:
                scores(1, sb_scr, smax_b_scr, cols, (j - half) * WQ)
    else:
        pb_scr[...] = jnp.zeros(pb_scr.shape, BF16)
        alpha_b_scr[...] = jnp.ones(alpha_b_scr.shape, F32)

        def body(t, carry):
            pair(t, 'plain')
            return carry

        lax.fori_loop(0, i - 1, body, 0)
        pair(i - 1, 'diag')
    pair(i, 'tile', first=is_first, diag=True)
    for j, cols in enumerate(windows[half:]):
        pv(2 * i + 1, pb_scr, alpha_b_scr[:, cols], cols, live_rows(j * WQ))
    l = acc_scr[HEAD_DIM:HEAD_DIM + 1, :]
    o_ref[:, pl.ds(q0, TQ)] = (acc_scr[:HEAD_DIM, :] / l).astype(BF16)


def _attention(qT, k, vT):
    return pl.pallas_call(
        _attn_kernel,
        grid=(HEADS,),
        in_specs=[
            pl.BlockSpec((None, SLOT, SEQ), lambda h: (h, 0, 0)),
            pl.BlockSpec((None, SEQ, SLOT), lambda h: (h, 0, 0)),
            pl.BlockSpec((HEAD_DIM, SEQ), lambda h: (h, 0)),
        ],
        out_specs=pl.BlockSpec((HEAD_DIM, SEQ), lambda h: (h, 0)),
        out_shape=jax.ShapeDtypeStruct((HEADS * HEAD_DIM, SEQ), BF16),
        scratch_shapes=[
            pltpu.VMEM((CH, TQ), F32),
            pltpu.VMEM((CH, TQ), F32),
            pltpu.VMEM((CH, TQ), BF16),
            pltpu.VMEM((CH, TQ), BF16),
            pltpu.VMEM((8, TQ), F32),
            pltpu.VMEM((8, TQ), F32),
            pltpu.VMEM((1, TQ), F32),
            pltpu.VMEM((1, TQ), F32),
            pltpu.VMEM((ACC_ROWS, TQ), F32),
        ],
        compiler_params=_params(("arbitrary",)),
        name="attention",
    )(qT, k, vT)


def _out_ffn_kernel(oT_ref, h_ref, wo_ref, g_ref, wg_ref, wu_ref, wd_ref, gfin_ref,
                    out_ref, *, final):
    o = oT_ref[...].T
    h1 = h_ref[...] + _dot(o, wo_ref[...])
    xn = _rms(h1, g_ref[...]).astype(BF16)
    acc = h1
    for c in range(D_FF // FF_CHUNK):
        sl = slice(c * FF_CHUNK, (c + 1) * FF_CHUNK)
        gate = _dot(xn, wg_ref[:, sl])
        up = _dot(xn, wu_ref[:, sl])
        act = (gate * (1.0 / (1.0 + jnp.exp(-gate))) * up).astype(BF16)
        acc = acc + _dot(act, wd_ref[sl, :])
    if final:
        acc = _rms(acc, gfin_ref[...])
    out_ref[...] = acc


def _out_ffn(oT, h, wo, g, wg, wu, wd, gfin, final):
    n = SEQ // TM
    return pl.pallas_call(
        functools.partial(_out_ffn_kernel, final=final),
        grid=(n,),
        in_specs=[
            pl.BlockSpec((D_MODEL, TM), lambda i: (0, i)),
            pl.BlockSpec((TM, D_MODEL), lambda i: (i, 0)),
            _const_spec((D_MODEL, D_MODEL)),
            _const_spec((1, D_MODEL)),
            _const_spec((D_MODEL, D_FF)),
            _const_spec((D_MODEL, D_FF)),
            _const_spec((D_FF, D_MODEL)),
            _const_spec((1, D_MODEL)),
        ],
        out_specs=pl.BlockSpec((TM, D_MODEL), lambda i: (i, 0)),
        out_shape=jax.ShapeDtypeStruct((SEQ, D_MODEL), F32),
        compiler_params=_params(("arbitrary",)),
        name="out_ffn",
    )(oT, h, wo, g, wg, wu, wd, gfin)


def _rope_table_kernel(pos_ref, invf_ref, cos_ref, sin_ref, cosT_ref, sinT_ref):
    ang = pos_ref[...] * invf_ref[...]
    cos = jnp.cos(ang)
    sin = jnp.sin(ang)
    cos_ref[...] = cos
    sin_ref[...] = sin
    cosT_ref[...] = cos.T
    sinT_ref[...] = sin.T


def _rope_tables(pos_col, invf):
    n = SEQ // TM
    row = pl.BlockSpec((TM, LANES), lambda i: (i, 0))
    colT = pl.BlockSpec((LANES, TM), lambda i: (0, i))
    return pl.pallas_call(
        _rope_table_kernel,
        grid=(n,),
        in_specs=[pl.BlockSpec((TM, 1), lambda i: (i, 0)), _const_spec((1, LANES))],
        out_specs=[row, row, colT, colT],
        out_shape=[jax.ShapeDtypeStruct((SEQ, LANES), F32)] * 2
        + [jax.ShapeDtypeStruct((LANES, SEQ), F32)] * 2,
        compiler_params=_params(("arbitrary",)),
        name="rope_tables",
    )(pos_col, invf)


def _mla_kv_kernel(h_ref, g_ref, wa_ref, gckv_ref, cos_ref, sin_ref, wuk_ref,
                   p1_ref, p2_ref, wuvT_ref, k_ref, vT_ref):
    hn = _rms(h_ref[...], g_ref[...]).astype(BF16)
    a = _dot(hn, wa_ref[...])
    c = _rms(a[:, :KV_RANK], gckv_ref[...]).astype(BF16)
    x1 = a[:, KV_RANK:KV_RANK + LANES]
    x2 = a[:, KV_RANK + LANES:]
    cos = cos_ref[...]
    sin = sin_ref[...]
    r1 = (x1 * cos - x2 * sin).astype(BF16)
    r2 = (x1 * sin + x2 * cos).astype(BF16)
    kd = _dot(c, wuk_ref[...])
    rb = _dot(r1, p1_ref[...]) + _dot(r2, p2_ref[...])
    low = lax.broadcasted_iota(jnp.int32, rb.shape, 1) < MLA_NOPE
    for g in range(HEADS // 2):
        kg = kd[:, g * LANES:(g + 1) * LANES]
        k_ref[2 * g] = jnp.where(low, kg, rb).astype(BF16)
        k_ref[2 * g + 1] = jnp.where(low, rb, kg).astype(BF16)
    vT_ref[...] = _dot(wuvT_ref[...], c.T).astype(BF16)


def _mla_kv(h, g, wa, gckv, cos, sin, wuk, p1, p2, wuvT):
    n = SEQ // TM
    return pl.pallas_call(
        _mla_kv_kernel,
        grid=(n,),
        in_specs=[
            pl.BlockSpec((TM, D_MODEL), lambda i: (i, 0)),
            _const_spec((1, D_MODEL)),
            _const_spec((D_MODEL, 4 * LANES)),
            _const_spec((1, KV_RANK)),
            pl.BlockSpec((TM, LANES), lambda i: (i, 0)),
            pl.BlockSpec((TM, LANES), lambda i: (i, 0)),
            _const_spec((KV_RANK, D_MODEL)),
            _const_spec((LANES, LANES)),
            _const_spec((LANES, LANES)),
            _const_spec((D_MODEL, KV_RANK)),
        ],
        out_specs=[
            pl.BlockSpec((HEADS, TM, SLOT), lambda i: (0, i, 0)),
            pl.BlockSpec((D_MODEL, TM), lambda i: (0, i)),
        ],
        out_shape=[
            jax.ShapeDtypeStruct((HEADS, SEQ, SLOT), BF16),
            jax.ShapeDtypeStruct((D_MODEL, SEQ), BF16),
        ],
        compiler_params=_params(("arbitrary",)),
        name="mla_kv",
    )(h, g, wa, gckv, cos, sin, wuk, p1, p2, wuvT)


def _mla_q_kernel(h_ref, g_ref, wdq_ref, gcq_ref, wnT_ref, wx1T_ref, wx2T_ref,
                  cosT_ref, sinT_ref, qT_ref):
    scale = LOG2E / math.sqrt(MLA_NOPE + MLA_ROPE)
    xn = _rms(h_ref[...], g_ref[...]).astype(BF16)
    cq = _rms(_dot(xn, wdq_ref[...]), gcq_ref[...]).astype(BF16)
    cqT = cq.T
    nT = _dot(wnT_ref[...], cqT) * scale
    x1T = _dot(wx1T_ref[...], cqT)
    x2T = _dot(wx2T_ref[...], cqT)
    cosT = jnp.concatenate([cosT_ref[...]] * HEADS, axis=0)
    sinT = jnp.concatenate([sinT_ref[...]] * HEADS, axis=0)
    r1 = (x1T * cosT - x2T * sinT) * scale
    r2 = (x1T * sinT + x2T * cosT) * scale
    zeros = jnp.zeros((SLOT - MLA_NOPE - MLA_ROPE, TM), BF16)
    for h in range(HEADS):
        nope0, rope0 = (0, MLA_NOPE) if h % 2 == 0 else (SLOT - MLA_NOPE, 0)
        qT_ref[h, nope0:nope0 + MLA_NOPE, :] = nT[h * MLA_NOPE:(h + 1) * MLA_NOPE].astype(BF16)
        qT_ref[h, rope0:rope0 + HALF_ROPE, :] = r1[h * HALF_ROPE:(h + 1) * HALF_ROPE].astype(BF16)
        qT_ref[h, rope0 + HALF_ROPE:rope0 + MLA_ROPE, :] = (
            r2[h * HALF_ROPE:(h + 1) * HALF_ROPE].astype(BF16))
        qT_ref[h, rope0 + MLA_ROPE:rope0 + SLOT - MLA_NOPE, :] = zeros


def _mla_q(h, g, wdq, gcq, wnT, wx1T, wx2T, cosT, sinT):
    n = SEQ // TM
    return pl.pallas_call(
        _mla_q_kernel,
        grid=(n,),
        in_specs=[
            pl.BlockSpec((TM, D_MODEL), lambda i: (i, 0)),
            _const_spec((1, D_MODEL)),
            _const_spec((D_MODEL, Q_RANK)),
            _const_spec((1, Q_RANK)),
            _const_spec((HEADS * MLA_NOPE, Q_RANK)),
            _const_spec((HEADS * HALF_ROPE, Q_RANK)),
            _const_spec((HEADS * HALF_ROPE, Q_RANK)),
            pl.BlockSpec((HALF_ROPE, TM), lambda i: (0, i)),
            pl.BlockSpec((HALF_ROPE, TM), lambda i: (0, i)),
        ],
        out_specs=pl.BlockSpec((HEADS, SLOT, TM), lambda i: (0, 0, i)),
        out_shape=jax.ShapeDtypeStruct((HEADS, SLOT, SEQ), BF16),
        compiler_params=_params(("arbitrary",)),
        name="mla_q",
    )(h, g, wdq, gcq, wnT, wx1T, wx2T, cosT, sinT)


def _fox_bias_placement():
    w = HEADS * HEAD_DIM
    pk = np.zeros((3 * LANES, w), np.float32)
    for h in range(HEADS):
        base = (h // 2) * LANES + (HEAD_DIM if h % 2 == 0 else 0)
        for j in range(3):
            pk[HEADS, base + j] = 1.0
            pk[j * LANES + h, base + 3 + j] = -1.0
    pq = np.zeros((w, 3 * LANES), np.float32)
    for h in range(HEADS):
        for j in range(3):
            pq[h * HEAD_DIM + j, j * LANES + h] = 1.0
            pq[h * HEAD_DIM + 3 + j, HEADS] = 1.0
    return jnp.asarray(pk, BF16), jnp.asarray(pq, BF16)


def _rope_placement():
    p1 = np.zeros((LANES, LANES), np.float32)
    p2 = np.zeros((LANES, LANES), np.float32)
    for base in (0, MLA_NOPE):
        for j in range(HALF_ROPE):
            p1[j, base + j] = 1.0
            p2[j, base + HALF_ROPE + j] = 1.0
    return jnp.asarray(p1, BF16), jnp.asarray(p2, BF16)


def kernel(x, positions, attn_norm, ffn_norm, w_gate, w_up, w_down, fox_w_in, fox_b_f,
           fox_w_o, kv_norm, w_kv_a, ckv_norm, w_uk, w_uv, mla_w_dq, cq_norm, mla_w_uq,
           mla_w_o, final_norm):
    assert x.shape == (1, SEQ, D_MODEL)
    h = x.reshape(SEQ, D_MODEL)
    W = HEADS * HEAD_DIM
    pk, pqT = _fox_bias_placement()
    p1, p2 = _rope_placement()
    tri = jnp.asarray(np.tril(np.ones((TM, TM), np.float32)), BF16)
    row = lambda v: v.reshape(1, -1)

    def ffn_args(l):
        return (row(ffn_norm[l]), w_gate[l].astype(BF16), w_up[l].astype(BF16),
                w_down[l].astype(BF16), row(final_norm))

    for l in range(N_A):
        w_in = fox_w_in[l]
        wqT = w_in[:, :W].T.astype(BF16)
        wk = w_in[:, W:2 * W].astype(BF16)
        wvT = w_in[:, 2 * W:3 * W].T.astype(BF16)
        wf = jnp.pad(w_in[:, 3 * W:], ((0, 0), (0, LANES - HEADS))).astype(BF16)
        bf = jnp.pad(fox_b_f[l], (0, LANES - HEADS)).reshape(1, LANES)
        qT, k, vT = _fox_proj(h, row(attn_norm[l]), wqT, wk, wvT, wf, bf, pk, pqT, tri)
        oT = _attention(qT.reshape(HEADS, SLOT, SEQ), k, vT)
        h = _out_ffn(oT, h, fox_w_o[l].astype(BF16), *ffn_args(l), final=False)

    invf = ROPE_THETA ** (-jnp.arange(0, HALF_ROPE, dtype=F32) * 2.0 / MLA_ROPE)
    invf = jnp.pad(invf, (0, LANES - HALF_ROPE)).reshape(1, LANES)
    pos_col = positions.reshape(SEQ, 1).astype(F32)
    cos, sin, cosT, sinT = _rope_tables(pos_col, invf)
    wa = jnp.concatenate([
        w_kv_a[:, :KV_RANK],
        jnp.pad(w_kv_a[:, KV_RANK:KV_RANK + HALF_ROPE], ((0, 0), (0, LANES - HALF_ROPE))),
        jnp.pad(w_kv_a[:, KV_RANK + HALF_ROPE:], ((0, 0), (0, LANES - HALF_ROPE))),
    ], axis=1).astype(BF16)
    wuk = w_uk.reshape(KV_RANK, HEADS * MLA_NOPE).astype(BF16)
    wuvT = w_uv.reshape(KV_RANK, HEADS * HEAD_DIM).T.astype(BF16)
    k_sh, vT_sh = _mla_kv(h, row(kv_norm), wa, row(ckv_norm), cos, sin, wuk, p1, p2, wuvT)

    for l in range(N_A, DEPTH):
        j = l - N_A
        wuq = mla_w_uq[j].reshape(Q_RANK, HEADS, MLA_NOPE + MLA_ROPE)
        wnT = wuq[:, :, :MLA_NOPE].reshape(Q_RANK, -1).T.astype(BF16)
        wx1T = wuq[:, :, MLA_NOPE:MLA_NOPE + HALF_ROPE].reshape(Q_RANK, -1).T.astype(BF16)
        wx2T = wuq[:, :, MLA_NOPE + HALF_ROPE:].reshape(Q_RANK, -1).T.astype(BF16)
        qT = _mla_q(h, row(attn_norm[l]), mla_w_dq[j].astype(BF16), row(cq_norm[j]),
                    wnT, wx1T, wx2T, cosT, sinT)
        oT = _attention(qT, k_sh, vT_sh)
        h = _out_ffn(oT, h, mla_w_o[j].astype(BF16), *ffn_args(l), final=(l == DEPTH - 1))

    return h.reshape(1, SEQ, D_MODEL)
```

```python
import functools
import math

import numpy as np
import jax
import jax.numpy as jnp
from jax import lax
from jax.experimental import pallas as pl
from jax.experimental.pallas import tpu as pltpu

F32 = jnp.float32
BF16 = jnp.bfloat16

D_MODEL = 1024
SEQ = 16384
DEPTH = 4
N_A = DEPTH // 2
RMS_EPS = 1e-6
HEADS = 16
HEAD_DIM = 64
MLA_NOPE = 64
MLA_ROPE = 32
HALF_ROPE = MLA_ROPE // 2
KV_RANK = 256
Q_RANK = 768
ROPE_THETA = 10000.0
D_FF = 2816

LANES = 128
SLOT = 128
AUG = HEADS * SLOT
LOG2E = math.log2(math.e)
NEG = -1e30

TM = 512
TQ = 1024
CH = TQ // 2
WQ = 256
ACC_ROWS = HEAD_DIM + 16
FF_CHUNK = 256
VMEM_LIMIT = 56 * 1024 * 1024


def _dot(a, b):
    return jnp.dot(a, b, preferred_element_type=F32)


def _rms(x, g):
    inv = lax.rsqrt(jnp.mean(x * x, axis=-1, keepdims=True) + RMS_EPS)
    return (x * inv) * g


def _split3(x):
    hi = x.astype(BF16)
    r = x - hi.astype(F32)
    mid = r.astype(BF16)
    lo = (r - mid.astype(F32)).astype(BF16)
    return hi, mid, lo


def _const_spec(shape):
    nd = len(shape)
    return pl.BlockSpec(shape, lambda *_: (0,) * nd, pipeline_mode=pl.Buffered(1))


def _params(sem):
    return pltpu.CompilerParams(dimension_semantics=sem, vmem_limit_bytes=VMEM_LIMIT)


def _fox_proj_kernel(h_ref, g_ref, wqT_ref, wk_ref, wvT_ref, wf_ref, bf_ref,
                     pk_ref, pqT_ref, tri_ref, qT_ref, k_ref, vT_ref, carry_ref):
    @pl.when(pl.program_id(0) == 0)
    def _():
        carry_ref[...] = jnp.zeros_like(carry_ref)

    xn = _rms(h_ref[...], g_ref[...]).astype(BF16)
    xnT = xn.T

    f = _dot(xn, wf_ref[...]) + bf_ref[...]
    kd = _dot(xn, wk_ref[...])
    qd = (_dot(wqT_ref[...], xnT) * (LOG2E / math.sqrt(HEAD_DIM))).astype(BF16)
    vT_ref[...] = _dot(wvT_ref[...], xnT).astype(BF16)
    logf = (jnp.minimum(f, 0.0) - jnp.log1p(jnp.exp(-jnp.abs(f)))) * LOG2E
    hi, mid, lo = _split3(logf)
    tri = tri_ref[...]
    cum = _dot(tri, hi) + _dot(tri, mid) + _dot(tri, lo) + carry_ref[...]
    carry_ref[...] = cum[TM - 1:TM, :]

    lane = lax.broadcasted_iota(jnp.int32, cum.shape, 1)
    c = jnp.where(lane < HEADS, cum, 1.0)
    c3 = jnp.concatenate(_split3(c), axis=1)
    c3T = jnp.concatenate(_split3(c.T), axis=0)

    kb = _dot(c3, pk_ref[...])
    low = lane < HEAD_DIM
    for g in range(HEADS // 2):
        kg = kd[:, g * LANES:(g + 1) * LANES]
        bg = kb[:, g * LANES:(g + 1) * LANES]
        k_ref[2 * g] = jnp.where(low, kg, bg).astype(BF16)
        k_ref[2 * g + 1] = jnp.where(low, bg, kg).astype(BF16)
    qb = _dot(pqT_ref[...], c3T).astype(BF16)
    for h in range(HEADS):
        val = qd[h * HEAD_DIM:(h + 1) * HEAD_DIM]
        bias = qb[h * HEAD_DIM:(h + 1) * HEAD_DIM]
        first, second = (val, bias) if h % 2 == 0 else (bias, val)
        qT_ref[h * SLOT:h * SLOT + HEAD_DIM, :] = first
        qT_ref[h * SLOT + HEAD_DIM:(h + 1) * SLOT, :] = second


def _fox_proj(h, g, wqT, wk, wvT, wf, bf, pk, pqT, tri):
    n = SEQ // TM
    return pl.pallas_call(
        _fox_proj_kernel,
        grid=(n,),
        in_specs=[
            pl.BlockSpec((TM, D_MODEL), lambda i: (i, 0)),
            _const_spec((1, D_MODEL)),
            _const_spec((D_MODEL, D_MODEL)),
            _const_spec((D_MODEL, D_MODEL)),
            _const_spec((D_MODEL, D_MODEL)),
            _const_spec((D_MODEL, LANES)),
            _const_spec((1, LANES)),
            _const_spec((3 * LANES, D_MODEL)),
            _const_spec((D_MODEL, 3 * LANES)),
            _const_spec((TM, TM)),
        ],
        out_specs=[
            pl.BlockSpec((AUG, TM), lambda i: (0, i)),
            pl.BlockSpec((HEADS, TM, SLOT), lambda i: (0, i, 0)),
            pl.BlockSpec((D_MODEL, TM), lambda i: (0, i)),
        ],
        out_shape=[
            jax.ShapeDtypeStruct((AUG, SEQ), BF16),
            jax.ShapeDtypeStruct((HEADS, SEQ, SLOT), BF16),
            jax.ShapeDtypeStruct((D_MODEL, SEQ), BF16),
        ],
        scratch_shapes=[pltpu.VMEM((1, LANES), F32)],
        compiler_params=_params(("arbitrary",)),
        name="fox_proj",
    )(h, g, wqT, wk, wvT, wf, bf, pk, pqT, tri)


def _attn_kernel(qT_ref, k_ref, vT_ref, o_ref, sa_scr, sb_scr, pa_scr, pb_scr,
                 smax_a_scr, smax_b_scr, alpha_b_scr, m_scr, acc_scr):
    _attn_tile(jnp.int32(0), True, qT_ref, k_ref, vT_ref, o_ref, sa_scr, sb_scr, pa_scr, pb_scr,
               smax_a_scr, smax_b_scr, alpha_b_scr, m_scr, acc_scr)

    def tile(i, carry):
        _attn_tile(i, False, qT_ref, k_ref, vT_ref, o_ref, sa_scr, sb_scr, pa_scr, pb_scr,
                   smax_a_scr, smax_b_scr, alpha_b_scr, m_scr, acc_scr)
        return carry

    lax.fori_loop(1, SEQ // TQ, tile, 0)


def _attn_tile(i, is_first, qT_ref, k_ref, vT_ref, o_ref, sa_scr, sb_scr, pa_scr, pb_scr,
               smax_a_scr, smax_b_scr, alpha_b_scr, m_scr, acc_scr):
    groups = CH // 8
    windows = tuple(slice(j * WQ, (j + 1) * WQ) for j in range(TQ // WQ))
    half = CH // WQ
    q0 = pl.multiple_of(i * TQ, TQ)
    q_next = pl.multiple_of(jnp.minimum(i + 1, SEQ // TQ - 1) * TQ, TQ)

    def scores(c, s_ref, smax_ref, cols, mask_off=None, qbase=q0):
        rows = live_rows(mask_off)
        off = pl.multiple_of(c * CH, CH)
        q_cols = pl.ds(pl.multiple_of(qbase + cols.start, WQ), WQ)
        s = _dot(k_ref[pl.ds(off, rows), :], qT_ref[:, q_cols])
        if mask_off is not None:
            row = lax.broadcasted_iota(jnp.int32, (rows, WQ), 0)
            col = lax.broadcasted_iota(jnp.int32, (rows, WQ), 1)
            s = jnp.where(row <= col + mask_off, s, NEG)
        s_ref[:rows, cols] = s
        smax_ref[:, cols] = jnp.max(s.reshape(rows // 8, 8, WQ), axis=0)

    def live_rows(mask_off):
        return CH if mask_off is None else min(CH, mask_off + WQ)

    def probs(s_ref, p_ref, smax_ref, cols, rows=CH):
        m_old = m_scr[:, cols]
        m_new = jnp.maximum(m_old, jnp.max(smax_ref[:, cols], axis=0, keepdims=True))
        alpha = jnp.exp2(m_old - m_new)
        m_scr[:, cols] = m_new
        p_ref[:rows, cols] = jnp.exp2(s_ref[:rows, cols] - m_new).astype(BF16)
        return alpha

    ones_rows = jnp.ones((ACC_ROWS - HEAD_DIM, CH), BF16)

    def pv(c, p_ref, alpha, cols, rows=CH):
        off = pl.multiple_of(c * CH, CH)
        lhs = jnp.concatenate([vT_ref[:, pl.ds(off, rows)], ones_rows[:, :rows]], axis=0)
        acc_scr[:, cols] = alpha * acc_scr[:, cols] + _dot(lhs, p_ref[:rows, cols])

    def next_scores(t, nxt, chunk_b, j, cols):
        s_ref, smax_ref = (sb_scr, smax_b_scr) if chunk_b else (sa_scr, smax_a_scr)
        if nxt == 'plain':
            scores(2 * t + 2 + chunk_b, s_ref, smax_ref, cols)
        elif nxt == 'tile':
            scores(chunk_b, s_ref, smax_ref, cols, None, q_next)
        elif not chunk_b:
            scores(2 * t + 2, s_ref, smax_ref, cols, j * WQ if j < half else None)
        elif j >= half:
            scores(2 * t + 3, s_ref, smax_ref, cols, (j - half) * WQ)

    def pair(t, nxt, first=False, diag=False):
        for j, cols in enumerate(windows):
            rows_a = live_rows(j * WQ if j < half else None) if diag else CH
            rows_b = live_rows((j - half) * WQ) if diag and j >= half else CH
            alpha_a = probs(sa_scr, pa_scr, smax_a_scr, cols, rows_a)
            next_scores(t, nxt, 0, j, cols)
            if not first:
                pv(jnp.maximum(2 * t - 1, 0), pb_scr, alpha_b_scr[:, cols], cols)
            if not (diag and j < half):
                alpha_b_scr[:, cols] = probs(sb_scr, pb_scr, smax_b_scr, cols, rows_b)
            next_scores(t, nxt, 1, j, cols)
            pv(2 * t, pa_scr, alpha_a, cols, rows_a)

    m_scr[...] = jnp.full(m_scr.shape, NEG, F32)
    acc_scr[...] = jnp.zeros(acc_scr.shape, F32)
    if is_first:
        for j, cols in enumerate(windows):
            scores(0, sa_scr, smax_a_scr, cols, j * WQ if j < half else None)
            if j >= half:
                scores(1, sb_scr, smax_b_scr, cols, (j - half) * WQ)
    else:
        pb_scr[...] = jnp.zeros(pb_scr.shape, BF16)
        alpha_b_scr[...] = jnp.ones(alpha_b_scr.shape, F32)

        def body(t, carry):
            pair(t, 'plain')
            return carry

        lax.fori_loop(0, i - 1, body, 0)
        pair(i - 1, 'diag')
    pair(i, 'tile', first=is_first, diag=True)
    for j, cols in enumerate(windows[half:]):
        pv(2 * i + 1, pb_scr, alpha_b_scr[:, cols], cols, live_rows(j * WQ))
    l = acc_scr[HEAD_DIM:HEAD_DIM + 1, :]
    o_ref[:, pl.ds(q0, TQ)] = (acc_scr[:HEAD_DIM, :] / l).astype(BF16)


def _attention(qT, k, vT):
    return pl.pallas_call(
        _attn_kernel,
        grid=(HEADS,),
        in_specs=[
            pl.BlockSpec((None, SLOT, SEQ), lambda h: (h, 0, 0)),
            pl.BlockSpec((None, SEQ, SLOT), lambda h: (h, 0, 0)),
            pl.BlockSpec((HEAD_DIM, SEQ), lambda h: (h, 0)),
        ],
        out_specs=pl.BlockSpec((HEAD_DIM, SEQ), lambda h: (h, 0)),
        out_shape=jax.ShapeDtypeStruct((HEADS * HEAD_DIM, SEQ), BF16),
        scratch_shapes=[
            pltpu.VMEM((CH, TQ), F32),
            pltpu.VMEM((CH, TQ), F32),
            pltpu.VMEM((CH, TQ), BF16),
            pltpu.VMEM((CH, TQ), BF16),
            pltpu.VMEM((8, TQ), F32),
            pltpu.VMEM((8, TQ), F32),
            pltpu.VMEM((1, TQ), F32),
            pltpu.VMEM((1, TQ), F32),
            pltpu.VMEM((ACC_ROWS, TQ), F32),
        ],
        compiler_params=_params(("arbitrary",)),
        name="attention",
    )(qT, k, vT)


def _out_ffn_kernel(oT_ref, h_ref, wo_ref, g_ref, wg_ref, wu_ref, wd_ref, gfin_ref,
                    out_ref, *, final):
    o = oT_ref[...].T
    h1 = h_ref[...] + _dot(o, wo_ref[...])
    xn = _rms(h1, g_ref[...]).astype(BF16)
    acc = h1
    for c in range(D_FF // FF_CHUNK):
        sl = slice(c * FF_CHUNK, (c + 1) * FF_CHUNK)
        gate = _dot(xn, wg_ref[:, sl])
        up = _dot(xn, wu_ref[:, sl])
        act = (gate * (1.0 / (1.0 + jnp.exp(-gate))) * up).astype(BF16)
        acc = acc + _dot(act, wd_ref[sl, :])
    if final:
        acc = _rms(acc, gfin_ref[...])
    out_ref[...] = acc


def _out_ffn(oT, h, wo, g, wg, wu, wd, gfin, final):
    n = SEQ // TM
    return pl.pallas_call(
        functools.partial(_out_ffn_kernel, final=final),
        grid=(n,),
        in_specs=[
            pl.BlockSpec((D_MODEL, TM), lambda i: (0, i)),
            pl.BlockSpec((TM, D_MODEL), lambda i: (i, 0)),
            _const_spec((D_MODEL, D_MODEL)),
            _const_spec((1, D_MODEL)),
            _const_spec((D_MODEL, D_FF)),
            _const_spec((D_MODEL, D_FF)),
            _const_spec((D_FF, D_MODEL)),
            _const_spec((1, D_MODEL)),
        ],
        out_specs=pl.BlockSpec((TM, D_MODEL), lambda i: (i, 0)),
        out_shape=jax.ShapeDtypeStruct((SEQ, D_MODEL), F32),
        compiler_params=_params(("arbitrary",)),
        name="out_ffn",
    )(oT, h, wo, g, wg, wu, wd, gfin)


def _rope_table_kernel(pos_ref, invf_ref, cos_ref, sin_ref, cosT_ref, sinT_ref):
    ang = pos_ref[...] * invf_ref[...]
    cos = jnp.cos(ang)
    sin = jnp.sin(ang)
    cos_ref[...] = cos
    sin_ref[...] = sin
    cosT_ref[...] = cos.T
    sinT_ref[...] = sin.T


def _rope_tables(pos_col, invf):
    n = SEQ // TM
    row = pl.BlockSpec((TM, LANES), lambda i: (i, 0))
    colT = pl.BlockSpec((LANES, TM), lambda i: (0, i))
    return pl.pallas_call(
        _rope_table_kernel,
        grid=(n,),
        in_specs=[pl.BlockSpec((TM, 1), lambda i: (i, 0)), _const_spec((1, LANES))],
        out_specs=[row, row, colT, colT],
        out_shape=[jax.ShapeDtypeStruct((SEQ, LANES), F32)] * 2
        + [jax.ShapeDtypeStruct((LANES, SEQ), F32)] * 2,
        compiler_params=_params(("arbitrary",)),
        name="rope_tables",
    )(pos_col, invf)


def _mla_kv_kernel(h_ref, g_ref, wa_ref, gckv_ref, cos_ref, sin_ref, wuk_ref,
                   p1_ref, p2_ref, wuvT_ref, k_ref, vT_ref):
    tp = TM // 2
    halves = (slice(0, tp), slice(tp, TM))
    a_raw = [_dot(_rms(h_ref[rows, :], g_ref[...]).astype(BF16), wa_ref[...]) for rows in halves]
    for rows, a in zip(halves, a_raw):
        c = _rms(a[:, :KV_RANK], gckv_ref[...]).astype(BF16)
        x1 = a[:, KV_RANK:KV_RANK + LANES]
        x2 = a[:, KV_RANK + LANES:]
        cos = cos_ref[rows, :]
        sin = sin_ref[rows, :]
        r1 = (x1 * cos - x2 * sin).astype(BF16)
        r2 = (x1 * sin + x2 * cos).astype(BF16)
        kd = _dot(c, wuk_ref[...])
        rb = _dot(r1, p1_ref[...]) + _dot(r2, p2_ref[...])
        low = lax.broadcasted_iota(jnp.int32, rb.shape, 1) < MLA_NOPE
        for g in range(HEADS // 2):
            kg = kd[:, g * LANES:(g + 1) * LANES]
            k_ref[2 * g, rows, :] = jnp.where(low, kg, rb).astype(BF16)
            k_ref[2 * g + 1, rows, :] = jnp.where(low, rb, kg).astype(BF16)
        vT_ref[:, rows] = _dot(wuvT_ref[...], c.T).astype(BF16)


def _mla_kv(h, g, wa, gckv, cos, sin, wuk, p1, p2, wuvT):
    n = SEQ // TM
    return pl.pallas_call(
        _mla_kv_kernel,
        grid=(n,),
        in_specs=[
            pl.BlockSpec((TM, D_MODEL), lambda i: (i, 0)),
            _const_spec((1, D_MODEL)),
            _const_spec((D_MODEL, 4 * LANES)),
            _const_spec((1, KV_RANK)),
            pl.BlockSpec((TM, LANES), lambda i: (i, 0)),
            pl.BlockSpec((TM, LANES), lambda i: (i, 0)),
            _const_spec((KV_RANK, D_MODEL)),
            _const_spec((LANES, LANES)),
            _const_spec((LANES, LANES)),
            _const_spec((D_MODEL, KV_RANK)),
        ],
        out_specs=[
            pl.BlockSpec((HEADS, TM, SLOT), lambda i: (0, i, 0)),
            pl.BlockSpec((D_MODEL, TM), lambda i: (0, i)),
        ],
        out_shape=[
            jax.ShapeDtypeStruct((HEADS, SEQ, SLOT), BF16),
            jax.ShapeDtypeStruct((D_MODEL, SEQ), BF16),
        ],
        compiler_params=_params(("arbitrary",)),
        name="mla_kv",
    )(h, g, wa, gckv, cos, sin, wuk, p1, p2, wuvT)


def _mla_q_kernel(h_ref, g_ref, wdq_ref, gcq_ref, wnT_ref, wx1T_ref, wx2T_ref,
                  cosT_ref, sinT_ref, qT_ref):
    scale = LOG2E / math.sqrt(MLA_NOPE + MLA_ROPE)
    tp = TM // 2
    halves = (slice(0, tp), slice(tp, TM))
    cq_raw = [_dot(_rms(h_ref[rows, :], g_ref[...]).astype(BF16), wdq_ref[...]) for rows in halves]
    for rows, raw in zip(halves, cq_raw):
        cqT = _rms(raw, gcq_ref[...]).astype(BF16).T
        nT = _dot(wnT_ref[...], cqT) * scale
        x1T = _dot(wx1T_ref[...], cqT)
        x2T = _dot(wx2T_ref[...], cqT)
        cosT = jnp.concatenate([cosT_ref[:, rows]] * HEADS, axis=0)
        sinT = jnp.concatenate([sinT_ref[:, rows]] * HEADS, axis=0)
        r1 = (x1T * cosT - x2T * sinT) * scale
        r2 = (x1T * sinT + x2T * cosT) * scale
        zeros = jnp.zeros((SLOT - MLA_NOPE - MLA_ROPE, tp), BF16)
        for h in range(HEADS):
            nope0, rope0 = (0, MLA_NOPE) if h % 2 == 0 else (SLOT - MLA_NOPE, 0)
            qT_ref[h, nope0:nope0 + MLA_NOPE, rows] = (
                nT[h * MLA_NOPE:(h + 1) * MLA_NOPE].astype(BF16))
            qT_ref[h, rope0:rope0 + HALF_ROPE, rows] = (
                r1[h * HALF_ROPE:(h + 1) * HALF_ROPE].astype(BF16))
            qT_ref[h, rope0 + HALF_ROPE:rope0 + MLA_ROPE, rows] = (
                r2[h * HALF_ROPE:(h + 1) * HALF_ROPE].astype(BF16))
            qT_ref[h, rope0 + MLA_ROPE:rope0 + SLOT - MLA_NOPE, rows] = zeros


def _mla_q(h, g, wdq, gcq, wnT, wx1T, wx2T, cosT, sinT):
    n = SEQ // TM
    return pl.pallas_call(
        _mla_q_kernel,
        grid=(n,),
        in_specs=[
            pl.BlockSpec((TM, D_MODEL), lambda i: (i, 0)),
            _const_spec((1, D_MODEL)),
            _const_spec((D_MODEL, Q_RANK)),
            _const_spec((1, Q_RANK)),
            _const_spec((HEADS * MLA_NOPE, Q_RANK)),
            _const_spec((HEADS * HALF_ROPE, Q_RANK)),
            _const_spec((HEADS * HALF_ROPE, Q_RANK)),
            pl.BlockSpec((HALF_ROPE, TM), lambda i: (0, i)),
            pl.BlockSpec((HALF_ROPE, TM), lambda i: (0, i)),
        ],
        out_specs=pl.BlockSpec((HEADS, SLOT, TM), lambda i: (0, 0, i)),
        out_shape=jax.ShapeDtypeStruct((HEADS, SLOT, SEQ), BF16),
        compiler_params=_params(("arbitrary",)),
        name="mla_q",
    )(h, g, wdq, gcq, wnT, wx1T, wx2T, cosT, sinT)


def _fox_bias_placement():
    w = HEADS * HEAD_DIM
    pk = np.zeros((3 * LANES, w), np.float32)
    for h in range(HEADS):
        base = (h // 2) * LANES + (HEAD_DIM if h % 2 == 0 else 0)
        for j in range(3):
            pk[HEADS, base + j] = 1.0
            pk[j * LANES + h, base + 3 + j] = -1.0
    pq = np.zeros((w, 3 * LANES), np.float32)
    for h in range(HEADS):
        for j in range(3):
            pq[h * HEAD_DIM + j, j * LANES + h] = 1.0
            pq[h * HEAD_DIM + 3 + j, HEADS] = 1.0
    return jnp.asarray(pk, BF16), jnp.asarray(pq, BF16)


def _rope_placement():
    p1 = np.zeros((LANES, LANES), np.float32)
    p2 = np.zeros((LANES, LANES), np.float32)
    for base in (0, MLA_NOPE):
        for j in range(HALF_ROPE):
            p1[j, base + j] = 1.0
            p2[j, base + HALF_ROPE + j] = 1.0
    return jnp.asarray(p1, BF16), jnp.asarray(p2, BF16)


def kernel(x, positions, attn_norm, ffn_norm, w_gate, w_up, w_down, fox_w_in, fox_b_f,
           fox_w_o, kv_norm, w_kv_a, ckv_norm, w_uk, w_uv, mla_w_dq, cq_norm, mla_w_uq,
           mla_w_o, final_norm):
    assert x.shape == (1, SEQ, D_MODEL)
    h = x.reshape(SEQ, D_MODEL)
    W = HEADS * HEAD_DIM
    pk, pqT = _fox_bias_placement()
    p1, p2 = _rope_placement()
    tri = jnp.asarray(np.tril(np.ones((TM, TM), np.float32)), BF16)
    row = lambda v: v.reshape(1, -1)

    def ffn_args(l):
        return (row(ffn_norm[l]), w_gate[l].astype(BF16), w_up[l].astype(BF16),
                w_down[l].astype(BF16), row(final_norm))

    for l in range(N_A):
        w_in = fox_w_in[l]
        wqT = w_in[:, :W].T.astype(BF16)
        wk = w_in[:, W:2 * W].astype(BF16)
        wvT = w_in[:, 2 * W:3 * W].T.astype(BF16)
        wf = jnp.pad(w_in[:, 3 * W:], ((0, 0), (0, LANES - HEADS))).astype(BF16)
        bf = jnp.pad(fox_b_f[l], (0, LANES - HEADS)).reshape(1, LANES)
        qT, k, vT = _fox_proj(h, row(attn_norm[l]), wqT, wk, wvT, wf, bf, pk, pqT, tri)
        oT = _attention(qT.reshape(HEADS, SLOT, SEQ), k, vT)
        h = _out_ffn(oT, h, fox_w_o[l].astype(BF16), *ffn_args(l), final=False)

    invf = ROPE_THETA ** (-jnp.arange(0, HALF_ROPE, dtype=F32) * 2.0 / MLA_ROPE)
    invf = jnp.pad(invf, (0, LANES - HALF_ROPE)).reshape(1, LANES)
    pos_col = positions.reshape(SEQ, 1).astype(F32)
    cos, sin, cosT, sinT = _rope_tables(pos_col, invf)
    wa = jnp.concatenate([
        w_kv_a[:, :KV_RANK],
        jnp.pad(w_kv_a[:, KV_RANK:KV_RANK + HALF_ROPE], ((0, 0), (0, LANES - HALF_ROPE))),
        jnp.pad(w_kv_a[:, KV_RANK + HALF_ROPE:], ((0, 0), (0, LANES - HALF_ROPE))),
    ], axis=1).astype(BF16)
    wuk = w_uk.reshape(KV_RANK, HEADS * MLA_NOPE).astype(BF16)
    wuvT = w_uv.reshape(KV_RANK, HEADS * HEAD_DIM).T.astype(BF16)
    k_sh, vT_sh = _mla_kv(h, row(kv_norm), wa, row(ckv_norm), cos, sin, wuk, p1, p2, wuvT)

    for l in range(N_A, DEPTH):
        j = l - N_A
        wuq = mla_w_uq[j].reshape(Q_RANK, HEADS, MLA_NOPE + MLA_ROPE)
        wnT = wuq[:, :, :MLA_NOPE].reshape(Q_RANK, -1).T.astype(BF16)
        wx1T = wuq[:, :, MLA_NOPE:MLA_NOPE + HALF_ROPE].reshape(Q_RANK, -1).T.astype(BF16)
        wx2T = wuq[:, :, MLA_NOPE + HALF_ROPE:].reshape(Q_RANK, -1).T.astype(BF16)
        qT = _mla_q(h, row(attn_norm[l]), mla_w_dq[j].astype(BF16), row(cq_norm[j]),
                    wnT, wx1T, wx2T, cosT, sinT)
        oT = _attention(qT, k_sh, vT_sh)
        h = _out_ffn(oT, h, mla_w_o[j].astype(BF16), *ffn_args(l), final=(l == DEPTH - 1))

    return h.reshape(1, SEQ, D_MODEL)
```
